```python
import jax, jax.numpy as jnp
from jax import lax
import numpy as np

D_MODEL = 1024
BATCH = 1
SEQ = 16384
DEPTH = 1
DEC_BATCH = 2
DEC_SEQ = 8192
PAST_LEN = 128

HEAD_DIM = 64
N_SLOTS = 8
DILATION_GROUPS = ((128, 1), (512, 4), (2048, 16))
N_GROUPS = 3
N_ATT_HEADS = N_GROUPS * N_SLOTS
ATT_WIDTH = N_ATT_HEADS * HEAD_DIM
ATT_OUT_WIDTH = N_SLOTS * HEAD_DIM
CONV_WIDTH = D_MODEL
CONV_K = 3
N_BRANCHES = 2
IN_WIDTH = 3 * ATT_WIDTH + 3 * CONV_WIDTH + N_BRANCHES * D_MODEL
SPLIT_POINTS = (ATT_WIDTH, 2 * ATT_WIDTH, 3 * ATT_WIDTH,
                3 * ATT_WIDTH + CONV_WIDTH, 3 * ATT_WIDTH + 2 * CONV_WIDTH,
                3 * ATT_WIDTH + 3 * CONV_WIDTH, 3 * ATT_WIDTH + 3 * CONV_WIDTH + D_MODEL)
N_EXPERTS = 16
EXPERT_FF = 2048
CAPACITY_FACTOR = 2
RMS_EPS = 1e-6

kernel_name = 'hybrid_dilated_attn_shortconv_ec_moe_encoder'


def rms_norm(x, g):
    xf = x.astype(jnp.float32)
    y = xf * lax.rsqrt(jnp.mean(xf * xf, axis=-1, keepdims=True) + RMS_EPS)
    return (y * g.astype(jnp.float32)).astype(x.dtype)


def alibi_slopes():
    s = 2.0 ** (-8.0 * (np.arange(N_ATT_HEADS, dtype=np.float32) + 1.0) / N_ATT_HEADS)
    return jnp.asarray(s, jnp.float32).reshape(N_GROUPS, N_SLOTS)


def dilated_window_attention(q, k, v, slopes, window, dilation):
    b, s, h, dh = q.shape
    r = dilation
    half = window // (2 * r)
    qb = half
    m = s // r
    nb = -(-m // qb)
    mp = nb * qb

    def split(t):
        return t.reshape(b, m, r, h, dh).transpose(0, 2, 3, 1, 4)

    qs = jnp.pad(split(q), ((0, 0), (0, 0), (0, 0), (0, mp - m), (0, 0)))
    qs = qs.reshape(b, r, h, nb, qb, dh)

    def windows(t):
        t = jnp.pad(split(t), ((0, 0), (0, 0), (0, 0), (half, mp - m + half), (0, 0)))
        t = t.reshape(b, r, h, nb + 2, qb, dh)
        return jnp.concatenate([t[:, :, :, :-2], t[:, :, :, 1:-1], t[:, :, :, 2:]], axis=4)

    kw, vw = windows(k), windows(v)
    scale = HEAD_DIM ** -0.5
    sc = jnp.einsum('brhnqd,brhnkd->brhnqk', qs, kw).astype(jnp.float32) * scale
    off = jnp.arange(3 * qb)[None, :] - half - jnp.arange(qb)[:, None]
    band = jnp.abs(off) <= half
    key_pos = jnp.arange(nb)[:, None] * qb - half + jnp.arange(3 * qb)[None, :]
    valid = (key_pos >= 0) & (key_pos < m)
    dist = (r * jnp.abs(off)).astype(jnp.float32)
    bias = -slopes.astype(jnp.float32)[:, None, None] * dist[None]
    sc = sc + bias[None, None, :, None]
    mask = band[None] & valid[:, None, :]
    sc = jnp.where(mask, sc, -jnp.inf)
    lse = jax.nn.logsumexp(sc, axis=-1)
    p = jnp.exp(sc - lse[..., None]).astype(v.dtype)
    o = jnp.einsum('brhnqk,brhnkd->brhnqd', p, vw)
    o = o.reshape(b, r, h, mp, dh)[:, :, :, :m].transpose(0, 3, 1, 2, 4).reshape(b, s, h, dh)
    lse = lse.reshape(b, r, h, mp)[..., :m].transpose(0, 3, 1, 2).reshape(b, s, h)
    return o, lse


def head_rms(t, g):
    tf = t.astype(jnp.float32)
    y = tf * lax.rsqrt(jnp.mean(tf * tf, axis=-1, keepdims=True) + RMS_EPS)
    return (y * g.astype(jnp.float32)).astype(t.dtype)


def token_mix(x, mix_norm_g, w_in, q_norm_g, k_norm_g, conv_w, w_attn_branch, w_conv_branch, w_out):
    b, s, _ = x.shape
    hn = rms_norm(x, mix_norm_g)
    proj = hn @ w_in
    q, k, v, cb, cc, cx, ga, gc = jnp.split(proj, SPLIT_POINTS, axis=-1)
    shp = (b, s, N_GROUPS, N_SLOTS, HEAD_DIM)
    q = head_rms(q.reshape(shp), q_norm_g)
    k = head_rms(k.reshape(shp), k_norm_g)
    v = v.reshape(shp)
    slopes = alibi_slopes()
    outs, lses = [], []
    for gi, (window, dilation) in enumerate(DILATION_GROUPS):
        o, l = dilated_window_attention(q[:, :, gi], k[:, :, gi], v[:, :, gi], slopes[gi], window, dilation)
        outs.append(o)
        lses.append(l)
    wts = jax.nn.softmax(jnp.stack(lses, axis=0), axis=0).astype(x.dtype)
    att = jnp.einsum('gbsh,gbshd->bshd', wts, jnp.stack(outs, axis=0)).reshape(b, s, ATT_OUT_WIDTH)
    att_branch = att @ w_attn_branch
    u = cc * cx
    up = jnp.pad(u, ((0, 0), (1, 1), (0, 0)))
    cy = up[:, :-2] * conv_w[0] + up[:, 1:-1] * conv_w[1] + up[:, 2:] * conv_w[2]
    conv_branch = (cb * cy) @ w_conv_branch
    merged = jax.nn.sigmoid(ga) * att_branch + jax.nn.sigmoid(gc) * conv_branch
    return merged @ w_out


def expert_choice_ffn(x, ffn_norm_g, w_router, w_gate, w_up, w_down):
    b, s, d = x.shape
    n = b * s
    cap = CAPACITY_FACTOR * n // N_EXPERTS
    hn = rms_norm(x, ffn_norm_g).reshape(n, d)
    aff = jax.nn.softmax((hn @ w_router).astype(jnp.float32), axis=-1)
    gate, idx = lax.top_k(aff.T, cap)
    xe = hn[idx]
    hid = jax.nn.silu(jnp.einsum('ecd,edf->ecf', xe, w_gate)) * jnp.einsum('ecd,edf->ecf', xe, w_up)
    ye = jnp.einsum('ecf,efd->ecd', hid, w_down) * gate[..., None].astype(x.dtype)
    out = jnp.zeros((n, d), x.dtype).at[idx.reshape(-1)].add(ye.reshape(-1, d))
    return out.reshape(b, s, d)


def encoder_layer(x, mix_norm_g, w_in, q_norm_g, k_norm_g, conv_w, w_attn_branch, w_conv_branch,
                  w_out, ffn_norm_g, w_router, w_gate, w_up, w_down):
    x = x + token_mix(x, mix_norm_g, w_in, q_norm_g, k_norm_g, conv_w, w_attn_branch, w_conv_branch, w_out)
    return x + expert_choice_ffn(x, ffn_norm_g, w_router, w_gate, w_up, w_down)


def setup_inputs(seed: int = 0) -> dict:
    key = jax.random.key(seed)
    ks = jax.random.split(key, 16)
    f32 = jnp.float32
    nrm = lambda kk, shape, sc: jax.random.normal(kk, shape, f32) * sc
    return {
        'x_prompt': nrm(ks[0], (BATCH, SEQ, D_MODEL), 1.0),
        'x_sample': nrm(ks[1], (DEC_BATCH, DEC_SEQ, D_MODEL), 1.0),
        'mix_norm_g': 1.0 + nrm(ks[2], (DEPTH, D_MODEL), 0.02),
        'w_in': nrm(ks[3], (DEPTH, D_MODEL, IN_WIDTH), D_MODEL ** -0.5),
        'q_norm_g': 1.0 + nrm(ks[4], (DEPTH, HEAD_DIM), 0.02),
        'k_norm_g': 1.0 + nrm(ks[5], (DEPTH, HEAD_DIM), 0.02),
        'conv_w': nrm(ks[6], (DEPTH, CONV_K, CONV_WIDTH), 0.5),
        'w_attn_branch': nrm(ks[7], (DEPTH, ATT_OUT_WIDTH, D_MODEL), ATT_OUT_WIDTH ** -0.5),
        'w_conv_branch': nrm(ks[8], (DEPTH, CONV_WIDTH, D_MODEL), CONV_WIDTH ** -0.5),
        'w_out': nrm(ks[9], (DEPTH, D_MODEL, D_MODEL), D_MODEL ** -0.5),
        'ffn_norm_g': 1.0 + nrm(ks[10], (DEPTH, D_MODEL), 0.02),
        'w_router': nrm(ks[11], (DEPTH, D_MODEL, N_EXPERTS), D_MODEL ** -0.5),
        'w_gate': nrm(ks[12], (DEPTH, N_EXPERTS, D_MODEL, EXPERT_FF), D_MODEL ** -0.5),
        'w_up': nrm(ks[13], (DEPTH, N_EXPERTS, D_MODEL, EXPERT_FF), D_MODEL ** -0.5),
        'w_down': nrm(ks[14], (DEPTH, N_EXPERTS, EXPERT_FF, D_MODEL), EXPERT_FF ** -0.5),
    }


def reference(x_prompt, x_sample, mix_norm_g, w_in, q_norm_g, k_norm_g, conv_w, w_attn_branch,
              w_conv_branch, w_out, ffn_norm_g, w_router, w_gate, w_up, w_down):
    y_prompt = x_prompt
    y_sample = x_sample
    for layer in range(DEPTH):
        y_prompt = encoder_layer(y_prompt, mix_norm_g[layer], w_in[layer], q_norm_g[layer], k_norm_g[layer],
                                 conv_w[layer], w_attn_branch[layer], w_conv_branch[layer], w_out[layer],
                                 ffn_norm_g[layer], w_router[layer], w_gate[layer], w_up[layer], w_down[layer])
        y_sample = encoder_layer(y_sample, mix_norm_g[layer], w_in[layer], q_norm_g[layer], k_norm_g[layer],
                                 conv_w[layer], w_attn_branch[layer], w_conv_branch[layer], w_out[layer],
                                 ffn_norm_g[layer], w_router[layer], w_gate[layer], w_up[layer], w_down[layer])
    return (y_prompt, y_sample)
```

```python
import functools

import numpy as np
import jax
import jax.numpy as jnp
from jax import lax
from jax.experimental import pallas as pl
from jax.experimental.pallas import tpu as pltpu

F32 = jnp.float32
BF16 = jnp.bfloat16
I32 = jnp.int32
U32 = jnp.uint32

D_MODEL = 1024
HEAD_DIM = 64
N_SLOTS = 8
DILATIONS = (1, 4, 16)
KEYS_PER_SIDE = 64
GROUP_W = N_SLOTS * HEAD_DIM
ATT_W = len(DILATIONS) * GROUP_W
CONV_W = D_MODEL
IN_W = 3 * ATT_W + 3 * CONV_W + 2 * D_MODEL
N_EXPERTS = 16
EXPERT_FF = 2048
CAPACITY_FACTOR = 2
RMS_EPS = 1e-6
NEG = -1e30

LANES = 128
PAIR_SLABS = GROUP_W // LANES
VMEM_LIMIT = 58 * 1024 * 1024

PROJ_TM = 256
PROJ_HALO = 16
ATT_TT = 2048
ATT_TQ = 128
FFN_FC = 512
FFN_RB = 512
DISP_SB = 2048
DISP_W = 64
COMB_W = 80


def _cparams(sem):
    return pltpu.CompilerParams(dimension_semantics=sem, vmem_limit_bytes=VMEM_LIMIT)


def _any_eq(i, values):
    return functools.reduce(jnp.logical_or, [i == v for v in values])


def _const_spec(shape):
    nd = len(shape)
    return pl.BlockSpec(shape, lambda *a: (0,) * nd, pipeline_mode=pl.Buffered(1))


def _proj_body(x_ref, xp_ref, xn_ref, ng_ref, w_ref, qg_ref, kg_ref, cw_ref, wcb_ref,
               q_ref, k_ref, v_ref, sga_ref, gcb_ref, *, tm, halo, first_tiles, last_tiles):
    i = pl.program_id(0)
    is_first = _any_eq(i, first_tiles)
    is_last = _any_eq(i, last_tiles)

    x_ext = jnp.concatenate([xp_ref[...], x_ref[...], xn_ref[...]], axis=0)
    ms = jnp.mean(x_ext * x_ext, axis=-1, keepdims=True)
    hn_ext = (x_ext * lax.rsqrt(ms + RMS_EPS) * ng_ref[...]).astype(BF16)
    hn = hn_ext[halo:halo + tm]

    def proj(lhs, a, b):
        return jnp.dot(lhs, w_ref[:, a:b], preferred_element_type=F32)

    lo_lane = lax.broadcasted_iota(I32, (1, LANES), 1) < HEAD_DIM

    def head_norm_store(p, g_ref, out_ref, scale):
        for s in range(ATT_W // LANES):
            xs = p[:, s * LANES:(s + 1) * LANES]
            x2 = xs * xs
            tot = jnp.sum(x2, axis=-1, keepdims=True)
            lo = jnp.sum(jnp.where(lo_lane, x2, 0.0), axis=-1, keepdims=True)
            hi = tot - lo
            r = jnp.where(lo_lane, lax.rsqrt(lo * (1.0 / HEAD_DIM) + RMS_EPS),
                          lax.rsqrt(hi * (1.0 / HEAD_DIM) + RMS_EPS))
            y = xs * r * g_ref[:, s * LANES:(s + 1) * LANES]
            if scale != 1.0:
                y = y * scale
            out_ref[:, s * LANES:(s + 1) * LANES] = y.astype(out_ref.dtype)

    head_norm_store(proj(hn, 0, ATT_W), qg_ref, q_ref, HEAD_DIM ** -0.5)
    head_norm_store(proj(hn, ATT_W, 2 * ATT_W), kg_ref, k_ref, 1.0)
    v_ref[...] = proj(hn, 2 * ATT_W, 3 * ATT_W).astype(v_ref.dtype)

    c0 = 3 * ATT_W
    cb = proj(hn, c0, c0 + CONV_W)
    cc = proj(hn_ext, c0 + CONV_W, c0 + 2 * CONV_W)
    cx = proj(hn_ext, c0 + 2 * CONV_W, c0 + 3 * CONV_W)
    u_ext = cc * cx
    row = lax.broadcasted_iota(I32, (tm + 2 * halo, 1), 0)
    keep = jnp.logical_and(jnp.logical_or(row >= halo, jnp.logical_not(is_first)),
                           jnp.logical_or(row < halo + tm, jnp.logical_not(is_last)))
    u_ext = jnp.where(keep, u_ext, 0.0)
    cy = (u_ext[halo - 1:halo - 1 + tm] * cw_ref[0:1, :]
          + u_ext[halo:halo + tm] * cw_ref[1:2, :]
          + u_ext[halo + 1:halo + 1 + tm] * cw_ref[2:3, :])
    conv_in = (cb * cy).astype(BF16)
    cbr = jnp.dot(conv_in, wcb_ref[...], preferred_element_type=F32)
    g0 = c0 + 3 * CONV_W
    sga_ref[...] = jax.nn.sigmoid(proj(hn, g0, g0 + D_MODEL))
    gcb_ref[...] = jax.nn.sigmoid(proj(hn, g0 + D_MODEL, g0 + 2 * D_MODEL)) * cbr


def _proj_call(x, seqs, ng, w_in, qg, kg, cw, wcb):
    n = x.shape[0]
    tm, halo = PROJ_TM, PROJ_HALO
    nt = n // tm
    hb = tm // halo
    first_tiles = tuple(s // tm for s, _ in seqs)
    last_tiles = tuple((s + l) // tm - 1 for s, l in seqs)
    nhb = n // halo
    body = functools.partial(_proj_body, tm=tm, halo=halo, first_tiles=first_tiles, last_tiles=last_tiles)
    return pl.pallas_call(
        body,
        grid=(nt,),
        in_specs=[
            pl.BlockSpec((tm, D_MODEL), lambda i: (i, 0)),
            pl.BlockSpec((halo, D_MODEL), lambda i: (jnp.maximum(i * hb - 1, 0), 0)),
            pl.BlockSpec((halo, D_MODEL), lambda i: (jnp.minimum((i + 1) * hb, nhb - 1), 0)),
            _const_spec((1, D_MODEL)),
            _const_spec((D_MODEL, IN_W)),
            _const_spec((1, ATT_W)),
            _const_spec((1, ATT_W)),
            _const_spec((3, CONV_W)),
            _const_spec((CONV_W, D_MODEL)),
        ],
        out_specs=[
            pl.BlockSpec((tm, ATT_W), lambda i: (i, 0)),
            pl.BlockSpec((tm, ATT_W), lambda i: (i, 0)),
            pl.BlockSpec((tm, ATT_W), lambda i: (i, 0)),
            pl.BlockSpec((tm, D_MODEL), lambda i: (i, 0)),
            pl.BlockSpec((tm, D_MODEL), lambda i: (i, 0)),
        ],
        out_shape=[
            jax.ShapeDtypeStruct((n, ATT_W), BF16),
            jax.ShapeDtypeStruct((n, ATT_W), BF16),
            jax.ShapeDtypeStruct((n, ATT_W), BF16),
            jax.ShapeDtypeStruct((n, D_MODEL), F32),
            jax.ShapeDtypeStruct((n, D_MODEL), F32),
        ],
        compiler_params=_cparams(("arbitrary",)),
        name="proj",
    )(x, x, x, ng, w_in, qg, kg, cw, wcb)


def _attn_body(q_ref, kp_ref, kc_ref, kn_ref, vp_ref, vc_ref, vn_ref, bias_ref, o_ref,
               qs, kw, vw, *, r, tt, first_tiles, last_tiles):
    i = pl.program_id(0)
    is_first = _any_eq(i, first_tiles)
    is_last = _any_eq(i, last_tiles)
    h = KEYS_PER_SIDE * r
    tq = ATT_TQ
    kwin = tq + 2 * KEYS_PER_SIDE
    nq = tt // (tq * r)
    packed = r > 1

    def fill(dst, s, p_ref, c_ref, n_ref):
        sl = slice(s * LANES, (s + 1) * LANES)
        if packed:
            dst[s, 0:h // 2, :] = pltpu.bitcast(p_ref[:, sl], U32)
            dst[s, h // 2:(h + tt) // 2, :] = pltpu.bitcast(c_ref[:, sl], U32)
            dst[s, (h + tt) // 2:(2 * h + tt) // 2, :] = pltpu.bitcast(n_ref[:, sl], U32)
        else:
            dst[s, 0:h, :] = p_ref[:, sl]
            dst[s, h:h + tt, :] = c_ref[:, sl]
            dst[s, h + tt:2 * h + tt, :] = n_ref[:, sl]

    for s in range(PAIR_SLABS):
        fill(kw, s, kp_ref, kc_ref, kn_ref)
        fill(vw, s, vp_ref, vc_ref, vn_ref)
        if packed:
            qs[s, :, :] = pltpu.bitcast(q_ref[:, s * LANES:(s + 1) * LANES], U32)

    lane = lax.broadcasted_iota(I32, (1, LANES), 1)
    lo_lane = lane < HEAD_DIM
    col = lax.broadcasted_iota(I32, (1, kwin), 1)

    def split(u):
        lo = pltpu.bitcast(u << 16, F32).astype(BF16)
        hi = pltpu.bitcast(u & jnp.uint32(0xFFFF0000), F32).astype(BF16)
        return lo, hi

    def head_pair(q2, k2, v2, biases):
        o_acc = None
        lse2 = None
        for half in range(2):
            mask = lo_lane if half == 0 else jnp.logical_not(lo_lane)
            qm = jnp.where(mask, q2, jnp.zeros_like(q2))
            sc = lax.dot_general(qm, k2, (((1,), (1,)), ((), ())), preferred_element_type=F32)
            sc = sc + biases[half]
            m = jnp.max(sc, axis=-1, keepdims=True)
            p = jnp.exp(sc - m)
            l = jnp.sum(p, axis=-1, keepdims=True)
            vm = jnp.where(mask, v2, jnp.zeros_like(v2))
            pv = jnp.dot(p.astype(BF16), vm, preferred_element_type=F32)
            contrib = pv * (1.0 / l)
            lse = m + jnp.log(l)
            o_acc = contrib if o_acc is None else o_acc + contrib
            lse2 = jnp.broadcast_to(lse, (tq, LANES)) if lse2 is None else jnp.where(mask, lse, lse2)
        return o_acc, lse2

    npairs = max(r // 2, 1)

    def step(it, carry):
        qb = it // npairs
        pr = it % npairs
        pen_lo = jnp.where(jnp.logical_and(is_first, qb == 0), NEG, 0.0)
        pen_hi = jnp.where(jnp.logical_and(is_last, qb == nq - 1), NEG, 0.0)
        edge = (jnp.where(col < KEYS_PER_SIDE, pen_lo, 0.0)
                + jnp.where(col >= KEYS_PER_SIDE + tq, pen_hi, 0.0))
        biases = [bias_ref[hh] + edge for hh in range(N_SLOTS)]
        for s in range(PAIR_SLABS):
            if packed:
                st = r // 2
                qu = qs[s, pl.ds(qb * (tq * st) + pr, tq, stride=st), :]
                ku = kw[s, pl.ds(qb * (tq * st) + pr, kwin, stride=st), :]
                vu = vw[s, pl.ds(qb * (tq * st) + pr, kwin, stride=st), :]
                parts = list(zip(split(qu), split(ku), split(vu)))
            else:
                q0 = pl.multiple_of(qb * tq, tq)
                parts = [(q_ref[pl.ds(q0, tq), s * LANES:(s + 1) * LANES],
                          kw[s, pl.ds(q0, kwin), :], vw[s, pl.ds(q0, kwin), :])]
            for ci, (q2, k2, v2) in enumerate(parts):
                o2, lse2 = head_pair(q2, k2, v2, biases[2 * s:2 * s + 2])
                if packed:
                    rows = pl.ds(qb * (tq * r) + 2 * pr + ci, tq, stride=r)
                else:
                    rows = pl.ds(pl.multiple_of(qb * tq, tq), tq)
                o_ref[s, rows, :] = o2
                o_ref[PAIR_SLABS + s, rows, :] = lse2
        return carry

    lax.fori_loop(0, nq * npairs, step, 0)


def _alibi_bias(gi, r):
    n_heads = len(DILATIONS) * N_SLOTS
    slopes = (2.0 ** (-8.0 * (np.arange(n_heads, dtype=np.float32) + 1.0) / n_heads)).astype(np.float32)
    slopes = slopes.reshape(len(DILATIONS), N_SLOTS)[gi]
    qi = np.arange(ATT_TQ)[:, None]
    kj = np.arange(ATT_TQ + 2 * KEYS_PER_SIDE)[None, :]
    off = kj - KEYS_PER_SIDE - qi
    dist = (r * np.abs(off)).astype(np.float32)
    bias = -slopes[:, None, None] * dist[None]
    bias = np.where((np.abs(off) <= KEYS_PER_SIDE)[None], bias, np.float32(NEG))
    return jnp.asarray(bias, F32)


def _attn_call(q, k, v, seqs, gi):
    n = q.shape[0]
    r = DILATIONS[gi]
    tt = ATT_TT
    h = KEYS_PER_SIDE * r
    nt = n // tt
    hb = tt // h
    nhb = n // h
    first_tiles = tuple(s // tt for s, _ in seqs)
    last_tiles = tuple((s + l) // tt - 1 for s, l in seqs)
    kwin = ATT_TQ + 2 * KEYS_PER_SIDE
    body = functools.partial(_attn_body, r=r, tt=tt, first_tiles=first_tiles, last_tiles=last_tiles)
    cur = pl.BlockSpec((tt, GROUP_W), lambda i: (i, gi))
    prev = pl.BlockSpec((h, GROUP_W), lambda i: (jnp.maximum(i * hb - 1, 0), gi))
    nxt = pl.BlockSpec((h, GROUP_W), lambda i: (jnp.minimum((i + 1) * hb, nhb - 1), gi))
    if r > 1:
        scratch = [pltpu.VMEM((PAIR_SLABS, tt // 2, LANES), U32),
                   pltpu.VMEM((PAIR_SLABS, (tt + 2 * h) // 2, LANES), U32),
                   pltpu.VMEM((PAIR_SLABS, (tt + 2 * h) // 2, LANES), U32)]
    else:
        scratch = [pltpu.VMEM((8, LANES), U32),
                   pltpu.VMEM((PAIR_SLABS, tt + 2 * h, LANES), BF16),
                   pltpu.VMEM((PAIR_SLABS, tt + 2 * h, LANES), BF16)]
    return pl.pallas_call(
        body,
        grid=(nt,),
        in_specs=[cur, prev, cur, nxt, prev, cur, nxt,
                  _const_spec((N_SLOTS, ATT_TQ, kwin))],
        out_specs=pl.BlockSpec((2 * PAIR_SLABS, tt, LANES), lambda i: (0, i, 0)),
        out_shape=jax.ShapeDtypeStruct((2 * PAIR_SLABS, n, LANES), F32),
        scratch_shapes=scratch,
        compiler_params=_cparams(("arbitrary",)),
        name=f"attn_d{r}",
    )(q, k, k, k, v, v, v, _alibi_bias(gi, r))


def _split_bf16(x):
    hi = x.astype(BF16)
    lo = (x - hi.astype(F32)).astype(BF16)
    return hi, lo


def _merge_body(o0_ref, o1_ref, o2_ref, sga_ref, gcb_ref, x_ref, wab_ref, wout_ref, g2_ref, wrh_ref, wrl_ref,
                x1_ref, hn_ref, aff_ref, afft_ref, *, tm):
    slabs = []
    for s in range(PAIR_SLABS):
        l0, l1, l2 = o0_ref[PAIR_SLABS + s], o1_ref[PAIR_SLABS + s], o2_ref[PAIR_SLABS + s]
        mx = jnp.maximum(jnp.maximum(l0, l1), l2)
        e0, e1, e2 = jnp.exp(l0 - mx), jnp.exp(l1 - mx), jnp.exp(l2 - mx)
        num = e0 * o0_ref[s] + e1 * o1_ref[s] + e2 * o2_ref[s]
        slabs.append(num / (e0 + e1 + e2))
    att = jnp.concatenate(slabs, axis=-1).astype(BF16)
    a_br = jnp.dot(att, wab_ref[...], preferred_element_type=F32)
    merged = (sga_ref[...] * a_br + gcb_ref[...]).astype(BF16)
    x1 = x_ref[...] + jnp.dot(merged, wout_ref[...], preferred_element_type=F32)
    x1_ref[...] = x1
    ms = jnp.mean(x1 * x1, axis=-1, keepdims=True)
    hn2 = x1 * lax.rsqrt(ms + RMS_EPS) * g2_ref[...]
    hn_ref[...] = hn2.astype(BF16)
    hh, hl = _split_bf16(hn2)
    logits = (jnp.dot(hh, wrh_ref[...], preferred_element_type=F32)
              + jnp.dot(hl, wrh_ref[...], preferred_element_type=F32)
              + jnp.dot(hh, wrl_ref[...], preferred_element_type=F32))
    lane = lax.broadcasted_iota(I32, (1, LANES), 1)
    logits = jnp.where(lane < N_EXPERTS, logits, NEG)
    e = jnp.exp(logits - jnp.max(logits, axis=-1, keepdims=True))
    aff = e / jnp.sum(e, axis=-1, keepdims=True)
    aff_ref[...] = aff
    aff_t = aff.T
    for c in range(tm // LANES):
        afft_ref[c] = aff_t[0:N_EXPERTS, c * LANES:(c + 1) * LANES]


def _merge_call(o0, o1, o2, sga, gcb, x, wab, wout, g2, wrh, wrl):
    n = x.shape[0]
    tm = PROJ_TM
    nt = n // tm
    body = functools.partial(_merge_body, tm=tm)
    ospec = pl.BlockSpec((2 * PAIR_SLABS, tm, LANES), lambda i: (0, i, 0))
    rowspec = pl.BlockSpec((tm, D_MODEL), lambda i: (i, 0))
    return pl.pallas_call(
        body,
        grid=(nt,),
        in_specs=[ospec, ospec, ospec, rowspec, rowspec, rowspec,
                  _const_spec((GROUP_W, D_MODEL)), _const_spec((D_MODEL, D_MODEL)),
                  _const_spec((1, D_MODEL)), _const_spec((D_MODEL, LANES)), _const_spec((D_MODEL, LANES))],
        out_specs=[rowspec, rowspec,
                   pl.BlockSpec((tm, LANES), lambda i: (i, 0)),
                   pl.BlockSpec((tm // LANES, N_EXPERTS, LANES), lambda i: (i, 0, 0))],
        out_shape=[jax.ShapeDtypeStruct((n, D_MODEL), F32),
                   jax.ShapeDtypeStruct((n, D_MODEL), BF16),
                   jax.ShapeDtypeStruct((n, LANES), F32),
                   jax.ShapeDtypeStruct((n // LANES, N_EXPERTS, LANES), F32)],
        compiler_params=_cparams(("arbitrary",)),
        name="merge",
    )(o0, o1, o2, sga, gcb, x, wab, wout, g2, wrh, wrl)


def _route_body(aff_ref, tri_ref, ones_ref, low_ref, slot_ref, offs_ref, rs_scr, off_scr, *, nch, cap):
    a = aff_ref[...]
    xi = lax.bitcast_convert_type(a, I32)

    def count(mask):
        c = jnp.sum(mask.astype(F32), axis=0, keepdims=True)
        return jnp.sum(c, axis=2, keepdims=True)

    cur = jnp.zeros((1, N_EXPERTS, 1), I32)
    for bit in range(30, -1, -1):
        cand = cur | (1 << bit)
        cur = jnp.where(count(xi >= cand) >= cap, cand, cur)
    gt = xi > cur
    eq = xi == cur
    need = cap - count(gt)

    def prefix(mask):
        m = mask.astype(BF16).reshape(nch * N_EXPERTS, LANES)
        incl = jnp.dot(m, tri_ref[...], preferred_element_type=F32).reshape(nch, N_EXPERTS, LANES)
        rs_scr[...] = jnp.dot(m, ones_ref[...], preferred_element_type=F32).reshape(nch, N_EXPERTS, LANES)
        for e in range(N_EXPERTS):
            off_scr[:, e, :] = jnp.dot(low_ref[...], rs_scr[:, e, :].astype(BF16), preferred_element_type=F32)
        off = off_scr[...]
        return incl + off - mask.astype(F32), off

    tie_rank, _ = prefix(eq)
    sel = jnp.logical_or(gt, jnp.logical_and(eq, tie_rank < need))
    pos, off = prefix(sel)
    slot_ref[...] = jnp.where(sel, pos, -1.0).astype(I32)
    offs_ref[...] = off.astype(I32)


def _route_call(afft, n_groups, cap):
    nchunks = afft.shape[0]
    nch = nchunks // n_groups
    tri = jnp.asarray(np.triu(np.ones((LANES, LANES), np.float32)), BF16)
    ones = jnp.ones((LANES, LANES), BF16)
    low = jnp.asarray(np.tril(np.ones((nch, nch), np.float32), -1), BF16)
    body = functools.partial(_route_body, nch=nch, cap=cap)
    blk = pl.BlockSpec((nch, N_EXPERTS, LANES), lambda b: (b, 0, 0))
    return pl.pallas_call(
        body,
        grid=(n_groups,),
        in_specs=[blk, _const_spec((LANES, LANES)), _const_spec((LANES, LANES)), _const_spec((nch, nch))],
        out_specs=[blk, blk],
        out_shape=[jax.ShapeDtypeStruct((nchunks, N_EXPERTS, LANES), I32),
                   jax.ShapeDtypeStruct((nchunks, N_EXPERTS, LANES), I32)],
        scratch_shapes=[pltpu.VMEM((nch, N_EXPERTS, LANES), F32), pltpu.VMEM((nch, N_EXPERTS, LANES), F32)],
        compiler_params=_cparams(("arbitrary",)),
        name="route",
    )(afft, tri, ones, low)


def _dispatch_body(starts_ref, ends_ref, hn_ref, slot_ref, x_ref, acc, *, tm, sb, cap, tiles_per_group):
    b, e, j = pl.program_id(0), pl.program_id(1), pl.program_id(2)
    nsb = pl.num_programs(2)
    w = DISP_W

    @pl.when(j == 0)
    def _():
        acc[...] = jnp.zeros_like(acc)

    iota_w = lax.broadcasted_iota(I32, (w, 1), 0)
    for t in range(sb // tm):
        tile = j * (sb // tm) + t
        idx = (b * tiles_per_group + tile) * N_EXPERTS + e
        start = starts_ref[idx]
        end = ends_ref[idx]
        a = (start // 8) * 8
        nrounds = (end - a + w - 1) // w
        srow = jnp.concatenate([slot_ref[t * (tm // LANES) + c, pl.ds(e, 1), :]
                                for c in range(tm // LANES)], axis=1)
        hn_t = hn_ref[t * tm:(t + 1) * tm, :]

        def round_(k, carry):
            base = pl.multiple_of(a + k * w, 8)
            oh = (srow - base == iota_w).astype(BF16)
            acc[pl.ds(base, w), :] += jnp.dot(oh, hn_t, preferred_element_type=F32)
            return carry

        lax.fori_loop(0, nrounds, round_, 0)

    @pl.when(j == nsb - 1)
    def _():
        x_ref[...] = acc[0:cap, :].astype(x_ref.dtype)


def _dispatch_call(starts, ends, hn, slot, n_groups, cap):
    n = hn.shape[0]
    ng_rows = n // n_groups
    tm, sb = PROJ_TM, DISP_SB
    nsb = ng_rows // sb
    tiles_per_group = ng_rows // tm
    cps = sb // LANES
    body = functools.partial(_dispatch_body, tm=tm, sb=sb, cap=cap, tiles_per_group=tiles_per_group)
    grid_spec = pltpu.PrefetchScalarGridSpec(
        num_scalar_prefetch=2,
        grid=(n_groups, N_EXPERTS, nsb),
        in_specs=[pl.BlockSpec((sb, D_MODEL), lambda b, e, j, *_: (b * nsb + j, 0)),
                  pl.BlockSpec((cps, N_EXPERTS, LANES), lambda b, e, j, *_: (b * nsb + j, 0, 0))],
        out_specs=pl.BlockSpec((pl.Squeezed(), pl.Squeezed(), cap, D_MODEL), lambda b, e, j, *_: (b, e, 0, 0)),
        scratch_shapes=[pltpu.VMEM((cap + DISP_W, D_MODEL), F32)],
    )
    return pl.pallas_call(
        body,
        grid_spec=grid_spec,
        out_shape=jax.ShapeDtypeStruct((n_groups, N_EXPERTS, cap, D_MODEL), BF16),
        compiler_params=_cparams(("arbitrary", "arbitrary", "arbitrary")),
        name="dispatch",
    )(starts, ends, hn, slot)


def _ffn_body(x_ref, wg_ref, wu_ref, wd_ref, y_ref, acc, *, cap):
    f = pl.program_id(2)
    nf = pl.num_programs(2)
    wg = wg_ref[...].astype(BF16)
    wu = wu_ref[...].astype(BF16)
    wd = wd_ref[...].astype(BF16)
    for rb in range(cap // FFN_RB):
        rows = slice(rb * FFN_RB, (rb + 1) * FFN_RB)
        xb = x_ref[rows, :]
        g = jnp.dot(xb, wg, preferred_element_type=F32)
        u = jnp.dot(xb, wu, preferred_element_type=F32)
        hid = (g * jax.nn.sigmoid(g) * u).astype(BF16)
        part = jnp.dot(hid, wd, preferred_element_type=F32)

        @pl.when(f == 0)
        def _():
            acc[rows, :] = part

        @pl.when(f > 0)
        def _():
            acc[rows, :] += part

    @pl.when(f == nf - 1)
    def _():
        y_ref[...] = acc[...].astype(y_ref.dtype)


def _ffn_call(xe, w_gate, w_up, w_down):
    n_groups, _, cap, _ = xe.shape
    fc = FFN_FC
    nf = EXPERT_FF // fc
    body = functools.partial(_ffn_body, cap=cap)
    sq = pl.Squeezed()
    return pl.pallas_call(
        body,
        grid=(n_groups, N_EXPERTS, nf),
        in_specs=[pl.BlockSpec((sq, sq, cap, D_MODEL), lambda b, e, f: (b, e, 0, 0)),
                  pl.BlockSpec((sq, D_MODEL, fc), lambda b, e, f: (e, 0, f)),
                  pl.BlockSpec((sq, D_MODEL, fc), lambda b, e, f: (e, 0, f)),
                  pl.BlockSpec((sq, fc, D_MODEL), lambda b, e, f: (e, f, 0))],
        out_specs=pl.BlockSpec((sq, sq, cap, D_MODEL), lambda b, e, f: (b, e, 0, 0)),
        out_shape=jax.ShapeDtypeStruct(xe.shape, BF16),
        scratch_shapes=[pltpu.VMEM((cap, D_MODEL), F32)],
        compiler_params=_cparams(("arbitrary", "arbitrary", "arbitrary")),
        name="ffn",
    )(xe, w_gate, w_up, w_down)


def _combine_body(astart_ref, nrounds_ref, x1_ref, aff_ref, slot_ref, y_hbm, out_ref,
                  ybuf, sem, yext, sem_ext, *, tm, cap, tiles_per_group):
    i = pl.program_id(0)
    nt = pl.num_programs(0)
    w = COMB_W

    def window_copy(tile, e, k, dst, dsem):
        a0 = astart_ref[tile * N_EXPERTS + e]
        a = pl.multiple_of(jnp.minimum(a0 + k * w, cap - w), 16)
        return pltpu.make_async_copy(y_hbm.at[tile // tiles_per_group, e, pl.ds(a, w), :], dst, dsem)

    def first_copy(tile, e):
        sl = tile % 2
        return window_copy(tile, e, 0, ybuf.at[sl, e], sem.at[sl, e])

    @pl.when(i == 0)
    def _():
        for e in range(N_EXPERTS):
            first_copy(i, e).start()

    @pl.when(i + 1 < nt)
    def _():
        for e in range(N_EXPERTS):
            first_copy(i + 1, e).start()

    iota_w = lax.broadcasted_iota(I32, (w, 1), 0)

    def expand(e, k, ywin, acc):
        a0 = astart_ref[i * N_EXPERTS + e]
        lo = a0 + k * w
        a = jnp.minimum(lo, cap - w)
        srow = jnp.concatenate([slot_ref[c, e:e + 1, :] for c in range(tm // LANES)], axis=1)
        oh = jnp.logical_and(srow - a == iota_w, srow >= lo).astype(BF16)
        rows = lax.dot_general(oh, ywin, (((0,), (0,)), ((), ())), preferred_element_type=F32)
        return acc + aff_ref[:, e:e + 1] * rows

    acc = x1_ref[...]
    for e in range(N_EXPERTS):
        first_copy(i, e).wait()
        acc = expand(e, 0, ybuf[i % 2, e], acc)
    out_ref[...] = acc

    def extra_round(k, carry):
        for e in range(N_EXPERTS):
            window_copy(i, e, k, yext.at[e], sem_ext.at[e]).start()
        acc = out_ref[...]
        for e in range(N_EXPERTS):
            window_copy(i, e, k, yext.at[e], sem_ext.at[e]).wait()
            acc = expand(e, k, yext[e], acc)
        out_ref[...] = acc
        return carry

    lax.fori_loop(1, nrounds_ref[i], extra_round, 0)


def _combine_call(astart, nrounds, x1, aff, slot, y, n_groups, cap):
    n = x1.shape[0]
    tm = PROJ_TM
    nt = n // tm
    tiles_per_group = nt // n_groups
    body = functools.partial(_combine_body, tm=tm, cap=cap, tiles_per_group=tiles_per_group)
    grid_spec = pltpu.PrefetchScalarGridSpec(
        num_scalar_prefetch=2,
        grid=(nt,),
        in_specs=[pl.BlockSpec((tm, D_MODEL), lambda i, *_: (i, 0)),
                  pl.BlockSpec((tm, LANES), lambda i, *_: (i, 0)),
                  pl.BlockSpec((tm // LANES, N_EXPERTS, LANES), lambda i, *_: (i, 0, 0)),
                  pl.BlockSpec(memory_space=pl.ANY)],
        out_specs=pl.BlockSpec((tm, D_MODEL), lambda i, *_: (i, 0)),
        scratch_shapes=[pltpu.VMEM((2, N_EXPERTS, COMB_W, D_MODEL), BF16),
                        pltpu.SemaphoreType.DMA((2, N_EXPERTS)),
                        pltpu.VMEM((N_EXPERTS, COMB_W, D_MODEL), BF16),
                        pltpu.SemaphoreType.DMA((N_EXPERTS,))],
    )
    return pl.pallas_call(
        body,
        grid_spec=grid_spec,
        out_shape=jax.ShapeDtypeStruct((n, D_MODEL), F32),
        compiler_params=_cparams(("arbitrary",)),
        name="combine",
    )(astart, nrounds, x1, aff, slot, y)


def _layer(x, seqs, groups, mix_norm_g, w_in, q_norm_g, k_norm_g, conv_w, w_attn_branch, w_conv_branch,
           w_out, ffn_norm_g, w_router, w_gate, w_up, w_down):
    n = x.shape[0]
    n_groups = len(groups)
    group_rows = groups[0][1]
    cap = CAPACITY_FACTOR * group_rows // N_EXPERTS
    tm = PROJ_TM

    qg = jnp.tile(q_norm_g, ATT_W // HEAD_DIM)[None, :]
    kg = jnp.tile(k_norm_g, ATT_W // HEAD_DIM)[None, :]
    q, k, v, sga, gcb = _proj_call(x, seqs, mix_norm_g[None, :], w_in.astype(BF16), qg, kg, conv_w,
                                   w_conv_branch.astype(BF16))
    o = [_attn_call(q, k, v, seqs, gi) for gi in range(len(DILATIONS))]

    wr = jnp.pad(w_router, ((0, 0), (0, LANES - N_EXPERTS)))
    wrh = wr.astype(BF16)
    wrl = (wr - wrh.astype(F32)).astype(BF16)
    x1, hn2, aff, afft = _merge_call(o[0], o[1], o[2], sga, gcb, x, w_attn_branch.astype(BF16),
                                     w_out.astype(BF16), ffn_norm_g[None, :], wrh, wrl)

    slot, offs = _route_call(afft, n_groups, cap)

    cpt = tm // LANES
    tiles_per_group = group_rows // tm
    starts = offs[::cpt, :, 0].reshape(n_groups, tiles_per_group, N_EXPERTS)
    ends = jnp.concatenate([starts[:, 1:], jnp.full((n_groups, 1, N_EXPERTS), cap, I32)], axis=1)
    xe = _dispatch_call(starts.reshape(-1), ends.reshape(-1), hn2, slot, n_groups, cap)
    y = _ffn_call(xe, w_gate, w_up, w_down)

    astart = jnp.minimum((starts // 16) * 16, cap - COMB_W)
    nrounds = jnp.maximum(jnp.max((ends - astart + COMB_W - 1) // COMB_W, axis=-1), 1)
    return _combine_call(astart.reshape(-1), nrounds.reshape(-1), x1, aff, slot, y, n_groups, cap)


def _run(x_prompt, x_sample, params):
    bp, sp, d = x_prompt.shape
    bs, ss, _ = x_sample.shape
    seqs, pos = [], 0
    for b, s in ((bp, sp), (bs, ss)):
        for _ in range(b):
            seqs.append((pos, s))
            pos += s
    groups = ((0, bp * sp), (bp * sp, bs * ss))
    assert groups[0][1] == groups[1][1], "request groups are routed with one capacity"
    x = jnp.concatenate([x_prompt.reshape(bp * sp, d), x_sample.reshape(bs * ss, d)], axis=0)
    depth = params[0].shape[0]
    for layer in range(depth):
        x = _layer(x, tuple(seqs), groups, *[p[layer] for p in params])
    return x[:bp * sp].reshape(bp, sp, d), x[bp * sp:].reshape(bs, ss, d)


def kernel(x_prompt, x_sample, mix_norm_g, w_in, q_norm_g, k_norm_g, conv_w, w_attn_branch, w_conv_branch, w_out,
           ffn_norm_g, w_router, w_gate, w_up, w_down):
    params = (mix_norm_g, w_in, q_norm_g, k_norm_g, conv_w, w_attn_branch, w_conv_branch, w_out,
              ffn_norm_g, w_router, w_gate, w_up, w_down)
    return _run(x_prompt, x_sample, params)
```

```python
import functools

import numpy as np
import jax
import jax.numpy as jnp
from jax import lax
from jax.experimental import pallas as pl
from jax.experimental.pallas import tpu as pltpu

F32 = jnp.float32
BF16 = jnp.bfloat16
I32 = jnp.int32

D_MODEL = 1024
HEAD_DIM = 64
N_SLOTS = 8
DILATIONS = (1, 4, 16)
KEYS_PER_SIDE = 64
GROUP_W = N_SLOTS * HEAD_DIM
ATT_W = len(DILATIONS) * GROUP_W
CONV_W = D_MODEL
IN_W = 3 * ATT_W + 3 * CONV_W + 2 * D_MODEL
N_EXPERTS = 16
EXPERT_FF = 2048
CAPACITY_FACTOR = 2
RMS_EPS = 1e-6
NEG = -1e30

LANES = 128
PAIR_SLABS = GROUP_W // LANES
VMEM_LIMIT = 58 * 1024 * 1024

PROJ_TM = 256
PROJ_HALO = 16
ATT_TB_MAX = 2048
ATT_TQ = 128
FFN_FC = 512
FFN_RB = 512
DISP_SB = 2048
DISP_EG = 4
DISP_W = 96
COMB_W = 64


def _cparams(sem):
    return pltpu.CompilerParams(dimension_semantics=sem, vmem_limit_bytes=VMEM_LIMIT)


def _any_eq(i, values):
    return functools.reduce(jnp.logical_or, [i == v for v in values])


def _const_spec(shape):
    nd = len(shape)
    return pl.BlockSpec(shape, lambda *a: (0,) * nd, pipeline_mode=pl.Buffered(1))


def _two_array_specs(block_rows, n_first_blocks, cols, prefetch=False):
    if prefetch:
        first = pl.BlockSpec((block_rows, cols), lambda i, *_: (jnp.minimum(i, n_first_blocks - 1), 0))
        second = pl.BlockSpec((block_rows, cols), lambda i, *_: (jnp.maximum(i - n_first_blocks, 0), 0))
    else:
        first = pl.BlockSpec((block_rows, cols), lambda i: (jnp.minimum(i, n_first_blocks - 1), 0))
        second = pl.BlockSpec((block_rows, cols), lambda i: (jnp.maximum(i - n_first_blocks, 0), 0))
    return first, second


def _proj_body(xa_ref, xb_ref, xpa_ref, xpb_ref, xna_ref, xnb_ref, ng_ref, w_ref, qg_ref, kg_ref, cw_ref, wcb_ref,
               q0_ref, k0_ref, v0_ref, q1_ref, k1_ref, v1_ref, q2_ref, k2_ref, v2_ref, sga_ref, gcb_ref, dscr,
               *, tm, halo, n_first, first_tiles, last_tiles):
    i = pl.program_id(0)
    is_first = _any_eq(i, first_tiles)
    is_last = _any_eq(i, last_tiles)
    in_a = i < n_first

    def pick(a_ref, b_ref):
        return jnp.where(in_a, a_ref[...], b_ref[...])

    x_ext = jnp.concatenate([pick(xpa_ref, xpb_ref), pick(xa_ref, xb_ref), pick(xna_ref, xnb_ref)], axis=0)
    ms = jnp.mean(x_ext * x_ext, axis=-1, keepdims=True)
    hn_ext = (x_ext * lax.rsqrt(ms + RMS_EPS) * ng_ref[...]).astype(BF16)
    hn = hn_ext[halo:halo + tm]

    def proj(lhs, a, b):
        return jnp.dot(lhs, w_ref[:, a:b], preferred_element_type=F32)

    lo_lane = lax.broadcasted_iota(I32, (1, LANES), 1) < HEAD_DIM

    def head_norm(p, g_ref, col0, scale):
        out = []
        for s in range(PAIR_SLABS):
            xs = p[:, s * LANES:(s + 1) * LANES]
            x2 = xs * xs
            tot = jnp.sum(x2, axis=-1, keepdims=True)
            lo = jnp.sum(jnp.where(lo_lane, x2, 0.0), axis=-1, keepdims=True)
            hi = tot - lo
            r = jnp.where(lo_lane, lax.rsqrt(lo * (1.0 / HEAD_DIM) + RMS_EPS),
                          lax.rsqrt(hi * (1.0 / HEAD_DIM) + RMS_EPS))
            y = xs * r * g_ref[:, col0 + s * LANES:col0 + (s + 1) * LANES]
            out.append(y * scale if scale != 1.0 else y)
        return out

    def emit(gi, out_ref, slabs):
        r = DILATIONS[gi]
        if r == 1:
            for s in range(PAIR_SLABS):
                out_ref[:, s * LANES:(s + 1) * LANES] = slabs[s].astype(out_ref.dtype)
            return
        for s in range(PAIR_SLABS):
            dscr[s] = slabs[s]
        for c in range(r):
            for s in range(PAIR_SLABS):
                out_ref[c, :, s * LANES:(s + 1) * LANES] = (
                    dscr[s, pl.ds(c, tm // r, stride=r), :].astype(out_ref.dtype))

    q_refs, k_refs, v_refs = (q0_ref, q1_ref, q2_ref), (k0_ref, k1_ref, k2_ref), (v0_ref, v1_ref, v2_ref)
    for gi in range(len(DILATIONS)):
        c0 = gi * GROUP_W
        emit(gi, q_refs[gi], head_norm(proj(hn, c0, c0 + GROUP_W), qg_ref, c0, HEAD_DIM ** -0.5))
        emit(gi, k_refs[gi], head_norm(proj(hn, ATT_W + c0, ATT_W + c0 + GROUP_W), kg_ref, c0, 1.0))
        pv = proj(hn, 2 * ATT_W + c0, 2 * ATT_W + c0 + GROUP_W)
        emit(gi, v_refs[gi], [pv[:, s * LANES:(s + 1) * LANES] for s in range(PAIR_SLABS)])

    c0 = 3 * ATT_W
    cb = proj(hn, c0, c0 + CONV_W)
    cc = proj(hn_ext, c0 + CONV_W, c0 + 2 * CONV_W)
    cx = proj(hn_ext, c0 + 2 * CONV_W, c0 + 3 * CONV_W)
    u_ext = cc * cx
    row = lax.broadcasted_iota(I32, (tm + 2 * halo, 1), 0)
    keep = jnp.logical_and(jnp.logical_or(row >= halo, jnp.logical_not(is_first)),
                           jnp.logical_or(row < halo + tm, jnp.logical_not(is_last)))
    u_ext = jnp.where(keep, u_ext, 0.0)
    cy = (u_ext[halo - 1:halo - 1 + tm] * cw_ref[0:1, :]
          + u_ext[halo:halo + tm] * cw_ref[1:2, :]
          + u_ext[halo + 1:halo + 1 + tm] * cw_ref[2:3, :])
    conv_in = (cb * cy).astype(BF16)
    cbr = jnp.dot(conv_in, wcb_ref[...], preferred_element_type=F32)
    g0 = c0 + 3 * CONV_W
    sga_ref[...] = jax.nn.sigmoid(proj(hn, g0, g0 + D_MODEL))
    gcb_ref[...] = jax.nn.sigmoid(proj(hn, g0 + D_MODEL, g0 + 2 * D_MODEL)) * cbr


def _proj_call(xa, xb, seqs, ng, w_in, qg, kg, cw, wcb):
    na, nb_rows = xa.shape[0], xb.shape[0]
    n = na + nb_rows
    tm, halo = PROJ_TM, PROJ_HALO
    nt = n // tm
    n_first = na // tm
    hb = tm // halo
    first_tiles = tuple(s // tm for s, _ in seqs)
    last_tiles = tuple((s + l) // tm - 1 for s, l in seqs)
    nha, nhb = na // halo, nb_rows // halo
    body = functools.partial(_proj_body, tm=tm, halo=halo, n_first=n_first,
                             first_tiles=first_tiles, last_tiles=last_tiles)
    cur_a, cur_b = _two_array_specs(tm, n_first, D_MODEL)
    prev_a = pl.BlockSpec((halo, D_MODEL), lambda i: (jnp.clip(i * hb - 1, 0, nha - 1), 0))
    prev_b = pl.BlockSpec((halo, D_MODEL), lambda i: (jnp.clip((i - n_first) * hb - 1, 0, nhb - 1), 0))
    next_a = pl.BlockSpec((halo, D_MODEL), lambda i: (jnp.clip((i + 1) * hb, 0, nha - 1), 0))
    next_b = pl.BlockSpec((halo, D_MODEL), lambda i: (jnp.clip((i + 1 - n_first) * hb, 0, nhb - 1), 0))
    out_specs, out_shape = [], []
    for gi, r in enumerate(DILATIONS):
        for _ in range(3):
            if r == 1:
                out_specs.append(pl.BlockSpec((tm, GROUP_W), lambda i: (i, 0)))
                out_shape.append(jax.ShapeDtypeStruct((n, GROUP_W), BF16))
            else:
                out_specs.append(pl.BlockSpec((r, tm // r, GROUP_W), lambda i: (0, i, 0)))
                out_shape.append(jax.ShapeDtypeStruct((r, n // r, GROUP_W), BF16))
    out_specs += [pl.BlockSpec((tm, D_MODEL), lambda i: (i, 0))] * 2
    out_shape += [jax.ShapeDtypeStruct((n, D_MODEL), F32)] * 2
    return pl.pallas_call(
        body,
        grid=(nt,),
        in_specs=[cur_a, cur_b, prev_a, prev_b, next_a, next_b,
                  _const_spec((1, D_MODEL)), _const_spec((D_MODEL, IN_W)), _const_spec((1, ATT_W)),
                  _const_spec((1, ATT_W)), _const_spec((3, CONV_W)), _const_spec((CONV_W, D_MODEL))],
        out_specs=out_specs,
        out_shape=out_shape,
        scratch_shapes=[pltpu.VMEM((PAIR_SLABS, tm, LANES), F32)],
        compiler_params=_cparams(("arbitrary",)),
        name="proj",
    )(xa, xb, xa, xb, xa, xb, ng, w_in, qg, kg, cw, wcb)


def _attn_body(q_ref, kp_ref, kc_ref, kn_ref, vp_ref, vc_ref, vn_ref, bias_ref, o_ref, kw, vw,
               *, tb, first_tiles, last_tiles):
    i = pl.program_id(1)
    is_first = _any_eq(i, first_tiles)
    is_last = _any_eq(i, last_tiles)
    h = KEYS_PER_SIDE
    tq = ATT_TQ
    kwin = tq + 2 * h
    nq = tb // tq

    for s in range(PAIR_SLABS):
        sl = slice(s * LANES, (s + 1) * LANES)
        for dst, p_ref, c_ref, n_ref in ((kw, kp_ref, kc_ref, kn_ref), (vw, vp_ref, vc_ref, vn_ref)):
            dst[s, 0:h, :] = p_ref[:, sl]
            dst[s, h:h + tb, :] = c_ref[:, sl]
            dst[s, h + tb:2 * h + tb, :] = n_ref[:, sl]

    lane = lax.broadcasted_iota(I32, (1, LANES), 1)
    lo_lane = lane < HEAD_DIM
    col = lax.broadcasted_iota(I32, (1, kwin), 1)

    def head_pair(q2, k2, v2, biases):
        o_acc = None
        lse2 = None
        for half in range(2):
            mask = lo_lane if half == 0 else jnp.logical_not(lo_lane)
            qm = jnp.where(mask, q2, jnp.zeros_like(q2))
            sc = lax.dot_general(qm, k2, (((1,), (1,)), ((), ())), preferred_element_type=F32)
            sc = sc + biases[half]
            m = jnp.max(sc, axis=-1, keepdims=True)
            p = jnp.exp(sc - m)
            l = jnp.sum(p, axis=-1, keepdims=True)
            vm = jnp.where(mask, v2, jnp.zeros_like(v2))
            pv = jnp.dot(p.astype(BF16), vm, preferred_element_type=F32)
            contrib = pv * (1.0 / l)
            lse = m + jnp.log(l)
            o_acc = contrib if o_acc is None else o_acc + contrib
            lse2 = jnp.broadcast_to(lse, (tq, LANES)) if lse2 is None else jnp.where(mask, lse, lse2)
        return o_acc, lse2

    def step(qb, carry):
        pen_lo = jnp.where(jnp.logical_and(is_first, qb == 0), NEG, 0.0)
        pen_hi = jnp.where(jnp.logical_and(is_last, qb == nq - 1), NEG, 0.0)
        edge = (jnp.where(col < h, pen_lo, 0.0) + jnp.where(col >= h + tq, pen_hi, 0.0))
        q0 = pl.multiple_of(qb * tq, tq)
        for s in range(PAIR_SLABS):
            biases = [bias_ref[2 * s] + edge, bias_ref[2 * s + 1] + edge]
            o2, lse2 = head_pair(q_ref[pl.ds(q0, tq), s * LANES:(s + 1) * LANES],
                                 kw[s, pl.ds(q0, kwin), :], vw[s, pl.ds(q0, kwin), :], biases)
            o_ref[s, pl.ds(q0, tq), :] = o2
            o_ref[PAIR_SLABS + s, pl.ds(q0, tq), :] = lse2
        return carry

    lax.fori_loop(0, nq, step, 0)


def _alibi_bias(gi, r):
    n_heads = len(DILATIONS) * N_SLOTS
    slopes = (2.0 ** (-8.0 * (np.arange(n_heads, dtype=np.float32) + 1.0) / n_heads)).astype(np.float32)
    slopes = slopes.reshape(len(DILATIONS), N_SLOTS)[gi]
    qi = np.arange(ATT_TQ)[:, None]
    kj = np.arange(ATT_TQ + 2 * KEYS_PER_SIDE)[None, :]
    off = kj - KEYS_PER_SIDE - qi
    dist = (r * np.abs(off)).astype(np.float32)
    bias = -slopes[:, None, None] * dist[None]
    bias = np.where((np.abs(off) <= KEYS_PER_SIDE)[None], bias, np.float32(NEG))
    return jnp.asarray(bias, F32)


def _attn_call(q, k, v, seqs, gi):
    r, rows, _ = q.shape
    h = KEYS_PER_SIDE
    tb = min([ATT_TB_MAX] + [l // r for _, l in seqs])
    assert all((s // r) % tb == 0 and (l // r) % tb == 0 for s, l in seqs)
    nt = rows // tb
    hb = tb // h
    nhb = rows // h
    first_tiles = tuple((s // r) // tb for s, _ in seqs)
    last_tiles = tuple(((s + l) // r) // tb - 1 for s, l in seqs)
    kwin = ATT_TQ + 2 * h
    body = functools.partial(_attn_body, tb=tb, first_tiles=first_tiles, last_tiles=last_tiles)
    sq = pl.Squeezed()
    cur = pl.BlockSpec((sq, tb, GROUP_W), lambda c, i: (c, i, 0))
    prev = pl.BlockSpec((sq, h, GROUP_W), lambda c, i: (c, jnp.maximum(i * hb - 1, 0), 0))
    nxt = pl.BlockSpec((sq, h, GROUP_W), lambda c, i: (c, jnp.minimum((i + 1) * hb, nhb - 1), 0))
    return pl.pallas_call(
        body,
        grid=(r, nt),
        in_specs=[cur, prev, cur, nxt, prev, cur, nxt, _const_spec((N_SLOTS, ATT_TQ, kwin))],
        out_specs=pl.BlockSpec((2 * PAIR_SLABS, sq, tb, LANES), lambda c, i: (0, c, i, 0)),
        out_shape=jax.ShapeDtypeStruct((2 * PAIR_SLABS, r, rows, LANES), F32),
        scratch_shapes=[pltpu.VMEM((PAIR_SLABS, tb + 2 * h, LANES), BF16),
                        pltpu.VMEM((PAIR_SLABS, tb + 2 * h, LANES), BF16)],
        compiler_params=_cparams(("arbitrary", "arbitrary")),
        name=f"attn_d{r}",
    )(q, k, k, k, v, v, v, _alibi_bias(gi, r))


def _split_bf16(x):
    hi = x.astype(BF16)
    lo = (x - hi.astype(F32)).astype(BF16)
    return hi, lo


def _merge_body(o0_ref, o1_ref, o2_ref, sga_ref, gcb_ref, xa_ref, xb_ref, wab_ref, wout_ref, g2_ref,
                wrh_ref, wrl_ref, x1_ref, hn_ref, aff_ref, afft_ref, t1, t2, *, tm, n_first):
    i = pl.program_id(0)
    for o_ref, scr, r in ((o1_ref, t1, DILATIONS[1]), (o2_ref, t2, DILATIONS[2])):
        for s in range(2 * PAIR_SLABS):
            for c in range(r):
                scr[s, pl.ds(c, tm // r, stride=r), :] = o_ref[s, c]
    slabs = []
    for s in range(PAIR_SLABS):
        l0, l1, l2 = o0_ref[PAIR_SLABS + s, 0], t1[PAIR_SLABS + s], t2[PAIR_SLABS + s]
        mx = jnp.maximum(jnp.maximum(l0, l1), l2)
        e0, e1, e2 = jnp.exp(l0 - mx), jnp.exp(l1 - mx), jnp.exp(l2 - mx)
        num = e0 * o0_ref[s, 0] + e1 * t1[s] + e2 * t2[s]
        slabs.append(num / (e0 + e1 + e2))
    att = jnp.concatenate(slabs, axis=-1).astype(BF16)
    a_br = jnp.dot(att, wab_ref[...], preferred_element_type=F32)
    merged = (sga_ref[...] * a_br + gcb_ref[...]).astype(BF16)
    x = jnp.where(i < n_first, xa_ref[...], xb_ref[...])
    x1 = x + jnp.dot(merged, wout_ref[...], preferred_element_type=F32)
    x1_ref[...] = x1
    ms = jnp.mean(x1 * x1, axis=-1, keepdims=True)
    hn2 = x1 * lax.rsqrt(ms + RMS_EPS) * g2_ref[...]
    hn_ref[...] = hn2.astype(BF16)
    hh, hl = _split_bf16(hn2)
    logits = (jnp.dot(hh, wrh_ref[...], preferred_element_type=F32)
              + jnp.dot(hl, wrh_ref[...], preferred_element_type=F32)
              + jnp.dot(hh, wrl_ref[...], preferred_element_type=F32))
    lane = lax.broadcasted_iota(I32, (1, LANES), 1)
    logits = jnp.where(lane < N_EXPERTS, logits, NEG)
    e = jnp.exp(logits - jnp.max(logits, axis=-1, keepdims=True))
    aff = e / jnp.sum(e, axis=-1, keepdims=True)
    aff_ref[...] = aff
    aff_t = aff.T
    for c in range(tm // LANES):
        afft_ref[c] = aff_t[0:N_EXPERTS, c * LANES:(c + 1) * LANES]


def _merge_call(o0, o1, o2, sga, gcb, xa, xb, wab, wout, g2, wrh, wrl):
    n = sga.shape[0]
    tm = PROJ_TM
    nt = n // tm
    n_first = xa.shape[0] // tm
    body = functools.partial(_merge_body, tm=tm, n_first=n_first)
    ns = 2 * PAIR_SLABS
    o_specs = [pl.BlockSpec((ns, r, tm // r, LANES), lambda i: (0, 0, i, 0)) for r in DILATIONS]
    rowspec = pl.BlockSpec((tm, D_MODEL), lambda i: (i, 0))
    xa_spec, xb_spec = _two_array_specs(tm, n_first, D_MODEL)
    return pl.pallas_call(
        body,
        grid=(nt,),
        in_specs=o_specs + [rowspec, rowspec, xa_spec, xb_spec,
                            _const_spec((GROUP_W, D_MODEL)), _const_spec((D_MODEL, D_MODEL)),
                            _const_spec((1, D_MODEL)), _const_spec((D_MODEL, LANES)),
                            _const_spec((D_MODEL, LANES))],
        out_specs=[rowspec, rowspec,
                   pl.BlockSpec((tm, LANES), lambda i: (i, 0)),
                   pl.BlockSpec((tm // LANES, N_EXPERTS, LANES), lambda i: (i, 0, 0))],
        out_shape=[jax.ShapeDtypeStruct((n, D_MODEL), F32),
                   jax.ShapeDtypeStruct((n, D_MODEL), BF16),
                   jax.ShapeDtypeStruct((n, LANES), F32),
                   jax.ShapeDtypeStruct((n // LANES, N_EXPERTS, LANES), F32)],
        scratch_shapes=[pltpu.VMEM((ns, tm, LANES), F32), pltpu.VMEM((ns, tm, LANES), F32)],
        compiler_params=_cparams(("arbitrary",)),
        name="merge",
    )(o0, o1, o2, sga, gcb, xa, xb, wab, wout, g2, wrh, wrl)


def _route_body(aff_ref, tri_ref, ones_ref, low_ref, slot_ref, offs_ref, slot_tm_ref, rs_scr, off_scr,
                *, nch, cap):
    a = aff_ref[...]
    xi = lax.bitcast_convert_type(a, I32)

    def count(mask):
        c = jnp.sum(mask.astype(F32), axis=0, keepdims=True)
        return jnp.sum(c, axis=2, keepdims=True)

    cur = jnp.zeros((1, N_EXPERTS, 1), I32)
    for bit in range(30, -1, -1):
        cand = cur | (1 << bit)
        cur = jnp.where(count(xi >= cand) >= cap, cand, cur)
    gt = xi > cur
    eq = xi == cur
    need = cap - count(gt)

    def prefix(mask):
        m = mask.astype(BF16).reshape(nch * N_EXPERTS, LANES)
        incl = jnp.dot(m, tri_ref[...], preferred_element_type=F32).reshape(nch, N_EXPERTS, LANES)
        rs_scr[...] = jnp.dot(m, ones_ref[...], preferred_element_type=F32).reshape(nch, N_EXPERTS, LANES)
        for e in range(N_EXPERTS):
            off_scr[:, e, :] = jnp.dot(low_ref[...], rs_scr[:, e, :].astype(BF16), preferred_element_type=F32)
        off = off_scr[...]
        return incl + off - mask.astype(F32), off

    tie_rank, _ = prefix(eq)
    sel = jnp.logical_or(gt, jnp.logical_and(eq, tie_rank < need))
    pos, off = prefix(sel)
    slot = jnp.where(sel, pos, -1.0)
    slot_ref[...] = slot.astype(I32)
    offs_ref[...] = off.astype(I32)
    rs_scr[...] = slot
    pad = jnp.full((LANES - N_EXPERTS, LANES), -1.0, F32)

    def tr(c, carry):
        tile = jnp.concatenate([rs_scr[c], pad], axis=0)
        slot_tm_ref[pl.ds(pl.multiple_of(c * LANES, LANES), LANES), :] = tile.T.astype(I32)
        return carry

    lax.fori_loop(0, nch, tr, 0)


def _route_call(afft, n_groups, cap):
    nchunks = afft.shape[0]
    nch = nchunks // n_groups
    tri = jnp.asarray(np.triu(np.ones((LANES, LANES), np.float32)), BF16)
    ones = jnp.ones((LANES, LANES), BF16)
    low = jnp.asarray(np.tril(np.ones((nch, nch), np.float32), -1), BF16)
    body = functools.partial(_route_body, nch=nch, cap=cap)
    blk = pl.BlockSpec((nch, N_EXPERTS, LANES), lambda b: (b, 0, 0))
    return pl.pallas_call(
        body,
        grid=(n_groups,),
        in_specs=[blk, _const_spec((LANES, LANES)), _const_spec((LANES, LANES)), _const_spec((nch, nch))],
        out_specs=[blk, blk, pl.BlockSpec((nch * LANES, LANES), lambda b: (b, 0))],
        out_shape=[jax.ShapeDtypeStruct((nchunks, N_EXPERTS, LANES), I32),
                   jax.ShapeDtypeStruct((nchunks, N_EXPERTS, LANES), I32),
                   jax.ShapeDtypeStruct((nchunks * LANES, LANES), I32)],
        scratch_shapes=[pltpu.VMEM((nch, N_EXPERTS, LANES), F32), pltpu.VMEM((nch, N_EXPERTS, LANES), F32)],
        compiler_params=_cparams(("arbitrary",)),
        name="route",
    )(afft, tri, ones, low)


def _dispatch_body(starts_ref, ends_ref, hn_ref, slot_ref, x_ref, *, tm, sb, tiles_per_group):
    b, eg, j = pl.program_id(0), pl.program_id(1), pl.program_id(2)
    w = DISP_W

    @pl.when(j == 0)
    def _():
        x_ref[...] = jnp.zeros_like(x_ref)

    iota_w = lax.broadcasted_iota(I32, (w, 1), 0)
    cpt = tm // LANES
    for t in range(sb // tm):
        tile = j * (sb // tm) + t
        hn_t = hn_ref[t * tm:(t + 1) * tm, :]
        base, srows, ends = [], [], []
        for ee in range(DISP_EG):
            e = eg * DISP_EG + ee
            idx = (b * tiles_per_group + tile) * N_EXPERTS + e
            base.append(pl.multiple_of((starts_ref[idx] // 16) * 16, 16))
            ends.append(ends_ref[idx])
            srows.append(jnp.concatenate([slot_ref[t * cpt + c, pl.ds(e, 1), :] for c in range(cpt)], axis=1))
        oh = jnp.concatenate([(srows[ee] - base[ee] == iota_w).astype(BF16) for ee in range(DISP_EG)], axis=0)
        res = jnp.dot(oh, hn_t, preferred_element_type=F32)
        for ee in range(DISP_EG):
            x_ref[ee, pl.ds(base[ee], w), :] += res[ee * w:(ee + 1) * w].astype(x_ref.dtype)

            def more(k, carry, ee=ee):
                bk = pl.multiple_of(base[ee] + k * w, 16)
                ohk = (srows[ee] - bk == iota_w).astype(BF16)
                x_ref[ee, pl.ds(bk, w), :] += jnp.dot(ohk, hn_t, preferred_element_type=F32).astype(x_ref.dtype)
                return carry

            lax.fori_loop(1, (ends[ee] - base[ee] + w - 1) // w, more, 0)


def _dispatch_call(starts, ends, hn, slot, n_groups, cap):
    n = hn.shape[0]
    ng_rows = n // n_groups
    tm, sb = PROJ_TM, DISP_SB
    nsb = ng_rows // sb
    tiles_per_group = ng_rows // tm
    cps = sb // LANES
    body = functools.partial(_dispatch_body, tm=tm, sb=sb, tiles_per_group=tiles_per_group)
    sq = pl.Squeezed()
    grid_spec = pltpu.PrefetchScalarGridSpec(
        num_scalar_prefetch=2,
        grid=(n_groups, N_EXPERTS // DISP_EG, nsb),
        in_specs=[pl.BlockSpec((sb, D_MODEL), lambda b, g, j, *_: (b * nsb + j, 0)),
                  pl.BlockSpec((cps, N_EXPERTS, LANES), lambda b, g, j, *_: (b * nsb + j, 0, 0))],
        out_specs=pl.BlockSpec((sq, DISP_EG, cap + DISP_W, D_MODEL), lambda b, g, j, *_: (b, g, 0, 0)),
    )
    return pl.pallas_call(
        body,
        grid_spec=grid_spec,
        out_shape=jax.ShapeDtypeStruct((n_groups, N_EXPERTS, cap + DISP_W, D_MODEL), BF16),
        compiler_params=_cparams(("arbitrary", "arbitrary", "arbitrary")),
        name="dispatch",
    )(starts, ends, hn, slot)


def _ffn_body(x_ref, wg_ref, wu_ref, wd_ref, y_ref, acc, *, cap):
    f = pl.program_id(2)
    nf = pl.num_programs(2)
    wg = wg_ref[...].astype(BF16)
    wu = wu_ref[...].astype(BF16)
    wd = wd_ref[...].astype(BF16)
    for rb in range(cap // FFN_RB):
        rows = slice(rb * FFN_RB, (rb + 1) * FFN_RB)
        xb = x_ref[rows, :]
        g = jnp.dot(xb, wg, preferred_element_type=F32)
        u = jnp.dot(xb, wu, preferred_element_type=F32)
        hid = (g * jax.nn.sigmoid(g) * u).astype(BF16)
        part = jnp.dot(hid, wd, preferred_element_type=F32)

        @pl.when(f == 0)
        def _():
            acc[rows, :] = part

        @pl.when(f > 0)
        def _():
            acc[rows, :] += part

    @pl.when(f == nf - 1)
    def _():
        y_ref[...] = acc[...].astype(y_ref.dtype)


def _ffn_call(xe, w_gate, w_up, w_down, cap):
    n_groups = xe.shape[0]
    fc = FFN_FC
    nf = EXPERT_FF // fc
    body = functools.partial(_ffn_body, cap=cap)
    sq = pl.Squeezed()
    return pl.pallas_call(
        body,
        grid=(n_groups, N_EXPERTS, nf),
        in_specs=[pl.BlockSpec((sq, sq, cap, D_MODEL), lambda b, e, f: (b, e, 0, 0)),
                  pl.BlockSpec((sq, D_MODEL, fc), lambda b, e, f: (e, 0, f)),
                  pl.BlockSpec((sq, D_MODEL, fc), lambda b, e, f: (e, 0, f)),
                  pl.BlockSpec((sq, fc, D_MODEL), lambda b, e, f: (e, f, 0))],
        out_specs=pl.BlockSpec((sq, sq, cap, D_MODEL), lambda b, e, f: (b, e, 0, 0)),
        out_shape=jax.ShapeDtypeStruct((n_groups, N_EXPERTS, cap, D_MODEL), BF16),
        scratch_shapes=[pltpu.VMEM((cap, D_MODEL), F32)],
        compiler_params=_cparams(("arbitrary", "arbitrary", "arbitrary")),
        name="ffn",
    )(xe, w_gate, w_up, w_down)


def _combine_body(astart_ref, nrounds_ref, x1_ref, aff_ref, slot_ref, y_hbm, outa_ref, outb_ref,
                  ybuf, sem, yext, sem_ext, accs, *, tm, cap, tiles_per_group, n_first):
    i = pl.program_id(0)
    nt = pl.num_programs(0)
    w = COMB_W

    def window_copy(tile, e, k, dst, dsem):
        a0 = astart_ref[tile * N_EXPERTS + e]
        a = pl.multiple_of(jnp.minimum(a0 + k * w, cap - w), 16)
        return pltpu.make_async_copy(y_hbm.at[tile // tiles_per_group, e, pl.ds(a, w), :], dst, dsem)

    def first_copy(tile, e):
        sl = tile % 2
        return window_copy(tile, e, 0, ybuf.at[sl, pl.ds(e * w, w)], sem.at[sl, e])

    @pl.when(i == 0)
    def _():
        for e in range(N_EXPERTS):
            first_copy(i, e).start()

    @pl.when(i + 1 < nt)
    def _():
        for e in range(N_EXPERTS):
            first_copy(i + 1, e).start()

    lane = lax.broadcasted_iota(I32, (1, LANES), 1)
    lo_lane = lane < w
    slot_t = slot_ref[...]
    gate = aff_ref[...]
    g_hi = gate.astype(BF16).astype(F32)
    g_lo = gate - g_hi

    def expand(k, ywin):
        his, los = [], []
        for ep in range(N_EXPERTS // 2):
            rel, skip = [], []
            for e in (2 * ep, 2 * ep + 1):
                a0 = astart_ref[i * N_EXPERTS + e]
                lo = a0 + k * w
                a = jnp.minimum(lo, cap - w)
                rel.append(slot_t[:, e:e + 1] - a)
                skip.append(lo - a)
            fresh = lane >= jnp.where(lo_lane, skip[0], skip[1] + w)
            match = jnp.logical_and(jnp.where(lo_lane, rel[0], rel[1] + w) == lane, fresh)
            his.append(jnp.where(match, jnp.where(lo_lane, g_hi[:, 2 * ep:2 * ep + 1],
                                                  g_hi[:, 2 * ep + 1:2 * ep + 2]), 0.0).astype(BF16))
            los.append(jnp.where(match, jnp.where(lo_lane, g_lo[:, 2 * ep:2 * ep + 1],
                                                  g_lo[:, 2 * ep + 1:2 * ep + 2]), 0.0).astype(BF16))
        return (jnp.dot(jnp.concatenate(his, axis=1), ywin, preferred_element_type=F32)
                + jnp.dot(jnp.concatenate(los, axis=1), ywin, preferred_element_type=F32))

    for e in range(N_EXPERTS):
        first_copy(i, e).wait()
    accs[...] = x1_ref[...] + expand(0, ybuf[i % 2])

    def extra_round(k, carry):
        for e in range(N_EXPERTS):
            window_copy(i, e, k, yext.at[pl.ds(e * w, w)], sem_ext.at[e]).start()
        for e in range(N_EXPERTS):
            window_copy(i, e, k, yext.at[pl.ds(e * w, w)], sem_ext.at[e]).wait()
        accs[...] += expand(k, yext[...])
        return carry

    lax.fori_loop(1, nrounds_ref[i], extra_round, 0)

    @pl.when(i < n_first)
    def _():
        outa_ref[...] = accs[...]

    @pl.when(i >= n_first)
    def _():
        outb_ref[...] = accs[...]


def _combine_call(astart, nrounds, x1, aff, slot_tm, y, n_groups, cap, na):
    n = x1.shape[0]
    tm = PROJ_TM
    nt = n // tm
    n_first = na // tm
    tiles_per_group = nt // n_groups
    body = functools.partial(_combine_body, tm=tm, cap=cap, tiles_per_group=tiles_per_group, n_first=n_first)
    outa_spec, outb_spec = _two_array_specs(tm, n_first, D_MODEL, prefetch=True)
    grid_spec = pltpu.PrefetchScalarGridSpec(
        num_scalar_prefetch=2,
        grid=(nt,),
        in_specs=[pl.BlockSpec((tm, D_MODEL), lambda i, *_: (i, 0)),
                  pl.BlockSpec((tm, LANES), lambda i, *_: (i, 0)),
                  pl.BlockSpec((tm, LANES), lambda i, *_: (i, 0)),
                  pl.BlockSpec(memory_space=pl.ANY)],
        out_specs=[outa_spec, outb_spec],
        scratch_shapes=[pltpu.VMEM((2, N_EXPERTS * COMB_W, D_MODEL), BF16),
                        pltpu.SemaphoreType.DMA((2, N_EXPERTS)),
                        pltpu.VMEM((N_EXPERTS * COMB_W, D_MODEL), BF16),
                        pltpu.SemaphoreType.DMA((N_EXPERTS,)),
                        pltpu.VMEM((tm, D_MODEL), F32)],
    )
    return pl.pallas_call(
        body,
        grid_spec=grid_spec,
        out_shape=[jax.ShapeDtypeStruct((na, D_MODEL), F32), jax.ShapeDtypeStruct((n - na, D_MODEL), F32)],
        compiler_params=_cparams(("arbitrary",)),
        name="combine",
    )(astart, nrounds, x1, aff, slot_tm, y)


def _layer(xa, xb, seqs, n_groups, mix_norm_g, w_in, q_norm_g, k_norm_g, conv_w, w_attn_branch, w_conv_branch,
           w_out, ffn_norm_g, w_router, w_gate, w_up, w_down):
    na = xa.shape[0]
    n = na + xb.shape[0]
    group_rows = n // n_groups
    cap = CAPACITY_FACTOR * group_rows // N_EXPERTS
    tm = PROJ_TM

    qg = jnp.tile(q_norm_g, ATT_W // HEAD_DIM)[None, :]
    kg = jnp.tile(k_norm_g, ATT_W // HEAD_DIM)[None, :]
    outs = _proj_call(xa, xb, seqs, mix_norm_g[None, :], w_in.astype(BF16), qg, kg, conv_w,
                      w_conv_branch.astype(BF16))
    sga, gcb = outs[9], outs[10]
    o = []
    for gi, r in enumerate(DILATIONS):
        q, k, v = outs[3 * gi:3 * gi + 3]
        if r == 1:
            q, k, v = q[None], k[None], v[None]
        o.append(_attn_call(q, k, v, seqs, gi))

    wr = jnp.pad(w_router, ((0, 0), (0, LANES - N_EXPERTS)))
    wrh = wr.astype(BF16)
    wrl = (wr - wrh.astype(F32)).astype(BF16)
    x1, hn2, aff, afft = _merge_call(o[0], o[1], o[2], sga, gcb, xa, xb, w_attn_branch.astype(BF16),
                                     w_out.astype(BF16), ffn_norm_g[None, :], wrh, wrl)

    slot, offs, slot_tm = _route_call(afft, n_groups, cap)

    cpt = tm // LANES
    tiles_per_group = group_rows // tm
    starts = offs[::cpt, :, 0].reshape(n_groups, tiles_per_group, N_EXPERTS)
    ends = jnp.concatenate([starts[:, 1:], jnp.full((n_groups, 1, N_EXPERTS), cap, I32)], axis=1)
    xe = _dispatch_call(starts.reshape(-1), ends.reshape(-1), hn2, slot, n_groups, cap)
    y = _ffn_call(xe, w_gate, w_up, w_down, cap)

    astart = jnp.minimum((starts // 16) * 16, cap - COMB_W)
    nrounds = jnp.maximum(jnp.max((ends - astart + COMB_W - 1) // COMB_W, axis=-1), 1)
    return _combine_call(astart.reshape(-1), nrounds.reshape(-1), x1, aff, slot_tm, y, n_groups, cap, na)


def _run(x_prompt, x_sample, params):
    bp, sp, d = x_prompt.shape
    bs, ss, _ = x_sample.shape
    seqs, pos = [], 0
    for b, s in ((bp, sp), (bs, ss)):
        for _ in range(b):
            seqs.append((pos, s))
            pos += s
    assert bp * sp == bs * ss, "request groups are routed with one capacity"
    xa, xb = x_prompt.reshape(bp * sp, d), x_sample.reshape(bs * ss, d)
    depth = params[0].shape[0]
    for layer in range(depth):
        xa, xb = _layer(xa, xb, tuple(seqs), 2, *[p[layer] for p in params])
    return xa.reshape(bp, sp, d), xb.reshape(bs, ss, d)


def kernel(x_prompt, x_sample, mix_norm_g, w_in, q_norm_g, k_norm_g, conv_w, w_attn_branch, w_conv_branch, w_out,
           ffn_norm_g, w_router, w_gate, w_up, w_down):
    params = (mix_norm_g, w_in, q_norm_g, k_norm_g, conv_w, w_attn_branch, w_conv_branch, w_out,
              ffn_norm_g, w_router, w_gate, w_up, w_down)
    return _run(x_prompt, x_sample, params)
```

```python
import functools

import numpy as np
import jax
import jax.numpy as jnp
from jax import lax
from jax.experimental import pallas as pl
from jax.experimental.pallas import tpu as pltpu

F32 = jnp.float32
BF16 = jnp.bfloat16
I32 = jnp.int32

D_MODEL = 1024
HEAD_DIM = 64
N_SLOTS = 8
DILATIONS = (1, 4, 16)
KEYS_PER_SIDE = 64
GROUP_W = N_SLOTS * HEAD_DIM
ATT_W = len(DILATIONS) * GROUP_W
CONV_W = D_MODEL
IN_W = 3 * ATT_W + 3 * CONV_W + 2 * D_MODEL
N_EXPERTS = 16
EXPERT_FF = 2048
CAPACITY_FACTOR = 2
RMS_EPS = 1e-6
NEG = -1e30

LANES = 128
PAIR_SLABS = GROUP_W // LANES
VMEM_LIMIT = 58 * 1024 * 1024

PROJ_TM = 256
PROJ_HALO = 16
ATT_TB_MAX = 2048
ATT_TQ = 128
FFN_FC = 512
FFN_RB = 512
DISP_SB = 2048
DISP_EG = 4
DISP_W = 96
COMB_W = 64


def _cparams(sem):
    return pltpu.CompilerParams(dimension_semantics=sem, vmem_limit_bytes=VMEM_LIMIT)


def _any_eq(i, values):
    return functools.reduce(jnp.logical_or, [i == v for v in values])


def _const_spec(shape):
    nd = len(shape)
    return pl.BlockSpec(shape, lambda *a: (0,) * nd, pipeline_mode=pl.Buffered(1))


def _two_array_specs(block_rows, n_first_blocks, cols, prefetch=False):
    if prefetch:
        first = pl.BlockSpec((block_rows, cols), lambda i, *_: (jnp.minimum(i, n_first_blocks - 1), 0))
        second = pl.BlockSpec((block_rows, cols), lambda i, *_: (jnp.maximum(i - n_first_blocks, 0), 0))
    else:
        first = pl.BlockSpec((block_rows, cols), lambda i: (jnp.minimum(i, n_first_blocks - 1), 0))
        second = pl.BlockSpec((block_rows, cols), lambda i: (jnp.maximum(i - n_first_blocks, 0), 0))
    return first, second


def _proj_body(xa_ref, xb_ref, xpa_ref, xpb_ref, xna_ref, xnb_ref, ng_ref, w_ref, qg_ref, kg_ref, cw_ref, wcb_ref,
               q0_ref, k0_ref, v0_ref, q1_ref, k1_ref, v1_ref, q2_ref, k2_ref, v2_ref, sga_ref, gcb_ref, dscr,
               *, tm, halo, n_first, first_tiles, last_tiles):
    i = pl.program_id(0)
    is_first = _any_eq(i, first_tiles)
    is_last = _any_eq(i, last_tiles)
    in_a = i < n_first

    def pick(a_ref, b_ref):
        return jnp.where(in_a, a_ref[...], b_ref[...])

    x_ext = jnp.concatenate([pick(xpa_ref, xpb_ref), pick(xa_ref, xb_ref), pick(xna_ref, xnb_ref)], axis=0)
    ms = jnp.mean(x_ext * x_ext, axis=-1, keepdims=True)
    hn_ext = (x_ext * lax.rsqrt(ms + RMS_EPS) * ng_ref[...]).astype(BF16)
    hn = hn_ext[halo:halo + tm]

    def proj(lhs, a, b):
        return jnp.dot(lhs, w_ref[:, a:b], preferred_element_type=F32)

    lo_lane = lax.broadcasted_iota(I32, (1, LANES), 1) < HEAD_DIM

    def head_norm(p, g_ref, col0, scale):
        out = []
        for s in range(PAIR_SLABS):
            xs = p[:, s * LANES:(s + 1) * LANES]
            x2 = xs * xs
            tot = jnp.sum(x2, axis=-1, keepdims=True)
            lo = jnp.sum(jnp.where(lo_lane, x2, 0.0), axis=-1, keepdims=True)
            hi = tot - lo
            r = jnp.where(lo_lane, lax.rsqrt(lo * (1.0 / HEAD_DIM) + RMS_EPS),
                          lax.rsqrt(hi * (1.0 / HEAD_DIM) + RMS_EPS))
            y = xs * r * g_ref[:, col0 + s * LANES:col0 + (s + 1) * LANES]
            out.append(y * scale if scale != 1.0 else y)
        return out

    def emit(gi, out_ref, slabs):
        r = DILATIONS[gi]
        if r == 1:
            for s in range(PAIR_SLABS):
                out_ref[:, s * LANES:(s + 1) * LANES] = slabs[s].astype(out_ref.dtype)
            return
        for s in range(PAIR_SLABS):
            dscr[s] = slabs[s]
        for c in range(r):
            for s in range(PAIR_SLABS):
                out_ref[c, :, s * LANES:(s + 1) * LANES] = (
                    dscr[s, pl.ds(c, tm // r, stride=r), :].astype(out_ref.dtype))

    q_refs, k_refs, v_refs = (q0_ref, q1_ref, q2_ref), (k0_ref, k1_ref, k2_ref), (v0_ref, v1_ref, v2_ref)
    for gi in range(len(DILATIONS)):
        c0 = gi * GROUP_W
        emit(gi, q_refs[gi], head_norm(proj(hn, c0, c0 + GROUP_W), qg_ref, c0, HEAD_DIM ** -0.5))
        emit(gi, k_refs[gi], head_norm(proj(hn, ATT_W + c0, ATT_W + c0 + GROUP_W), kg_ref, c0, 1.0))
        pv = proj(hn, 2 * ATT_W + c0, 2 * ATT_W + c0 + GROUP_W)
        emit(gi, v_refs[gi], [pv[:, s * LANES:(s + 1) * LANES] for s in range(PAIR_SLABS)])

    c0 = 3 * ATT_W
    cb = proj(hn, c0, c0 + CONV_W)
    cc = proj(hn_ext, c0 + CONV_W, c0 + 2 * CONV_W)
    cx = proj(hn_ext, c0 + 2 * CONV_W, c0 + 3 * CONV_W)
    u_ext = cc * cx
    row = lax.broadcasted_iota(I32, (tm + 2 * halo, 1), 0)
    keep = jnp.logical_and(jnp.logical_or(row >= halo, jnp.logical_not(is_first)),
                           jnp.logical_or(row < halo + tm, jnp.logical_not(is_last)))
    u_ext = jnp.where(keep, u_ext, 0.0)
    cy = (u_ext[halo - 1:halo - 1 + tm] * cw_ref[0:1, :]
          + u_ext[halo:halo + tm] * cw_ref[1:2, :]
          + u_ext[halo + 1:halo + 1 + tm] * cw_ref[2:3, :])
    conv_in = (cb * cy).astype(BF16)
    cbr = jnp.dot(conv_in, wcb_ref[...], preferred_element_type=F32)
    g0 = c0 + 3 * CONV_W
    sga_ref[...] = jax.nn.sigmoid(proj(hn, g0, g0 + D_MODEL))
    gcb_ref[...] = jax.nn.sigmoid(proj(hn, g0 + D_MODEL, g0 + 2 * D_MODEL)) * cbr


def _proj_call(xa, xb, seqs, ng, w_in, qg, kg, cw, wcb):
    na, nb_rows = xa.shape[0], xb.shape[0]
    n = na + nb_rows
    tm, halo = PROJ_TM, PROJ_HALO
    nt = n // tm
    n_first = na // tm
    hb = tm // halo
    first_tiles = tuple(s // tm for s, _ in seqs)
    last_tiles = tuple((s + l) // tm - 1 for s, l in seqs)
    nha, nhb = na // halo, nb_rows // halo
    body = functools.partial(_proj_body, tm=tm, halo=halo, n_first=n_first,
                             first_tiles=first_tiles, last_tiles=last_tiles)
    cur_a, cur_b = _two_array_specs(tm, n_first, D_MODEL)
    prev_a = pl.BlockSpec((halo, D_MODEL), lambda i: (jnp.clip(i * hb - 1, 0, nha - 1), 0))
    prev_b = pl.BlockSpec((halo, D_MODEL), lambda i: (jnp.clip((i - n_first) * hb - 1, 0, nhb - 1), 0))
    next_a = pl.BlockSpec((halo, D_MODEL), lambda i: (jnp.clip((i + 1) * hb, 0, nha - 1), 0))
    next_b = pl.BlockSpec((halo, D_MODEL), lambda i: (jnp.clip((i + 1 - n_first) * hb, 0, nhb - 1), 0))
    out_specs, out_shape = [], []
    for gi, r in enumerate(DILATIONS):
        for _ in range(3):
            if r == 1:
                out_specs.append(pl.BlockSpec((tm, GROUP_W), lambda i: (i, 0)))
                out_shape.append(jax.ShapeDtypeStruct((n, GROUP_W), BF16))
            else:
                out_specs.append(pl.BlockSpec((r, tm // r, GROUP_W), lambda i: (0, i, 0)))
                out_shape.append(jax.ShapeDtypeStruct((r, n // r, GROUP_W), BF16))
    out_specs += [pl.BlockSpec((tm, D_MODEL), lambda i: (i, 0))] * 2
    out_shape += [jax.ShapeDtypeStruct((n, D_MODEL), F32)] * 2
    return pl.pallas_call(
        body,
        grid=(nt,),
        in_specs=[cur_a, cur_b, prev_a, prev_b, next_a, next_b,
                  _const_spec((1, D_MODEL)), _const_spec((D_MODEL, IN_W)), _const_spec((1, ATT_W)),
                  _const_spec((1, ATT_W)), _const_spec((3, CONV_W)), _const_spec((CONV_W, D_MODEL))],
        out_specs=out_specs,
        out_shape=out_shape,
        scratch_shapes=[pltpu.VMEM((PAIR_SLABS, tm, LANES), F32)],
        compiler_params=_cparams(("arbitrary",)),
        name="proj",
    )(xa, xb, xa, xb, xa, xb, ng, w_in, qg, kg, cw, wcb)


def _attn_body(q_ref, kp_ref, kc_ref, kn_ref, vp_ref, vc_ref, vn_ref, bias_ref, o_ref, kw, vw,
               *, tb, first_tiles, last_tiles):
    i = pl.program_id(1)
    is_first = _any_eq(i, first_tiles)
    is_last = _any_eq(i, last_tiles)
    h = KEYS_PER_SIDE
    tq = ATT_TQ
    kwin = tq + 2 * h
    nq = tb // tq

    for s in range(PAIR_SLABS):
        sl = slice(s * LANES, (s + 1) * LANES)
        for dst, p_ref, c_ref, n_ref in ((kw, kp_ref, kc_ref, kn_ref), (vw, vp_ref, vc_ref, vn_ref)):
            dst[s, 0:h, :] = p_ref[:, sl]
            dst[s, h:h + tb, :] = c_ref[:, sl]
            dst[s, h + tb:2 * h + tb, :] = n_ref[:, sl]

    lane = lax.broadcasted_iota(I32, (1, LANES), 1)
    lo_lane = lane < HEAD_DIM
    col = lax.broadcasted_iota(I32, (1, kwin), 1)

    def head_pair(q2, k2, v2, biases):
        o_acc = None
        lse2 = None
        for half in range(2):
            mask = lo_lane if half == 0 else jnp.logical_not(lo_lane)
            qm = jnp.where(mask, q2, jnp.zeros_like(q2))
            sc = lax.dot_general(qm, k2, (((1,), (1,)), ((), ())), preferred_element_type=F32)
            sc = sc + biases[half]
            m = jnp.max(sc, axis=-1, keepdims=True)
            p = jnp.exp(sc - m)
            l = jnp.sum(p, axis=-1, keepdims=True)
            vm = jnp.where(mask, v2, jnp.zeros_like(v2))
            pv = jnp.dot(p.astype(BF16), vm, preferred_element_type=F32)
            contrib = pv * (1.0 / l)
            lse = m + jnp.log(l)
            o_acc = contrib if o_acc is None else o_acc + contrib
            lse2 = jnp.broadcast_to(lse, (tq, LANES)) if lse2 is None else jnp.where(mask, lse, lse2)
        return o_acc, lse2

    def step(qb, carry):
        pen_lo = jnp.where(jnp.logical_and(is_first, qb == 0), NEG, 0.0)
        pen_hi = jnp.where(jnp.logical_and(is_last, qb == nq - 1), NEG, 0.0)
        edge = (jnp.where(col < h, pen_lo, 0.0) + jnp.where(col >= h + tq, pen_hi, 0.0))
        q0 = pl.multiple_of(qb * tq, tq)
        for s in range(PAIR_SLABS):
            biases = [bias_ref[2 * s] + edge, bias_ref[2 * s + 1] + edge]
            o2, lse2 = head_pair(q_ref[pl.ds(q0, tq), s * LANES:(s + 1) * LANES],
                                 kw[s, pl.ds(q0, kwin), :], vw[s, pl.ds(q0, kwin), :], biases)
            o_ref[s, pl.ds(q0, tq), :] = o2
            o_ref[PAIR_SLABS + s, pl.ds(q0, tq), :] = lse2
        return carry

    lax.fori_loop(0, nq, step, 0)


def _alibi_bias(gi, r):
    n_heads = len(DILATIONS) * N_SLOTS
    slopes = (2.0 ** (-8.0 * (np.arange(n_heads, dtype=np.float32) + 1.0) / n_heads)).astype(np.float32)
    slopes = slopes.reshape(len(DILATIONS), N_SLOTS)[gi]
    qi = np.arange(ATT_TQ)[:, None]
    kj = np.arange(ATT_TQ + 2 * KEYS_PER_SIDE)[None, :]
    off = kj - KEYS_PER_SIDE - qi
    dist = (r * np.abs(off)).astype(np.float32)
    bias = -slopes[:, None, None] * dist[None]
    bias = np.where((np.abs(off) <= KEYS_PER_SIDE)[None], bias, np.float32(NEG))
    return jnp.asarray(bias, F32)


def _attn_call(q, k, v, seqs, gi):
    r, rows, _ = q.shape
    h = KEYS_PER_SIDE
    tb = min([ATT_TB_MAX] + [l // r for _, l in seqs])
    assert all((s // r) % tb == 0 and (l // r) % tb == 0 for s, l in seqs)
    nt = rows // tb
    hb = tb // h
    nhb = rows // h
    first_tiles = tuple((s // r) // tb for s, _ in seqs)
    last_tiles = tuple(((s + l) // r) // tb - 1 for s, l in seqs)
    kwin = ATT_TQ + 2 * h
    body = functools.partial(_attn_body, tb=tb, first_tiles=first_tiles, last_tiles=last_tiles)
    sq = pl.Squeezed()
    cur = pl.BlockSpec((sq, tb, GROUP_W), lambda c, i: (c, i, 0))
    prev = pl.BlockSpec((sq, h, GROUP_W), lambda c, i: (c, jnp.maximum(i * hb - 1, 0), 0))
    nxt = pl.BlockSpec((sq, h, GROUP_W), lambda c, i: (c, jnp.minimum((i + 1) * hb, nhb - 1), 0))
    return pl.pallas_call(
        body,
        grid=(r, nt),
        in_specs=[cur, prev, cur, nxt, prev, cur, nxt, _const_spec((N_SLOTS, ATT_TQ, kwin))],
        out_specs=pl.BlockSpec((2 * PAIR_SLABS, sq, tb, LANES), lambda c, i: (0, c, i, 0)),
        out_shape=jax.ShapeDtypeStruct((2 * PAIR_SLABS, r, rows, LANES), F32),
        scratch_shapes=[pltpu.VMEM((PAIR_SLABS, tb + 2 * h, LANES), BF16),
                        pltpu.VMEM((PAIR_SLABS, tb + 2 * h, LANES), BF16)],
        compiler_params=_cparams(("arbitrary", "arbitrary")),
        name=f"attn_d{r}",
    )(q, k, k, k, v, v, v, _alibi_bias(gi, r))


def _split_bf16(x):
    hi = x.astype(BF16)
    lo = (x - hi.astype(F32)).astype(BF16)
    return hi, lo


def _merge_body(o0_ref, o1_ref, o2_ref, sga_ref, gcb_ref, xa_ref, xb_ref, wab_ref, wout_ref, g2_ref,
                wrh_ref, wrl_ref, x1_ref, hn_ref, aff_ref, afft_ref, t1, t2, *, tm, n_first):
    i = pl.program_id(0)
    for o_ref, scr, r in ((o1_ref, t1, DILATIONS[1]), (o2_ref, t2, DILATIONS[2])):
        for s in range(2 * PAIR_SLABS):
            for c in range(r):
                scr[s, pl.ds(c, tm // r, stride=r), :] = o_ref[s, c]
    slabs = []
    for s in range(PAIR_SLABS):
        l0, l1, l2 = o0_ref[PAIR_SLABS + s, 0], t1[PAIR_SLABS + s], t2[PAIR_SLABS + s]
        mx = jnp.maximum(jnp.maximum(l0, l1), l2)
        e0, e1, e2 = jnp.exp(l0 - mx), jnp.exp(l1 - mx), jnp.exp(l2 - mx)
        num = e0 * o0_ref[s, 0] + e1 * t1[s] + e2 * t2[s]
        slabs.append(num / (e0 + e1 + e2))
    att = jnp.concatenate(slabs, axis=-1).astype(BF16)
    a_br = jnp.dot(att, wab_ref[...], preferred_element_type=F32)
    merged = (sga_ref[...] * a_br + gcb_ref[...]).astype(BF16)
    x = jnp.where(i < n_first, xa_ref[...], xb_ref[...])
    x1 = x + jnp.dot(merged, wout_ref[...], preferred_element_type=F32)
    x1_ref[...] = x1
    ms = jnp.mean(x1 * x1, axis=-1, keepdims=True)
    hn2 = x1 * lax.rsqrt(ms + RMS_EPS) * g2_ref[...]
    hn_ref[...] = hn2.astype(BF16)
    hh, hl = _split_bf16(hn2)
    logits = (jnp.dot(hh, wrh_ref[...], preferred_element_type=F32)
              + jnp.dot(hl, wrh_ref[...], preferred_element_type=F32)
              + jnp.dot(hh, wrl_ref[...], preferred_element_type=F32))
    lane = lax.broadcasted_iota(I32, (1, LANES), 1)
    logits = jnp.where(lane < N_EXPERTS, logits, NEG)
    e = jnp.exp(logits - jnp.max(logits, axis=-1, keepdims=True))
    aff = e / jnp.sum(e, axis=-1, keepdims=True)
    aff_ref[...] = aff
    aff_t = aff.T
    for c in range(tm // LANES):
        afft_ref[c] = aff_t[0:N_EXPERTS, c * LANES:(c + 1) * LANES]


def _merge_call(o0, o1, o2, sga, gcb, xa, xb, wab, wout, g2, wrh, wrl):
    n = sga.shape[0]
    tm = PROJ_TM
    nt = n // tm
    n_first = xa.shape[0] // tm
    body = functools.partial(_merge_body, tm=tm, n_first=n_first)
    ns = 2 * PAIR_SLABS
    o_specs = [pl.BlockSpec((ns, r, tm // r, LANES), lambda i: (0, 0, i, 0)) for r in DILATIONS]
    rowspec = pl.BlockSpec((tm, D_MODEL), lambda i: (i, 0))
    xa_spec, xb_spec = _two_array_specs(tm, n_first, D_MODEL)
    return pl.pallas_call(
        body,
        grid=(nt,),
        in_specs=o_specs + [rowspec, rowspec, xa_spec, xb_spec,
                            _const_spec((GROUP_W, D_MODEL)), _const_spec((D_MODEL, D_MODEL)),
                            _const_spec((1, D_MODEL)), _const_spec((D_MODEL, LANES)),
                            _const_spec((D_MODEL, LANES))],
        out_specs=[rowspec, rowspec,
                   pl.BlockSpec((tm, LANES), lambda i: (i, 0)),
                   pl.BlockSpec((tm // LANES, N_EXPERTS, LANES), lambda i: (i, 0, 0))],
        out_shape=[jax.ShapeDtypeStruct((n, D_MODEL), F32),
                   jax.ShapeDtypeStruct((n, D_MODEL), BF16),
                   jax.ShapeDtypeStruct((n, LANES), F32),
                   jax.ShapeDtypeStruct((n // LANES, N_EXPERTS, LANES), F32)],
        scratch_shapes=[pltpu.VMEM((ns, tm, LANES), F32), pltpu.VMEM((ns, tm, LANES), F32)],
        compiler_params=_cparams(("arbitrary",)),
        name="merge",
    )(o0, o1, o2, sga, gcb, xa, xb, wab, wout, g2, wrh, wrl)


def _route_body(aff_ref, tri_ref, ones_ref, low_ref, slot_ref, offs_ref, slot_tm_ref, rs_scr, off_scr,
                *, nch, cap):
    a = aff_ref[...]

    def count(mask):
        c = jnp.sum(mask.astype(F32), axis=0, keepdims=True)
        return jnp.sum(c, axis=2, keepdims=True)

    cur = jnp.zeros((1, N_EXPERTS, 1), I32)
    for bit in range(30, -1, -1):
        cand = cur | (1 << bit)
        cur = jnp.where(count(a >= lax.bitcast_convert_type(cand, F32)) >= cap, cand, cur)
    tau = lax.bitcast_convert_type(cur, F32)
    gt = a > tau
    eq = a == tau
    need = cap - count(gt)

    def prefix(mask):
        m = mask.astype(BF16).reshape(nch * N_EXPERTS, LANES)
        incl = jnp.dot(m, tri_ref[...], preferred_element_type=F32).reshape(nch, N_EXPERTS, LANES)
        rs_scr[...] = jnp.dot(m, ones_ref[...], preferred_element_type=F32).reshape(nch, N_EXPERTS, LANES)
        for e in range(N_EXPERTS):
            off_scr[:, e, :] = jnp.dot(low_ref[...], rs_scr[:, e, :].astype(BF16), preferred_element_type=F32)
        off = off_scr[...]
        return incl + off - mask.astype(F32), off

    tie_rank, _ = prefix(eq)
    sel = jnp.logical_or(gt, jnp.logical_and(eq, tie_rank < need))
    pos, off = prefix(sel)
    slot = jnp.where(sel, pos, -1.0)
    slot_ref[...] = slot.astype(I32)
    offs_ref[...] = off.astype(I32)
    rs_scr[...] = slot
    pad = jnp.full((LANES - N_EXPERTS, LANES), -1.0, F32)

    def tr(c, carry):
        tile = jnp.concatenate([rs_scr[c], pad], axis=0)
        slot_tm_ref[pl.ds(pl.multiple_of(c * LANES, LANES), LANES), :] = tile.T.astype(I32)
        return carry

    lax.fori_loop(0, nch, tr, 0)


def _route_call(afft, n_groups, cap):
    nchunks = afft.shape[0]
    nch = nchunks // n_groups
    tri = jnp.asarray(np.triu(np.ones((LANES, LANES), np.float32)), BF16)
    ones = jnp.ones((LANES, LANES), BF16)
    low = jnp.asarray(np.tril(np.ones((nch, nch), np.float32), -1), BF16)
    body = functools.partial(_route_body, nch=nch, cap=cap)
    blk = pl.BlockSpec((nch, N_EXPERTS, LANES), lambda b: (b, 0, 0))
    return pl.pallas_call(
        body,
        grid=(n_groups,),
        in_specs=[blk, _const_spec((LANES, LANES)), _const_spec((LANES, LANES)), _const_spec((nch, nch))],
        out_specs=[blk, blk, pl.BlockSpec((nch * LANES, LANES), lambda b: (b, 0))],
        out_shape=[jax.ShapeDtypeStruct((nchunks, N_EXPERTS, LANES), I32),
                   jax.ShapeDtypeStruct((nchunks, N_EXPERTS, LANES), I32),
                   jax.ShapeDtypeStruct((nchunks * LANES, LANES), I32)],
        scratch_shapes=[pltpu.VMEM((nch, N_EXPERTS, LANES), F32), pltpu.VMEM((nch, N_EXPERTS, LANES), F32)],
        compiler_params=_cparams(("arbitrary",)),
        name="route",
    )(afft, tri, ones, low)


def _dispatch_body(starts_ref, ends_ref, hn_ref, slot_ref, x_ref, *, tm, sb, tiles_per_group):
    b, eg, j = pl.program_id(0), pl.program_id(1), pl.program_id(2)
    w = DISP_W

    @pl.when(j == 0)
    def _():
        x_ref[...] = jnp.zeros_like(x_ref)

    iota_w = lax.broadcasted_iota(I32, (w, 1), 0)
    cpt = tm // LANES
    for t in range(sb // tm):
        tile = j * (sb // tm) + t
        hn_t = hn_ref[t * tm:(t + 1) * tm, :]
        base, srows, ends = [], [], []
        for ee in range(DISP_EG):
            e = eg * DISP_EG + ee
            idx = (b * tiles_per_group + tile) * N_EXPERTS + e
            base.append(pl.multiple_of((starts_ref[idx] // 16) * 16, 16))
            ends.append(ends_ref[idx])
            srows.append(jnp.concatenate([slot_ref[t * cpt + c, pl.ds(e, 1), :] for c in range(cpt)], axis=1))
        oh = jnp.concatenate([(srows[ee] - base[ee] == iota_w).astype(BF16) for ee in range(DISP_EG)], axis=0)
        res = jnp.dot(oh, hn_t, preferred_element_type=F32)
        for ee in range(DISP_EG):
            x_ref[ee, pl.ds(base[ee], w), :] += res[ee * w:(ee + 1) * w].astype(x_ref.dtype)

            def more(k, carry, ee=ee):
                bk = pl.multiple_of(base[ee] + k * w, 16)
                ohk = (srows[ee] - bk == iota_w).astype(BF16)
                x_ref[ee, pl.ds(bk, w), :] += jnp.dot(ohk, hn_t, preferred_element_type=F32).astype(x_ref.dtype)
                return carry

            lax.fori_loop(1, (ends[ee] - base[ee] + w - 1) // w, more, 0)


def _dispatch_call(starts, ends, hn, slot, n_groups, cap):
    n = hn.shape[0]
    ng_rows = n // n_groups
    tm, sb = PROJ_TM, DISP_SB
    nsb = ng_rows // sb
    tiles_per_group = ng_rows // tm
    cps = sb // LANES
    body = functools.partial(_dispatch_body, tm=tm, sb=sb, tiles_per_group=tiles_per_group)
    sq = pl.Squeezed()
    grid_spec = pltpu.PrefetchScalarGridSpec(
        num_scalar_prefetch=2,
        grid=(n_groups, N_EXPERTS // DISP_EG, nsb),
        in_specs=[pl.BlockSpec((sb, D_MODEL), lambda b, g, j, *_: (b * nsb + j, 0)),
                  pl.BlockSpec((cps, N_EXPERTS, LANES), lambda b, g, j, *_: (b * nsb + j, 0, 0))],
        out_specs=pl.BlockSpec((sq, DISP_EG, cap + DISP_W, D_MODEL), lambda b, g, j, *_: (b, g, 0, 0)),
    )
    return pl.pallas_call(
        body,
        grid_spec=grid_spec,
        out_shape=jax.ShapeDtypeStruct((n_groups, N_EXPERTS, cap + DISP_W, D_MODEL), BF16),
        compiler_params=_cparams(("arbitrary", "arbitrary", "arbitrary")),
        name="dispatch",
    )(starts, ends, hn, slot)


def _ffn_body(x_ref, wg_ref, wu_ref, wd_ref, y_ref, acc, *, cap):
    f = pl.program_id(2)
    nf = pl.num_programs(2)
    @pl.when(f == 0)
    def _():
        acc[...] = jnp.zeros_like(acc)

    wg = wg_ref[...].astype(BF16)
    wu = wu_ref[...].astype(BF16)
    wd = wd_ref[...].astype(BF16)
    for rb in range(cap // FFN_RB):
        rows = slice(rb * FFN_RB, (rb + 1) * FFN_RB)
        xb = x_ref[rows, :]
        g = jnp.dot(xb, wg, preferred_element_type=F32)
        u = jnp.dot(xb, wu, preferred_element_type=F32)
        hid = (g * jax.nn.sigmoid(g) * u).astype(BF16)
        acc[rows, :] += jnp.dot(hid, wd, preferred_element_type=F32)

    @pl.when(f == nf - 1)
    def _():
        y_ref[...] = acc[...].astype(y_ref.dtype)


def _ffn_call(xe, w_gate, w_up, w_down, cap):
    n_groups = xe.shape[0]
    fc = FFN_FC
    nf = EXPERT_FF // fc
    body = functools.partial(_ffn_body, cap=cap)
    sq = pl.Squeezed()
    return pl.pallas_call(
        body,
        grid=(n_groups, N_EXPERTS, nf),
        in_specs=[pl.BlockSpec((sq, sq, cap, D_MODEL), lambda b, e, f: (b, e, 0, 0)),
                  pl.BlockSpec((sq, D_MODEL, fc), lambda b, e, f: (e, 0, f)),
                  pl.BlockSpec((sq, D_MODEL, fc), lambda b, e, f: (e, 0, f)),
                  pl.BlockSpec((sq, fc, D_MODEL), lambda b, e, f: (e, f, 0))],
        out_specs=pl.BlockSpec((sq, sq, cap, D_MODEL), lambda b, e, f: (b, e, 0, 0)),
        out_shape=jax.ShapeDtypeStruct((n_groups, N_EXPERTS, cap, D_MODEL), BF16),
        scratch_shapes=[pltpu.VMEM((cap, D_MODEL), F32)],
        compiler_params=_cparams(("arbitrary", "arbitrary", "arbitrary")),
        name="ffn",
    )(xe, w_gate, w_up, w_down)


def _combine_body(astart_ref, nrounds_ref, x1_ref, aff_ref, slot_ref, y_hbm, outa_ref, outb_ref,
                  ybuf, sem, yext, sem_ext, accs, *, tm, cap, tiles_per_group, n_first):
    i = pl.program_id(0)
    nt = pl.num_programs(0)
    w = COMB_W

    def window_copy(tile, e, k, dst, dsem):
        a0 = astart_ref[tile * N_EXPERTS + e]
        a = pl.multiple_of(jnp.minimum(a0 + k * w, cap - w), 16)
        return pltpu.make_async_copy(y_hbm.at[tile // tiles_per_group, e, pl.ds(a, w), :], dst, dsem)

    def first_copy(tile, e):
        sl = tile % 2
        return window_copy(tile, e, 0, ybuf.at[sl, pl.ds(e * w, w)], sem.at[sl, e])

    @pl.when(i == 0)
    def _():
        for e in range(N_EXPERTS):
            first_copy(i, e).start()

    @pl.when(i + 1 < nt)
    def _():
        for e in range(N_EXPERTS):
            first_copy(i + 1, e).start()

    lane = lax.broadcasted_iota(I32, (1, LANES), 1)
    lo_lane = lane < w
    slot_t = slot_ref[...]
    gate = aff_ref[...]
    g_hi = gate.astype(BF16).astype(F32)
    g_lo = gate - g_hi

    def expand(k, ywin):
        his, los = [], []
        for ep in range(N_EXPERTS // 2):
            rel, skip = [], []
            for e in (2 * ep, 2 * ep + 1):
                a0 = astart_ref[i * N_EXPERTS + e]
                lo = a0 + k * w
                a = jnp.minimum(lo, cap - w)
                rel.append(slot_t[:, e:e + 1] - a)
                skip.append(lo - a)
            fresh = lane >= jnp.where(lo_lane, skip[0], skip[1] + w)
            match = jnp.logical_and(jnp.where(lo_lane, rel[0], rel[1] + w) == lane, fresh)
            his.append(jnp.where(match, jnp.where(lo_lane, g_hi[:, 2 * ep:2 * ep + 1],
                                                  g_hi[:, 2 * ep + 1:2 * ep + 2]), 0.0).astype(BF16))
            los.append(jnp.where(match, jnp.where(lo_lane, g_lo[:, 2 * ep:2 * ep + 1],
                                                  g_lo[:, 2 * ep + 1:2 * ep + 2]), 0.0).astype(BF16))
        return (jnp.dot(jnp.concatenate(his, axis=1), ywin, preferred_element_type=F32)
                + jnp.dot(jnp.concatenate(los, axis=1), ywin, preferred_element_type=F32))

    for e in range(N_EXPERTS):
        first_copy(i, e).wait()
    accs[...] = x1_ref[...] + expand(0, ybuf[i % 2])

    def extra_round(k, carry):
        for e in range(N_EXPERTS):
            window_copy(i, e, k, yext.at[pl.ds(e * w, w)], sem_ext.at[e]).start()
        for e in range(N_EXPERTS):
            window_copy(i, e, k, yext.at[pl.ds(e * w, w)], sem_ext.at[e]).wait()
        accs[...] += expand(k, yext[...])
        return carry

    lax.fori_loop(1, nrounds_ref[i], extra_round, 0)

    @pl.when(i < n_first)
    def _():
        outa_ref[...] = accs[...]

    @pl.when(i >= n_first)
    def _():
        outb_ref[...] = accs[...]


def _combine_call(astart, nrounds, x1, aff, slot_tm, y, n_groups, cap, na):
    n = x1.shape[0]
    tm = PROJ_TM
    nt = n // tm
    n_first = na // tm
    tiles_per_group = nt // n_groups
    body = functools.partial(_combine_body, tm=tm, cap=cap, tiles_per_group=tiles_per_group, n_first=n_first)
    outa_spec, outb_spec = _two_array_specs(tm, n_first, D_MODEL, prefetch=True)
    grid_spec = pltpu.PrefetchScalarGridSpec(
        num_scalar_prefetch=2,
        grid=(nt,),
        in_specs=[pl.BlockSpec((tm, D_MODEL), lambda i, *_: (i, 0)),
                  pl.BlockSpec((tm, LANES), lambda i, *_: (i, 0)),
                  pl.BlockSpec((tm, LANES), lambda i, *_: (i, 0)),
                  pl.BlockSpec(memory_space=pl.ANY)],
        out_specs=[outa_spec, outb_spec],
        scratch_shapes=[pltpu.VMEM((2, N_EXPERTS * COMB_W, D_MODEL), BF16),
                        pltpu.SemaphoreType.DMA((2, N_EXPERTS)),
                        pltpu.VMEM((N_EXPERTS * COMB_W, D_MODEL), BF16),
                        pltpu.SemaphoreType.DMA((N_EXPERTS,)),
                        pltpu.VMEM((tm, D_MODEL), F32)],
    )
    return pl.pallas_call(
        body,
        grid_spec=grid_spec,
        out_shape=[jax.ShapeDtypeStruct((na, D_MODEL), F32), jax.ShapeDtypeStruct((n - na, D_MODEL), F32)],
        compiler_params=_cparams(("arbitrary",)),
        name="combine",
    )(astart, nrounds, x1, aff, slot_tm, y)


def _layer(xa, xb, seqs, n_groups, mix_norm_g, w_in, q_norm_g, k_norm_g, conv_w, w_attn_branch, w_conv_branch,
           w_out, ffn_norm_g, w_router, w_gate, w_up, w_down):
    na = xa.shape[0]
    n = na + xb.shape[0]
    group_rows = n // n_groups
    cap = CAPACITY_FACTOR * group_rows // N_EXPERTS
    tm = PROJ_TM

    qg = jnp.tile(q_norm_g, ATT_W // HEAD_DIM)[None, :]
    kg = jnp.tile(k_norm_g, ATT_W // HEAD_DIM)[None, :]
    outs = _proj_call(xa, xb, seqs, mix_norm_g[None, :], w_in.astype(BF16), qg, kg, conv_w,
                      w_conv_branch.astype(BF16))
    sga, gcb = outs[9], outs[10]
    o = []
    for gi, r in enumerate(DILATIONS):
        q, k, v = outs[3 * gi:3 * gi + 3]
        if r == 1:
            q, k, v = q[None], k[None], v[None]
        o.append(_attn_call(q, k, v, seqs, gi))

    wr = jnp.pad(w_router, ((0, 0), (0, LANES - N_EXPERTS)))
    wrh = wr.astype(BF16)
    wrl = (wr - wrh.astype(F32)).astype(BF16)
    x1, hn2, aff, afft = _merge_call(o[0], o[1], o[2], sga, gcb, xa, xb, w_attn_branch.astype(BF16),
                                     w_out.astype(BF16), ffn_norm_g[None, :], wrh, wrl)

    slot, offs, slot_tm = _route_call(afft, n_groups, cap)

    cpt = tm // LANES
    tiles_per_group = group_rows // tm
    starts = offs[::cpt, :, 0].reshape(n_groups, tiles_per_group, N_EXPERTS)
    ends = jnp.concatenate([starts[:, 1:], jnp.full((n_groups, 1, N_EXPERTS), cap, I32)], axis=1)
    xe = _dispatch_call(starts.reshape(-1), ends.reshape(-1), hn2, slot, n_groups, cap)
    y = _ffn_call(xe, w_gate, w_up, w_down, cap)

    astart = jnp.minimum((starts // 16) * 16, cap - COMB_W)
    nrounds = jnp.maximum(jnp.max((ends - astart + COMB_W - 1) // COMB_W, axis=-1), 1)
    return _combine_call(astart.reshape(-1), nrounds.reshape(-1), x1, aff, slot_tm, y, n_groups, cap, na)


def _run(x_prompt, x_sample, params):
    bp, sp, d = x_prompt.shape
    bs, ss, _ = x_sample.shape
    seqs, pos = [], 0
    for b, s in ((bp, sp), (bs, ss)):
        for _ in range(b):
            seqs.append((pos, s))
            pos += s
    assert bp * sp == bs * ss, "request groups are routed with one capacity"
    xa, xb = x_prompt.reshape(bp * sp, d), x_sample.reshape(bs * ss, d)
    depth = params[0].shape[0]
    for layer in range(depth):
        xa, xb = _layer(xa, xb, tuple(seqs), 2, *[p[layer] for p in params])
    return xa.reshape(bp, sp, d), xb.reshape(bs, ss, d)


def kernel(x_prompt, x_sample, mix_norm_g, w_in, q_norm_g, k_norm_g, conv_w, w_attn_branch, w_conv_branch, w_out,
           ffn_norm_g, w_router, w_gate, w_up, w_down):
    params = (mix_norm_g, w_in, q_norm_g, k_norm_g, conv_w, w_attn_branch, w_conv_branch, w_out,
              ffn_norm_g, w_router, w_gate, w_up, w_down)
    return _run(x_prompt, x_sample, params)
```

```python
import functools

import numpy as np
import jax
import jax.numpy as jnp
from jax import lax
from jax.experimental import pallas as pl
from jax.experimental.pallas import tpu as pltpu

F32 = jnp.float32
BF16 = jnp.bfloat16
I32 = jnp.int32
U32 = jnp.uint32

D_MODEL = 1024
HEAD_DIM = 64
N_SLOTS = 8
DILATIONS = (1, 4, 16)
KEYS_PER_SIDE = 64
GROUP_W = N_SLOTS * HEAD_DIM
ATT_W = len(DILATIONS) * GROUP_W
CONV_W = D_MODEL
IN_W = 3 * ATT_W + 3 * CONV_W + 2 * D_MODEL
N_EXPERTS = 16
EXPERT_FF = 2048
CAPACITY_FACTOR = 2
RMS_EPS = 1e-6
NEG = -1e30

LANES = 128
PAIR_SLABS = GROUP_W // LANES
VMEM_LIMIT = 58 * 1024 * 1024

PROJ_TM = 256
PROJ_HALO = 16
ATT_TT = 2048
ATT_TQ = 128
FFN_FC = 512
FFN_RB = 512
DISP_SB = 2048
DISP_EG = 4
DISP_W = 96
COMB_W = 64


def _cparams(sem):
    return pltpu.CompilerParams(dimension_semantics=sem, vmem_limit_bytes=VMEM_LIMIT)


def _any_eq(i, values):
    return functools.reduce(jnp.logical_or, [i == v for v in values])


def _const_spec(shape):
    nd = len(shape)
    return pl.BlockSpec(shape, lambda *a: (0,) * nd, pipeline_mode=pl.Buffered(1))


def _two_array_specs(block_rows, n_first_blocks, cols, prefetch=False):
    if prefetch:
        first = pl.BlockSpec((block_rows, cols), lambda i, *_: (jnp.minimum(i, n_first_blocks - 1), 0))
        second = pl.BlockSpec((block_rows, cols), lambda i, *_: (jnp.maximum(i - n_first_blocks, 0), 0))
    else:
        first = pl.BlockSpec((block_rows, cols), lambda i: (jnp.minimum(i, n_first_blocks - 1), 0))
        second = pl.BlockSpec((block_rows, cols), lambda i: (jnp.maximum(i - n_first_blocks, 0), 0))
    return first, second


def _proj_body(xa_ref, xb_ref, xpa_ref, xpb_ref, xna_ref, xnb_ref, ng_ref, w_ref, qg_ref, kg_ref, cw_ref, wcb_ref,
               q_ref, k_ref, v_ref, sga_ref, gcb_ref, *, tm, halo, n_first, first_tiles, last_tiles):
    i = pl.program_id(0)
    is_first = _any_eq(i, first_tiles)
    is_last = _any_eq(i, last_tiles)
    in_a = i < n_first

    def pick(a_ref, b_ref):
        return jnp.where(in_a, a_ref[...], b_ref[...])

    x_ext = jnp.concatenate([pick(xpa_ref, xpb_ref), pick(xa_ref, xb_ref), pick(xna_ref, xnb_ref)], axis=0)
    ms = jnp.mean(x_ext * x_ext, axis=-1, keepdims=True)
    hn_ext = (x_ext * lax.rsqrt(ms + RMS_EPS) * ng_ref[...]).astype(BF16)
    hn = hn_ext[halo:halo + tm]

    def proj(lhs, a, b):
        return jnp.dot(lhs, w_ref[:, a:b], preferred_element_type=F32)

    lo_lane = lax.broadcasted_iota(I32, (1, LANES), 1) < HEAD_DIM

    def head_norm_store(p, g_ref, out_ref, scale):
        for s in range(ATT_W // LANES):
            xs = p[:, s * LANES:(s + 1) * LANES]
            x2 = xs * xs
            tot = jnp.sum(x2, axis=-1, keepdims=True)
            lo = jnp.sum(jnp.where(lo_lane, x2, 0.0), axis=-1, keepdims=True)
            hi = tot - lo
            r = jnp.where(lo_lane, lax.rsqrt(lo * (1.0 / HEAD_DIM) + RMS_EPS),
                          lax.rsqrt(hi * (1.0 / HEAD_DIM) + RMS_EPS))
            y = xs * r * g_ref[:, s * LANES:(s + 1) * LANES]
            if scale != 1.0:
                y = y * scale
            out_ref[:, s * LANES:(s + 1) * LANES] = y.astype(out_ref.dtype)

    head_norm_store(proj(hn, 0, ATT_W), qg_ref, q_ref, HEAD_DIM ** -0.5)
    head_norm_store(proj(hn, ATT_W, 2 * ATT_W), kg_ref, k_ref, 1.0)
    v_ref[...] = proj(hn, 2 * ATT_W, 3 * ATT_W).astype(v_ref.dtype)

    c0 = 3 * ATT_W
    cb = proj(hn, c0, c0 + CONV_W)
    cc = proj(hn_ext, c0 + CONV_W, c0 + 2 * CONV_W)
    cx = proj(hn_ext, c0 + 2 * CONV_W, c0 + 3 * CONV_W)
    u_ext = cc * cx
    row = lax.broadcasted_iota(I32, (tm + 2 * halo, 1), 0)
    keep = jnp.logical_and(jnp.logical_or(row >= halo, jnp.logical_not(is_first)),
                           jnp.logical_or(row < halo + tm, jnp.logical_not(is_last)))
    u_ext = jnp.where(keep, u_ext, 0.0)
    cy = (u_ext[halo - 1:halo - 1 + tm] * cw_ref[0:1, :]
          + u_ext[halo:halo + tm] * cw_ref[1:2, :]
          + u_ext[halo + 1:halo + 1 + tm] * cw_ref[2:3, :])
    conv_in = (cb * cy).astype(BF16)
    cbr = jnp.dot(conv_in, wcb_ref[...], preferred_element_type=F32)
    g0 = c0 + 3 * CONV_W
    sga_ref[...] = jax.nn.sigmoid(proj(hn, g0, g0 + D_MODEL))
    gcb_ref[...] = jax.nn.sigmoid(proj(hn, g0 + D_MODEL, g0 + 2 * D_MODEL)) * cbr


def _proj_call(xa, xb, seqs, ng, w_in, qg, kg, cw, wcb):
    na, nb_rows = xa.shape[0], xb.shape[0]
    n = na + nb_rows
    tm, halo = PROJ_TM, PROJ_HALO
    nt = n // tm
    n_first = na // tm
    hb = tm // halo
    first_tiles = tuple(s // tm for s, _ in seqs)
    last_tiles = tuple((s + l) // tm - 1 for s, l in seqs)
    nha, nhb = na // halo, nb_rows // halo
    body = functools.partial(_proj_body, tm=tm, halo=halo, n_first=n_first,
                             first_tiles=first_tiles, last_tiles=last_tiles)
    cur_a, cur_b = _two_array_specs(tm, n_first, D_MODEL)
    prev_a = pl.BlockSpec((halo, D_MODEL), lambda i: (jnp.clip(i * hb - 1, 0, nha - 1), 0))
    prev_b = pl.BlockSpec((halo, D_MODEL), lambda i: (jnp.clip((i - n_first) * hb - 1, 0, nhb - 1), 0))
    next_a = pl.BlockSpec((halo, D_MODEL), lambda i: (jnp.clip((i + 1) * hb, 0, nha - 1), 0))
    next_b = pl.BlockSpec((halo, D_MODEL), lambda i: (jnp.clip((i + 1 - n_first) * hb, 0, nhb - 1), 0))
    out_specs = ([pl.BlockSpec((tm, ATT_W), lambda i: (i, 0))] * 3
                 + [pl.BlockSpec((tm, D_MODEL), lambda i: (i, 0))] * 2)
    out_shape = ([jax.ShapeDtypeStruct((n, ATT_W), BF16)] * 3
                 + [jax.ShapeDtypeStruct((n, D_MODEL), F32)] * 2)
    return pl.pallas_call(
        body,
        grid=(nt,),
        in_specs=[cur_a, cur_b, prev_a, prev_b, next_a, next_b,
                  _const_spec((1, D_MODEL)), _const_spec((D_MODEL, IN_W)), _const_spec((1, ATT_W)),
                  _const_spec((1, ATT_W)), _const_spec((3, CONV_W)), _const_spec((CONV_W, D_MODEL))],
        out_specs=out_specs,
        out_shape=out_shape,
        compiler_params=_cparams(("arbitrary",)),
        name="proj",
    )(xa, xb, xa, xb, xa, xb, ng, w_in, qg, kg, cw, wcb)


def _attn_body(q_ref, kp_ref, kc_ref, kn_ref, vp_ref, vc_ref, vn_ref, bias_ref, o_ref, qs, kw, vw,
               *, r, tt, first_tiles, last_tiles):
    i = pl.program_id(0)
    is_first = _any_eq(i, first_tiles)
    is_last = _any_eq(i, last_tiles)
    h = KEYS_PER_SIDE * r
    tq = ATT_TQ
    kwin = tq + 2 * KEYS_PER_SIDE
    nq = tt // (tq * r)
    packed = r > 1

    def fill(dst, s, p_ref, c_ref, n_ref):
        sl = slice(s * LANES, (s + 1) * LANES)
        if packed:
            dst[s, 0:h // 2, :] = pltpu.bitcast(p_ref[:, sl], U32)
            dst[s, h // 2:(h + tt) // 2, :] = pltpu.bitcast(c_ref[:, sl], U32)
            dst[s, (h + tt) // 2:(2 * h + tt) // 2, :] = pltpu.bitcast(n_ref[:, sl], U32)
        else:
            dst[s, 0:h, :] = p_ref[:, sl]
            dst[s, h:h + tt, :] = c_ref[:, sl]
            dst[s, h + tt:2 * h + tt, :] = n_ref[:, sl]

    for s in range(PAIR_SLABS):
        fill(kw, s, kp_ref, kc_ref, kn_ref)
        fill(vw, s, vp_ref, vc_ref, vn_ref)
        if packed:
            qs[s, :, :] = pltpu.bitcast(q_ref[:, s * LANES:(s + 1) * LANES], U32)

    lane = lax.broadcasted_iota(I32, (1, LANES), 1)
    lo_lane = lane < HEAD_DIM
    col = lax.broadcasted_iota(I32, (1, kwin), 1)

    def split(u):
        lo = pltpu.bitcast(u << 16, F32).astype(BF16)
        hi = pltpu.bitcast(u & jnp.uint32(0xFFFF0000), F32).astype(BF16)
        return lo, hi

    rh = tq

    def head_pair(q2, k2, v2, biases):
        o_acc = None
        lse2 = None
        for half in range(2):
            mask = lo_lane if half == 0 else jnp.logical_not(lo_lane)
            qm = jnp.where(mask, q2, jnp.zeros_like(q2))
            sc = lax.dot_general(qm, k2, (((1,), (1,)), ((), ())), preferred_element_type=F32)
            sc = sc + biases[half]
            m = jnp.max(sc, axis=-1, keepdims=True)
            p = jnp.exp(sc - m)
            l = jnp.sum(p, axis=-1, keepdims=True)
            vm = jnp.where(mask, v2, jnp.zeros_like(v2))
            pv = jnp.dot(p.astype(BF16), vm, preferred_element_type=F32)
            contrib = pv * (1.0 / l)
            lse = m + jnp.log(l)
            o_acc = contrib if o_acc is None else o_acc + contrib
            lse2 = jnp.broadcast_to(lse, (rh, LANES)) if lse2 is None else jnp.where(mask, lse, lse2)
        return o_acc, lse2

    npairs = max(r // 2, 1)

    def step(it, carry):
        qb = it // npairs
        pr = it % npairs
        pen_lo = jnp.where(jnp.logical_and(is_first, qb == 0), NEG, 0.0)
        pen_hi = jnp.where(jnp.logical_and(is_last, qb == nq - 1), NEG, 0.0)
        edge = (jnp.where(col < KEYS_PER_SIDE, pen_lo, 0.0)
                + jnp.where(col >= KEYS_PER_SIDE + tq, pen_hi, 0.0))
        for s in range(PAIR_SLABS):
            biases = [bias_ref[2 * s] + edge, bias_ref[2 * s + 1] + edge]
            if packed:
                st = r // 2
                start = qb * (tq * st) + pr
                parts = list(zip(split(qs[s, pl.ds(start, tq, stride=st), :]),
                                 split(kw[s, pl.ds(start, kwin, stride=st), :]),
                                 split(vw[s, pl.ds(start, kwin, stride=st), :])))
            else:
                q0 = pl.multiple_of(qb * tq, tq)
                parts = [(q_ref[pl.ds(q0, tq), s * LANES:(s + 1) * LANES],
                          kw[s, pl.ds(q0, kwin), :], vw[s, pl.ds(q0, kwin), :])]
            for ci, (q2, k2, v2) in enumerate(parts):
                o2, lse2 = head_pair(q2, k2, v2, biases)
                if packed:
                    rows = pl.ds(qb * (tq * r) + 2 * pr + ci, tq, stride=r)
                else:
                    rows = pl.ds(pl.multiple_of(qb * tq, tq), tq)
                o_ref[s, rows, :] = o2
                o_ref[PAIR_SLABS + s, rows, :] = lse2
        return carry

    lax.fori_loop(0, nq * npairs, step, 0)


def _alibi_bias(gi, r):
    n_heads = len(DILATIONS) * N_SLOTS
    slopes = (2.0 ** (-8.0 * (np.arange(n_heads, dtype=np.float32) + 1.0) / n_heads)).astype(np.float32)
    slopes = slopes.reshape(len(DILATIONS), N_SLOTS)[gi]
    qi = np.arange(ATT_TQ)[:, None]
    kj = np.arange(ATT_TQ + 2 * KEYS_PER_SIDE)[None, :]
    off = kj - KEYS_PER_SIDE - qi
    dist = (r * np.abs(off)).astype(np.float32)
    bias = -slopes[:, None, None] * dist[None]
    bias = np.where((np.abs(off) <= KEYS_PER_SIDE)[None], bias, np.float32(NEG))
    return jnp.asarray(bias, F32)


def _attn_call(q, k, v, seqs, gi):
    n = q.shape[0]
    r = DILATIONS[gi]
    tt = ATT_TT
    h = KEYS_PER_SIDE * r
    assert all(s % tt == 0 and l % tt == 0 for s, l in seqs) and tt % (ATT_TQ * r) == 0
    nt = n // tt
    hb = tt // h
    nhb = n // h
    first_tiles = tuple(s // tt for s, _ in seqs)
    last_tiles = tuple((s + l) // tt - 1 for s, l in seqs)
    kwin = ATT_TQ + 2 * KEYS_PER_SIDE
    body = functools.partial(_attn_body, r=r, tt=tt, first_tiles=first_tiles, last_tiles=last_tiles)
    cur = pl.BlockSpec((tt, GROUP_W), lambda i: (i, gi))
    prev = pl.BlockSpec((h, GROUP_W), lambda i: (jnp.maximum(i * hb - 1, 0), gi))
    nxt = pl.BlockSpec((h, GROUP_W), lambda i: (jnp.minimum((i + 1) * hb, nhb - 1), gi))
    if r > 1:
        scratch = [pltpu.VMEM((PAIR_SLABS, tt // 2, LANES), U32),
                   pltpu.VMEM((PAIR_SLABS, (tt + 2 * h) // 2, LANES), U32),
                   pltpu.VMEM((PAIR_SLABS, (tt + 2 * h) // 2, LANES), U32)]
    else:
        scratch = [pltpu.VMEM((8, LANES), U32),
                   pltpu.VMEM((PAIR_SLABS, tt + 2 * h, LANES), BF16),
                   pltpu.VMEM((PAIR_SLABS, tt + 2 * h, LANES), BF16)]
    return pl.pallas_call(
        body,
        grid=(nt,),
        in_specs=[cur, prev, cur, nxt, prev, cur, nxt, _const_spec((N_SLOTS, ATT_TQ, kwin))],
        out_specs=pl.BlockSpec((2 * PAIR_SLABS, tt, LANES), lambda i: (0, i, 0)),
        out_shape=jax.ShapeDtypeStruct((2 * PAIR_SLABS, n, LANES), F32),
        scratch_shapes=scratch,
        compiler_params=_cparams(("arbitrary",)),
        name=f"attn_d{r}",
    )(q, k, k, k, v, v, v, _alibi_bias(gi, r))


def _split_bf16(x):
    hi = x.astype(BF16)
    lo = (x - hi.astype(F32)).astype(BF16)
    return hi, lo


def _merge_body(o0_ref, o1_ref, o2_ref, sga_ref, gcb_ref, xa_ref, xb_ref, wab_ref, wout_ref, g2_ref,
                wrh_ref, wrl_ref, x1_ref, hn_ref, aff_ref, afft_ref, *, tm, n_first):
    i = pl.program_id(0)
    slabs = []
    for s in range(PAIR_SLABS):
        l0, l1, l2 = o0_ref[PAIR_SLABS + s], o1_ref[PAIR_SLABS + s], o2_ref[PAIR_SLABS + s]
        mx = jnp.maximum(jnp.maximum(l0, l1), l2)
        e0, e1, e2 = jnp.exp(l0 - mx), jnp.exp(l1 - mx), jnp.exp(l2 - mx)
        num = e0 * o0_ref[s] + e1 * o1_ref[s] + e2 * o2_ref[s]
        slabs.append(num / (e0 + e1 + e2))
    att = jnp.concatenate(slabs, axis=-1).astype(BF16)
    a_br = jnp.dot(att, wab_ref[...], preferred_element_type=F32)
    merged = (sga_ref[...] * a_br + gcb_ref[...]).astype(BF16)
    x = jnp.where(i < n_first, xa_ref[...], xb_ref[...])
    x1 = x + jnp.dot(merged, wout_ref[...], preferred_element_type=F32)
    x1_ref[...] = x1
    ms = jnp.mean(x1 * x1, axis=-1, keepdims=True)
    hn2 = x1 * lax.rsqrt(ms + RMS_EPS) * g2_ref[...]
    hn_ref[...] = hn2.astype(BF16)
    hh, hl = _split_bf16(hn2)
    logits = (jnp.dot(hh, wrh_ref[...], preferred_element_type=F32)
              + jnp.dot(hl, wrh_ref[...], preferred_element_type=F32)
              + jnp.dot(hh, wrl_ref[...], preferred_element_type=F32))
    lane = lax.broadcasted_iota(I32, (1, LANES), 1)
    logits = jnp.where(lane < N_EXPERTS, logits, NEG)
    e = jnp.exp(logits - jnp.max(logits, axis=-1, keepdims=True))
    aff = e / jnp.sum(e, axis=-1, keepdims=True)
    aff_ref[...] = aff
    aff_t = aff.T
    for c in range(tm // LANES):
        afft_ref[c] = aff_t[0:N_EXPERTS, c * LANES:(c + 1) * LANES]


def _merge_call(o0, o1, o2, sga, gcb, xa, xb, wab, wout, g2, wrh, wrl):
    n = sga.shape[0]
    tm = PROJ_TM
    nt = n // tm
    n_first = xa.shape[0] // tm
    body = functools.partial(_merge_body, tm=tm, n_first=n_first)
    ns = 2 * PAIR_SLABS
    o_specs = [pl.BlockSpec((ns, tm, LANES), lambda i: (0, i, 0))] * len(DILATIONS)
    rowspec = pl.BlockSpec((tm, D_MODEL), lambda i: (i, 0))
    xa_spec, xb_spec = _two_array_specs(tm, n_first, D_MODEL)
    return pl.pallas_call(
        body,
        grid=(nt,),
        in_specs=o_specs + [rowspec, rowspec, xa_spec, xb_spec,
                            _const_spec((GROUP_W, D_MODEL)), _const_spec((D_MODEL, D_MODEL)),
                            _const_spec((1, D_MODEL)), _const_spec((D_MODEL, LANES)),
                            _const_spec((D_MODEL, LANES))],
        out_specs=[rowspec, rowspec,
                   pl.BlockSpec((tm, LANES), lambda i: (i, 0)),
                   pl.BlockSpec((tm // LANES, N_EXPERTS, LANES), lambda i: (i, 0, 0))],
        out_shape=[jax.ShapeDtypeStruct((n, D_MODEL), F32),
                   jax.ShapeDtypeStruct((n, D_MODEL), BF16),
                   jax.ShapeDtypeStruct((n, LANES), F32),
                   jax.ShapeDtypeStruct((n // LANES, N_EXPERTS, LANES), F32)],
        compiler_params=_cparams(("arbitrary",)),
        name="merge",
    )(o0, o1, o2, sga, gcb, xa, xb, wab, wout, g2, wrh, wrl)


def _route_body(aff_ref, tri_ref, ones_ref, low_ref, slot_ref, offs_ref, slot_tm_ref, rs_scr, off_scr,
                *, nch, cap):
    a = aff_ref[...]

    def count(mask):
        c = jnp.sum(mask.astype(F32), axis=0, keepdims=True)
        return jnp.sum(c, axis=2, keepdims=True)

    cur = jnp.zeros((1, N_EXPERTS, 1), I32)
    for bit in range(30, -1, -1):
        cand = cur | (1 << bit)
        cur = jnp.where(count(a >= lax.bitcast_convert_type(cand, F32)) >= cap, cand, cur)
    tau = lax.bitcast_convert_type(cur, F32)
    gt = a > tau
    eq = a == tau
    need = cap - count(gt)

    def prefix(mask):
        m = mask.astype(BF16).reshape(nch * N_EXPERTS, LANES)
        incl = jnp.dot(m, tri_ref[...], preferred_element_type=F32).reshape(nch, N_EXPERTS, LANES)
        rs_scr[...] = jnp.dot(m, ones_ref[...], preferred_element_type=F32).reshape(nch, N_EXPERTS, LANES)
        for e in range(N_EXPERTS):
            off_scr[:, e, :] = jnp.dot(low_ref[...], rs_scr[:, e, :].astype(BF16), preferred_element_type=F32)
        off = off_scr[...]
        return incl + off - mask.astype(F32), off

    tie_rank, _ = prefix(eq)
    sel = jnp.logical_or(gt, jnp.logical_and(eq, tie_rank < need))
    pos, off = prefix(sel)
    slot = jnp.where(sel, pos, -1.0)
    slot_ref[...] = slot.astype(I32)
    offs_ref[...] = off.astype(I32)
    rs_scr[...] = slot
    pad = jnp.full((LANES - N_EXPERTS, LANES), -1.0, F32)

    def tr(c, carry):
        tile = jnp.concatenate([rs_scr[c], pad], axis=0)
        slot_tm_ref[pl.ds(pl.multiple_of(c * LANES, LANES), LANES), :] = tile.T.astype(I32)
        return carry

    lax.fori_loop(0, nch, tr, 0)


def _route_call(afft, n_groups, cap):
    nchunks = afft.shape[0]
    nch = nchunks // n_groups
    tri = jnp.asarray(np.triu(np.ones((LANES, LANES), np.float32)), BF16)
    ones = jnp.ones((LANES, LANES), BF16)
    low = jnp.asarray(np.tril(np.ones((nch, nch), np.float32), -1), BF16)
    body = functools.partial(_route_body, nch=nch, cap=cap)
    blk = pl.BlockSpec((nch, N_EXPERTS, LANES), lambda b: (b, 0, 0))
    return pl.pallas_call(
        body,
        grid=(n_groups,),
        in_specs=[blk, _const_spec((LANES, LANES)), _const_spec((LANES, LANES)), _const_spec((nch, nch))],
        out_specs=[blk, blk, pl.BlockSpec((nch * LANES, LANES), lambda b: (b, 0))],
        out_shape=[jax.ShapeDtypeStruct((nchunks, N_EXPERTS, LANES), I32),
                   jax.ShapeDtypeStruct((nchunks, N_EXPERTS, LANES), I32),
                   jax.ShapeDtypeStruct((nchunks * LANES, LANES), I32)],
        scratch_shapes=[pltpu.VMEM((nch, N_EXPERTS, LANES), F32), pltpu.VMEM((nch, N_EXPERTS, LANES), F32)],
        compiler_params=_cparams(("arbitrary",)),
        name="route",
    )(afft, tri, ones, low)


def _dispatch_body(starts_ref, ends_ref, hn_ref, slot_ref, x_ref, *, tm, sb, tiles_per_group):
    b, eg, j = pl.program_id(0), pl.program_id(1), pl.program_id(2)
    w = DISP_W

    @pl.when(j == 0)
    def _():
        x_ref[...] = jnp.zeros_like(x_ref)

    iota_w = lax.broadcasted_iota(I32, (w, 1), 0)
    cpt = tm // LANES
    for t in range(sb // tm):
        tile = j * (sb // tm) + t
        hn_t = hn_ref[t * tm:(t + 1) * tm, :]
        base, srows, ends = [], [], []
        for ee in range(DISP_EG):
            e = eg * DISP_EG + ee
            idx = (b * tiles_per_group + tile) * N_EXPERTS + e
            base.append(pl.multiple_of((starts_ref[idx] // 16) * 16, 16))
            ends.append(ends_ref[idx])
            srows.append(jnp.concatenate([slot_ref[t * cpt + c, pl.ds(e, 1), :] for c in range(cpt)], axis=1))
        oh = jnp.concatenate([(srows[ee] - base[ee] == iota_w).astype(BF16) for ee in range(DISP_EG)], axis=0)
        res = jnp.dot(oh, hn_t, preferred_element_type=F32)
        for ee in range(DISP_EG):
            x_ref[ee, pl.ds(base[ee], w), :] += res[ee * w:(ee + 1) * w].astype(x_ref.dtype)

            def more(k, carry, ee=ee):
                bk = pl.multiple_of(base[ee] + k * w, 16)
                ohk = (srows[ee] - bk == iota_w).astype(BF16)
                x_ref[ee, pl.ds(bk, w), :] += jnp.dot(ohk, hn_t, preferred_element_type=F32).astype(x_ref.dtype)
                return carry

            lax.fori_loop(1, (ends[ee] - base[ee] + w - 1) // w, more, 0)


def _dispatch_call(starts, ends, hn, slot, n_groups, cap):
    n = hn.shape[0]
    ng_rows = n // n_groups
    tm, sb = PROJ_TM, DISP_SB
    nsb = ng_rows // sb
    tiles_per_group = ng_rows // tm
    cps = sb // LANES
    body = functools.partial(_dispatch_body, tm=tm, sb=sb, tiles_per_group=tiles_per_group)
    sq = pl.Squeezed()
    grid_spec = pltpu.PrefetchScalarGridSpec(
        num_scalar_prefetch=2,
        grid=(n_groups, N_EXPERTS // DISP_EG, nsb),
        in_specs=[pl.BlockSpec((sb, D_MODEL), lambda b, g, j, *_: (b * nsb + j, 0)),
                  pl.BlockSpec((cps, N_EXPERTS, LANES), lambda b, g, j, *_: (b * nsb + j, 0, 0))],
        out_specs=pl.BlockSpec((sq, DISP_EG, cap + DISP_W, D_MODEL), lambda b, g, j, *_: (b, g, 0, 0)),
    )
    return pl.pallas_call(
        body,
        grid_spec=grid_spec,
        out_shape=jax.ShapeDtypeStruct((n_groups, N_EXPERTS, cap + DISP_W, D_MODEL), BF16),
        compiler_params=_cparams(("arbitrary", "arbitrary", "arbitrary")),
        name="dispatch",
    )(starts, ends, hn, slot)


def _ffn_body(x_ref, wg_ref, wu_ref, wd_ref, y_ref, acc, *, cap):
    f = pl.program_id(2)
    nf = pl.num_programs(2)
    @pl.when(f == 0)
    def _():
        acc[...] = jnp.zeros_like(acc)

    wg = wg_ref[...].astype(BF16)
    wu = wu_ref[...].astype(BF16)
    wd = wd_ref[...].astype(BF16)
    for rb in range(cap // FFN_RB):
        rows = slice(rb * FFN_RB, (rb + 1) * FFN_RB)
        xb = x_ref[rows, :]
        g = jnp.dot(xb, wg, preferred_element_type=F32)
        u = jnp.dot(xb, wu, preferred_element_type=F32)
        hid = (g * jax.nn.sigmoid(g) * u).astype(BF16)
        acc[rows, :] += jnp.dot(hid, wd, preferred_element_type=F32)

    @pl.when(f == nf - 1)
    def _():
        y_ref[...] = acc[...].astype(y_ref.dtype)


def _ffn_call(xe, w_gate, w_up, w_down, cap):
    n_groups = xe.shape[0]
    fc = FFN_FC
    nf = EXPERT_FF // fc
    body = functools.partial(_ffn_body, cap=cap)
    sq = pl.Squeezed()
    return pl.pallas_call(
        body,
        grid=(n_groups, N_EXPERTS, nf),
        in_specs=[pl.BlockSpec((sq, sq, cap, D_MODEL), lambda b, e, f: (b, e, 0, 0)),
                  pl.BlockSpec((sq, D_MODEL, fc), lambda b, e, f: (e, 0, f)),
                  pl.BlockSpec((sq, D_MODEL, fc), lambda b, e, f: (e, 0, f)),
                  pl.BlockSpec((sq, fc, D_MODEL), lambda b, e, f: (e, f, 0))],
        out_specs=pl.BlockSpec((sq, sq, cap, D_MODEL), lambda b, e, f: (b, e, 0, 0)),
        out_shape=jax.ShapeDtypeStruct((n_groups, N_EXPERTS, cap, D_MODEL), BF16),
        scratch_shapes=[pltpu.VMEM((cap, D_MODEL), F32)],
        compiler_params=_cparams(("arbitrary", "arbitrary", "arbitrary")),
        name="ffn",
    )(xe, w_gate, w_up, w_down)


def _combine_body(astart_ref, nrounds_ref, x1_ref, aff_ref, slot_ref, y_hbm, outa_ref, outb_ref,
                  ybuf, sem, yext, sem_ext, accs, *, tm, cap, tiles_per_group, n_first):
    i = pl.program_id(0)
    nt = pl.num_programs(0)
    w = COMB_W

    def window_copy(tile, e, k, dst, dsem):
        a0 = astart_ref[tile * N_EXPERTS + e]
        a = pl.multiple_of(jnp.minimum(a0 + k * w, cap - w), 16)
        return pltpu.make_async_copy(y_hbm.at[tile // tiles_per_group, e, pl.ds(a, w), :], dst, dsem)

    def first_copy(tile, e):
        sl = tile % 2
        return window_copy(tile, e, 0, ybuf.at[sl, pl.ds(e * w, w)], sem.at[sl, e])

    @pl.when(i == 0)
    def _():
        for e in range(N_EXPERTS):
            first_copy(i, e).start()

    @pl.when(i + 1 < nt)
    def _():
        for e in range(N_EXPERTS):
            first_copy(i + 1, e).start()

    lane = lax.broadcasted_iota(I32, (1, LANES), 1)
    lo_lane = lane < w
    slot_t = slot_ref[...]
    gate = aff_ref[...]
    g_hi = gate.astype(BF16).astype(F32)
    g_lo = gate - g_hi

    def expand(k, ywin):
        his, los = [], []
        for ep in range(N_EXPERTS // 2):
            rel, skip = [], []
            for e in (2 * ep, 2 * ep + 1):
                a0 = astart_ref[i * N_EXPERTS + e]
                lo = a0 + k * w
                a = jnp.minimum(lo, cap - w)
                rel.append(slot_t[:, e:e + 1] - a)
                skip.append(lo - a)
            fresh = lane >= jnp.where(lo_lane, skip[0], skip[1] + w)
            match = jnp.logical_and(jnp.where(lo_lane, rel[0], rel[1] + w) == lane, fresh)
            his.append(jnp.where(match, jnp.where(lo_lane, g_hi[:, 2 * ep:2 * ep + 1],
                                                  g_hi[:, 2 * ep + 1:2 * ep + 2]), 0.0).astype(BF16))
            los.append(jnp.where(match, jnp.where(lo_lane, g_lo[:, 2 * ep:2 * ep + 1],
                                                  g_lo[:, 2 * ep + 1:2 * ep + 2]), 0.0).astype(BF16))
        return (jnp.dot(jnp.concatenate(his, axis=1), ywin, preferred_element_type=F32)
                + jnp.dot(jnp.concatenate(los, axis=1), ywin, preferred_element_type=F32))

    for e in range(N_EXPERTS):
        first_copy(i, e).wait()
    accs[...] = x1_ref[...] + expand(0, ybuf[i % 2])

    def extra_round(k, carry):
        for e in range(N_EXPERTS):
            window_copy(i, e, k, yext.at[pl.ds(e * w, w)], sem_ext.at[e]).start()
        for e in range(N_EXPERTS):
            window_copy(i, e, k, yext.at[pl.ds(e * w, w)], sem_ext.at[e]).wait()
        accs[...] += expand(k, yext[...])
        return carry

    lax.fori_loop(1, nrounds_ref[i], extra_round, 0)

    @pl.when(i < n_first)
    def _():
        outa_ref[...] = accs[...]

    @pl.when(i >= n_first)
    def _():
        outb_ref[...] = accs[...]


def _combine_call(astart, nrounds, x1, aff, slot_tm, y, n_groups, cap, na):
    n = x1.shape[0]
    tm = PROJ_TM
    nt = n // tm
    n_first = na // tm
    tiles_per_group = nt // n_groups
    body = functools.partial(_combine_body, tm=tm, cap=cap, tiles_per_group=tiles_per_group, n_first=n_first)
    outa_spec, outb_spec = _two_array_specs(tm, n_first, D_MODEL, prefetch=True)
    grid_spec = pltpu.PrefetchScalarGridSpec(
        num_scalar_prefetch=2,
        grid=(nt,),
        in_specs=[pl.BlockSpec((tm, D_MODEL), lambda i, *_: (i, 0)),
                  pl.BlockSpec((tm, LANES), lambda i, *_: (i, 0)),
                  pl.BlockSpec((tm, LANES), lambda i, *_: (i, 0)),
                  pl.BlockSpec(memory_space=pl.ANY)],
        out_specs=[outa_spec, outb_spec],
        scratch_shapes=[pltpu.VMEM((2, N_EXPERTS * COMB_W, D_MODEL), BF16),
                        pltpu.SemaphoreType.DMA((2, N_EXPERTS)),
                        pltpu.VMEM((N_EXPERTS * COMB_W, D_MODEL), BF16),
                        pltpu.SemaphoreType.DMA((N_EXPERTS,)),
                        pltpu.VMEM((tm, D_MODEL), F32)],
    )
    return pl.pallas_call(
        body,
        grid_spec=grid_spec,
        out_shape=[jax.ShapeDtypeStruct((na, D_MODEL), F32), jax.ShapeDtypeStruct((n - na, D_MODEL), F32)],
        compiler_params=_cparams(("arbitrary",)),
        name="combine",
    )(astart, nrounds, x1, aff, slot_tm, y)


def _layer(xa, xb, seqs, n_groups, mix_norm_g, w_in, q_norm_g, k_norm_g, conv_w, w_attn_branch, w_conv_branch,
           w_out, ffn_norm_g, w_router, w_gate, w_up, w_down):
    na = xa.shape[0]
    n = na + xb.shape[0]
    group_rows = n // n_groups
    cap = CAPACITY_FACTOR * group_rows // N_EXPERTS
    tm = PROJ_TM

    qg = jnp.tile(q_norm_g, ATT_W // HEAD_DIM)[None, :]
    kg = jnp.tile(k_norm_g, ATT_W // HEAD_DIM)[None, :]
    q, k, v, sga, gcb = _proj_call(xa, xb, seqs, mix_norm_g[None, :], w_in.astype(BF16), qg, kg, conv_w,
                                   w_conv_branch.astype(BF16))
    o = [_attn_call(q, k, v, seqs, gi) for gi in range(len(DILATIONS))]

    wr = jnp.pad(w_router, ((0, 0), (0, LANES - N_EXPERTS)))
    wrh = wr.astype(BF16)
    wrl = (wr - wrh.astype(F32)).astype(BF16)
    x1, hn2, aff, afft = _merge_call(o[0], o[1], o[2], sga, gcb, xa, xb, w_attn_branch.astype(BF16),
                                     w_out.astype(BF16), ffn_norm_g[None, :], wrh, wrl)

    slot, offs, slot_tm = _route_call(afft, n_groups, cap)

    cpt = tm // LANES
    tiles_per_group = group_rows // tm
    starts = offs[::cpt, :, 0].reshape(n_groups, tiles_per_group, N_EXPERTS)
    ends = jnp.concatenate([starts[:, 1:], jnp.full((n_groups, 1, N_EXPERTS), cap, I32)], axis=1)
    xe = _dispatch_call(starts.reshape(-1), ends.reshape(-1), hn2, slot, n_groups, cap)
    y = _ffn_call(xe, w_gate, w_up, w_down, cap)

    astart = jnp.minimum((starts // 16) * 16, cap - COMB_W)
    nrounds = jnp.maximum(jnp.max((ends - astart + COMB_W - 1) // COMB_W, axis=-1), 1)
    return _combine_call(astart.reshape(-1), nrounds.reshape(-1), x1, aff, slot_tm, y, n_groups, cap, na)


def _run(x_prompt, x_sample, params):
    bp, sp, d = x_prompt.shape
    bs, ss, _ = x_sample.shape
    seqs, pos = [], 0
    for b, s in ((bp, sp), (bs, ss)):
        for _ in range(b):
            seqs.append((pos, s))
            pos += s
    assert bp * sp == bs * ss, "request groups are routed with one capacity"
    xa, xb = x_prompt.reshape(bp * sp, d), x_sample.reshape(bs * ss, d)
    depth = params[0].shape[0]
    for layer in range(depth):
        xa, xb = _layer(xa, xb, tuple(seqs), 2, *[p[layer] for p in params])
    return xa.reshape(bp, sp, d), xb.reshape(bs, ss, d)


def kernel(x_prompt, x_sample, mix_norm_g, w_in, q_norm_g, k_norm_g, conv_w, w_attn_branch, w_conv_branch, w_out,
           ffn_norm_g, w_router, w_gate, w_up, w_down):
    params = (mix_norm_g, w_in, q_norm_g, k_norm_g, conv_w, w_attn_branch, w_conv_branch, w_out,
              ffn_norm_g, w_router, w_gate, w_up, w_down)
    return _run(x_prompt, x_sample, params)
```

```python
import functools

import numpy as np
import jax
import jax.numpy as jnp
from jax import lax
from jax.experimental import pallas as pl
from jax.experimental.pallas import tpu as pltpu

F32 = jnp.float32
BF16 = jnp.bfloat16
I32 = jnp.int32

D_MODEL = 1024
HEAD_DIM = 64
N_SLOTS = 8
DILATIONS = (1, 4, 16)
KEYS_PER_SIDE = 64
GROUP_W = N_SLOTS * HEAD_DIM
ATT_W = len(DILATIONS) * GROUP_W
CONV_W = D_MODEL
IN_W = 3 * ATT_W + 3 * CONV_W + 2 * D_MODEL
N_EXPERTS = 16
EXPERT_FF = 2048
CAPACITY_FACTOR = 2
RMS_EPS = 1e-6
NEG = -1e30

LANES = 128
PAIR_SLABS = GROUP_W // LANES
VMEM_LIMIT = 58 * 1024 * 1024

PROJ_TM = 256
PROJ_HALO = 16
ATT_TB_MAX = 2048
ATT_TQ = 128
FFN_FC = 512
FFN_RB = 512
DISP_SB = 2048
DISP_EG = 4
DISP_W = 96
COMB_W = 64


def _cparams(sem):
    return pltpu.CompilerParams(dimension_semantics=sem, vmem_limit_bytes=VMEM_LIMIT)


def _any_eq(i, values):
    return functools.reduce(jnp.logical_or, [i == v for v in values])


def _const_spec(shape):
    nd = len(shape)
    return pl.BlockSpec(shape, lambda *a: (0,) * nd, pipeline_mode=pl.Buffered(1))


def _two_array_specs(block_rows, n_first_blocks, cols, prefetch=False):
    if prefetch:
        first = pl.BlockSpec((block_rows, cols), lambda i, *_: (jnp.minimum(i, n_first_blocks - 1), 0))
        second = pl.BlockSpec((block_rows, cols), lambda i, *_: (jnp.maximum(i - n_first_blocks, 0), 0))
    else:
        first = pl.BlockSpec((block_rows, cols), lambda i: (jnp.minimum(i, n_first_blocks - 1), 0))
        second = pl.BlockSpec((block_rows, cols), lambda i: (jnp.maximum(i - n_first_blocks, 0), 0))
    return first, second


def _proj_body(xa_ref, xb_ref, xpa_ref, xpb_ref, xna_ref, xnb_ref, ng_ref, w_ref, qg_ref, kg_ref, cw_ref, wcb_ref,
               q0_ref, k0_ref, v0_ref, q1_ref, k1_ref, v1_ref, q2_ref, k2_ref, v2_ref, sga_ref, gcb_ref, hs, hp,
               *, tm, halo, n_first, first_tiles, last_tiles):
    i = pl.program_id(0)
    is_first = _any_eq(i, first_tiles)
    is_last = _any_eq(i, last_tiles)
    in_a = i < n_first

    def pick(a_ref, b_ref):
        return jnp.where(in_a, a_ref[...], b_ref[...])

    x_ext = jnp.concatenate([pick(xpa_ref, xpb_ref), pick(xa_ref, xb_ref), pick(xna_ref, xnb_ref)], axis=0)
    ms = jnp.mean(x_ext * x_ext, axis=-1, keepdims=True)
    hn_f32 = x_ext * lax.rsqrt(ms + RMS_EPS) * ng_ref[...]
    hn_ext = hn_f32.astype(BF16)
    hn = hn_ext[halo:halo + tm]

    for sd in range(D_MODEL // LANES):
        hs[sd] = hn_f32[halo:halo + tm, sd * LANES:(sd + 1) * LANES]
    hn_of_group = [hn]
    for gi, r in enumerate(DILATIONS):
        if r == 1:
            continue
        for c in range(r):
            for sd in range(D_MODEL // LANES):
                hp[gi - 1, c * (tm // r):(c + 1) * (tm // r), sd * LANES:(sd + 1) * LANES] = (
                    hs[sd, pl.ds(c, tm // r, stride=r), :].astype(BF16))
        hn_of_group.append(hp[gi - 1])

    def proj(lhs, a, b):
        return jnp.dot(lhs, w_ref[:, a:b], preferred_element_type=F32)

    lo_lane = lax.broadcasted_iota(I32, (1, LANES), 1) < HEAD_DIM

    def head_norm(p, g_ref, col0, scale):
        out = []
        for s in range(PAIR_SLABS):
            xs = p[:, s * LANES:(s + 1) * LANES]
            x2 = xs * xs
            tot = jnp.sum(x2, axis=-1, keepdims=True)
            lo = jnp.sum(jnp.where(lo_lane, x2, 0.0), axis=-1, keepdims=True)
            hi = tot - lo
            r = jnp.where(lo_lane, lax.rsqrt(lo * (1.0 / HEAD_DIM) + RMS_EPS),
                          lax.rsqrt(hi * (1.0 / HEAD_DIM) + RMS_EPS))
            y = xs * r * g_ref[:, col0 + s * LANES:col0 + (s + 1) * LANES]
            out.append(y * scale if scale != 1.0 else y)
        return out

    def emit(gi, out_ref, slabs):
        r = DILATIONS[gi]
        for s in range(PAIR_SLABS):
            y = slabs[s].astype(out_ref.dtype)
            if r == 1:
                out_ref[:, s * LANES:(s + 1) * LANES] = y
            else:
                for c in range(r):
                    out_ref[c, :, s * LANES:(s + 1) * LANES] = y[c * (tm // r):(c + 1) * (tm // r)]

    q_refs, k_refs, v_refs = (q0_ref, q1_ref, q2_ref), (k0_ref, k1_ref, k2_ref), (v0_ref, v1_ref, v2_ref)
    for gi in range(len(DILATIONS)):
        c0 = gi * GROUP_W
        lhs = hn_of_group[gi]
        emit(gi, q_refs[gi], head_norm(proj(lhs, c0, c0 + GROUP_W), qg_ref, c0, HEAD_DIM ** -0.5))
        emit(gi, k_refs[gi], head_norm(proj(lhs, ATT_W + c0, ATT_W + c0 + GROUP_W), kg_ref, c0, 1.0))
        pv = proj(lhs, 2 * ATT_W + c0, 2 * ATT_W + c0 + GROUP_W)
        emit(gi, v_refs[gi], [pv[:, s * LANES:(s + 1) * LANES] for s in range(PAIR_SLABS)])

    c0 = 3 * ATT_W
    cb = proj(hn, c0, c0 + CONV_W)
    cc = proj(hn_ext, c0 + CONV_W, c0 + 2 * CONV_W)
    cx = proj(hn_ext, c0 + 2 * CONV_W, c0 + 3 * CONV_W)
    u_ext = cc * cx
    row = lax.broadcasted_iota(I32, (tm + 2 * halo, 1), 0)
    keep = jnp.logical_and(jnp.logical_or(row >= halo, jnp.logical_not(is_first)),
                           jnp.logical_or(row < halo + tm, jnp.logical_not(is_last)))
    u_ext = jnp.where(keep, u_ext, 0.0)
    cy = (u_ext[halo - 1:halo - 1 + tm] * cw_ref[0:1, :]
          + u_ext[halo:halo + tm] * cw_ref[1:2, :]
          + u_ext[halo + 1:halo + 1 + tm] * cw_ref[2:3, :])
    conv_in = (cb * cy).astype(BF16)
    cbr = jnp.dot(conv_in, wcb_ref[...], preferred_element_type=F32)
    g0 = c0 + 3 * CONV_W
    sga_ref[...] = jax.nn.sigmoid(proj(hn, g0, g0 + D_MODEL))
    gcb_ref[...] = jax.nn.sigmoid(proj(hn, g0 + D_MODEL, g0 + 2 * D_MODEL)) * cbr


def _proj_call(xa, xb, seqs, ng, w_in, qg, kg, cw, wcb):
    na, nb_rows = xa.shape[0], xb.shape[0]
    n = na + nb_rows
    tm, halo = PROJ_TM, PROJ_HALO
    nt = n // tm
    n_first = na // tm
    hb = tm // halo
    first_tiles = tuple(s // tm for s, _ in seqs)
    last_tiles = tuple((s + l) // tm - 1 for s, l in seqs)
    nha, nhb = na // halo, nb_rows // halo
    body = functools.partial(_proj_body, tm=tm, halo=halo, n_first=n_first,
                             first_tiles=first_tiles, last_tiles=last_tiles)
    cur_a, cur_b = _two_array_specs(tm, n_first, D_MODEL)
    prev_a = pl.BlockSpec((halo, D_MODEL), lambda i: (jnp.clip(i * hb - 1, 0, nha - 1), 0))
    prev_b = pl.BlockSpec((halo, D_MODEL), lambda i: (jnp.clip((i - n_first) * hb - 1, 0, nhb - 1), 0))
    next_a = pl.BlockSpec((halo, D_MODEL), lambda i: (jnp.clip((i + 1) * hb, 0, nha - 1), 0))
    next_b = pl.BlockSpec((halo, D_MODEL), lambda i: (jnp.clip((i + 1 - n_first) * hb, 0, nhb - 1), 0))
    out_specs, out_shape = [], []
    for gi, r in enumerate(DILATIONS):
        for _ in range(3):
            if r == 1:
                out_specs.append(pl.BlockSpec((tm, GROUP_W), lambda i: (i, 0)))
                out_shape.append(jax.ShapeDtypeStruct((n, GROUP_W), BF16))
            else:
                out_specs.append(pl.BlockSpec((r, tm // r, GROUP_W), lambda i: (0, i, 0)))
                out_shape.append(jax.ShapeDtypeStruct((r, n // r, GROUP_W), BF16))
    out_specs += [pl.BlockSpec((tm, D_MODEL), lambda i: (i, 0))] * 2
    out_shape += [jax.ShapeDtypeStruct((n, D_MODEL), F32)] * 2
    return pl.pallas_call(
        body,
        grid=(nt,),
        in_specs=[cur_a, cur_b, prev_a, prev_b, next_a, next_b,
                  _const_spec((1, D_MODEL)), _const_spec((D_MODEL, IN_W)), _const_spec((1, ATT_W)),
                  _const_spec((1, ATT_W)), _const_spec((3, CONV_W)), _const_spec((CONV_W, D_MODEL))],
        out_specs=out_specs,
        out_shape=out_shape,
        scratch_shapes=[pltpu.VMEM((D_MODEL // LANES, tm, LANES), F32),
                        pltpu.VMEM((len(DILATIONS) - 1, tm, D_MODEL), BF16)],
        compiler_params=_cparams(("arbitrary",)),
        name="proj",
    )(xa, xb, xa, xb, xa, xb, ng, w_in, qg, kg, cw, wcb)


def _attn_body(q_ref, kp_ref, kc_ref, kn_ref, vp_ref, vc_ref, vn_ref, bias_ref, o_ref, kw, vw,
               *, tb, first_tiles, last_tiles):
    i = pl.program_id(1)
    is_first = _any_eq(i, first_tiles)
    is_last = _any_eq(i, last_tiles)
    h = KEYS_PER_SIDE
    tq = ATT_TQ
    kwin = tq + 2 * h
    nq = tb // tq

    for s in range(PAIR_SLABS):
        sl = slice(s * LANES, (s + 1) * LANES)
        for dst, p_ref, c_ref, n_ref in ((kw, kp_ref, kc_ref, kn_ref), (vw, vp_ref, vc_ref, vn_ref)):
            dst[s, 0:h, :] = p_ref[:, sl]
            dst[s, h:h + tb, :] = c_ref[:, sl]
            dst[s, h + tb:2 * h + tb, :] = n_ref[:, sl]

    lane = lax.broadcasted_iota(I32, (1, LANES), 1)
    lo_lane = lane < HEAD_DIM
    col = lax.broadcasted_iota(I32, (1, kwin), 1)

    def head_pair(q2, k2, v2, biases):
        o_acc = None
        lse2 = None
        for half in range(2):
            mask = lo_lane if half == 0 else jnp.logical_not(lo_lane)
            qm = jnp.where(mask, q2, jnp.zeros_like(q2))
            sc = lax.dot_general(qm, k2, (((1,), (1,)), ((), ())), preferred_element_type=F32)
            sc = sc + biases[half]
            m = jnp.max(sc, axis=-1, keepdims=True)
            p = jnp.exp(sc - m)
            l = jnp.sum(p, axis=-1, keepdims=True)
            vm = jnp.where(mask, v2, jnp.zeros_like(v2))
            pv = jnp.dot(p.astype(BF16), vm, preferred_element_type=F32)
            contrib = pv * (1.0 / l)
            lse = m + jnp.log(l)
            o_acc = contrib if o_acc is None else o_acc + contrib
            lse2 = jnp.broadcast_to(lse, (tq, LANES)) if lse2 is None else jnp.where(mask, lse, lse2)
        return o_acc, lse2

    def step(qb, carry):
        pen_lo = jnp.where(jnp.logical_and(is_first, qb == 0), NEG, 0.0)
        pen_hi = jnp.where(jnp.logical_and(is_last, qb == nq - 1), NEG, 0.0)
        edge = (jnp.where(col < h, pen_lo, 0.0) + jnp.where(col >= h + tq, pen_hi, 0.0))
        q0 = pl.multiple_of(qb * tq, tq)
        for s in range(PAIR_SLABS):
            biases = [bias_ref[2 * s] + edge, bias_ref[2 * s + 1] + edge]
            o2, lse2 = head_pair(q_ref[pl.ds(q0, tq), s * LANES:(s + 1) * LANES],
                                 kw[s, pl.ds(q0, kwin), :], vw[s, pl.ds(q0, kwin), :], biases)
            o_ref[s, pl.ds(q0, tq), :] = o2
            o_ref[PAIR_SLABS + s, pl.ds(q0, tq), :] = lse2
        return carry

    lax.fori_loop(0, nq, step, 0)


def _alibi_bias(gi, r):
    n_heads = len(DILATIONS) * N_SLOTS
    slopes = (2.0 ** (-8.0 * (np.arange(n_heads, dtype=np.float32) + 1.0) / n_heads)).astype(np.float32)
    slopes = slopes.reshape(len(DILATIONS), N_SLOTS)[gi]
    qi = np.arange(ATT_TQ)[:, None]
    kj = np.arange(ATT_TQ + 2 * KEYS_PER_SIDE)[None, :]
    off = kj - KEYS_PER_SIDE - qi
    dist = (r * np.abs(off)).astype(np.float32)
    bias = -slopes[:, None, None] * dist[None]
    bias = np.where((np.abs(off) <= KEYS_PER_SIDE)[None], bias, np.float32(NEG))
    return jnp.asarray(bias, F32)


def _attn_call(q, k, v, seqs, gi):
    r, rows, _ = q.shape
    h = KEYS_PER_SIDE
    tb = min([ATT_TB_MAX] + [l // r for _, l in seqs])
    assert all((s // r) % tb == 0 and (l // r) % tb == 0 for s, l in seqs)
    nt = rows // tb
    hb = tb // h
    nhb = rows // h
    first_tiles = tuple((s // r) // tb for s, _ in seqs)
    last_tiles = tuple(((s + l) // r) // tb - 1 for s, l in seqs)
    kwin = ATT_TQ + 2 * h
    body = functools.partial(_attn_body, tb=tb, first_tiles=first_tiles, last_tiles=last_tiles)
    sq = pl.Squeezed()
    cur = pl.BlockSpec((sq, tb, GROUP_W), lambda c, i: (c, i, 0))
    prev = pl.BlockSpec((sq, h, GROUP_W), lambda c, i: (c, jnp.maximum(i * hb - 1, 0), 0))
    nxt = pl.BlockSpec((sq, h, GROUP_W), lambda c, i: (c, jnp.minimum((i + 1) * hb, nhb - 1), 0))
    return pl.pallas_call(
        body,
        grid=(r, nt),
        in_specs=[cur, prev, cur, nxt, prev, cur, nxt, _const_spec((N_SLOTS, ATT_TQ, kwin))],
        out_specs=pl.BlockSpec((2 * PAIR_SLABS, sq, tb, LANES), lambda c, i: (0, c, i, 0)),
        out_shape=jax.ShapeDtypeStruct((2 * PAIR_SLABS, r, rows, LANES), F32),
        scratch_shapes=[pltpu.VMEM((PAIR_SLABS, tb + 2 * h, LANES), BF16),
                        pltpu.VMEM((PAIR_SLABS, tb + 2 * h, LANES), BF16)],
        compiler_params=_cparams(("arbitrary", "arbitrary")),
        name=f"attn_d{r}",
    )(q, k, k, k, v, v, v, _alibi_bias(gi, r))


def _split_bf16(x):
    hi = x.astype(BF16)
    lo = (x - hi.astype(F32)).astype(BF16)
    return hi, lo


def _merge_body(o0_ref, o1_ref, o2_ref, sga_ref, gcb_ref, xa_ref, xb_ref, wab_ref, wout_ref, g2_ref,
                wrh_ref, wrl_ref, x1_ref, hn_ref, aff_ref, afft_ref, t1, t2, *, tm, n_first):
    i = pl.program_id(0)
    for o_ref, scr, r in ((o1_ref, t1, DILATIONS[1]), (o2_ref, t2, DILATIONS[2])):
        for s in range(2 * PAIR_SLABS):
            for c in range(r):
                scr[s, pl.ds(c, tm // r, stride=r), :] = o_ref[s, c]
    slabs = []
    for s in range(PAIR_SLABS):
        l0, l1, l2 = o0_ref[PAIR_SLABS + s, 0], t1[PAIR_SLABS + s], t2[PAIR_SLABS + s]
        mx = jnp.maximum(jnp.maximum(l0, l1), l2)
        e0, e1, e2 = jnp.exp(l0 - mx), jnp.exp(l1 - mx), jnp.exp(l2 - mx)
        num = e0 * o0_ref[s, 0] + e1 * t1[s] + e2 * t2[s]
        slabs.append(num / (e0 + e1 + e2))
    att = jnp.concatenate(slabs, axis=-1).astype(BF16)
    a_br = jnp.dot(att, wab_ref[...], preferred_element_type=F32)
    merged = (sga_ref[...] * a_br + gcb_ref[...]).astype(BF16)
    x = jnp.where(i < n_first, xa_ref[...], xb_ref[...])
    x1 = x + jnp.dot(merged, wout_ref[...], preferred_element_type=F32)
    x1_ref[...] = x1
    ms = jnp.mean(x1 * x1, axis=-1, keepdims=True)
    hn2 = x1 * lax.rsqrt(ms + RMS_EPS) * g2_ref[...]
    hn_ref[...] = hn2.astype(BF16)
    hh, hl = _split_bf16(hn2)
    logits = (jnp.dot(hh, wrh_ref[...], preferred_element_type=F32)
              + jnp.dot(hl, wrh_ref[...], preferred_element_type=F32)
              + jnp.dot(hh, wrl_ref[...], preferred_element_type=F32))
    lane = lax.broadcasted_iota(I32, (1, LANES), 1)
    logits = jnp.where(lane < N_EXPERTS, logits, NEG)
    e = jnp.exp(logits - jnp.max(logits, axis=-1, keepdims=True))
    aff = e / jnp.sum(e, axis=-1, keepdims=True)
    aff_ref[...] = aff
    aff_t = aff.T
    for c in range(tm // LANES):
        afft_ref[c] = aff_t[0:N_EXPERTS, c * LANES:(c + 1) * LANES]


def _merge_call(o0, o1, o2, sga, gcb, xa, xb, wab, wout, g2, wrh, wrl):
    n = sga.shape[0]
    tm = PROJ_TM
    nt = n // tm
    n_first = xa.shape[0] // tm
    body = functools.partial(_merge_body, tm=tm, n_first=n_first)
    ns = 2 * PAIR_SLABS
    o_specs = [pl.BlockSpec((ns, r, tm // r, LANES), lambda i: (0, 0, i, 0)) for r in DILATIONS]
    rowspec = pl.BlockSpec((tm, D_MODEL), lambda i: (i, 0))
    xa_spec, xb_spec = _two_array_specs(tm, n_first, D_MODEL)
    return pl.pallas_call(
        body,
        grid=(nt,),
        in_specs=o_specs + [rowspec, rowspec, xa_spec, xb_spec,
                            _const_spec((GROUP_W, D_MODEL)), _const_spec((D_MODEL, D_MODEL)),
                            _const_spec((1, D_MODEL)), _const_spec((D_MODEL, LANES)),
                            _const_spec((D_MODEL, LANES))],
        out_specs=[rowspec, rowspec,
                   pl.BlockSpec((tm, LANES), lambda i: (i, 0)),
                   pl.BlockSpec((tm // LANES, N_EXPERTS, LANES), lambda i: (i, 0, 0))],
        out_shape=[jax.ShapeDtypeStruct((n, D_MODEL), F32),
                   jax.ShapeDtypeStruct((n, D_MODEL), BF16),
                   jax.ShapeDtypeStruct((n, LANES), F32),
                   jax.ShapeDtypeStruct((n // LANES, N_EXPERTS, LANES), F32)],
        scratch_shapes=[pltpu.VMEM((ns, tm, LANES), F32), pltpu.VMEM((ns, tm, LANES), F32)],
        compiler_params=_cparams(("arbitrary",)),
        name="merge",
    )(o0, o1, o2, sga, gcb, xa, xb, wab, wout, g2, wrh, wrl)


def _route_body(aff_ref, tri_ref, ones_ref, low_ref, slot_ref, offs_ref, slot_tm_ref, rs_scr, off_scr,
                *, nch, cap):
    a = aff_ref[...]

    def count(mask):
        c = jnp.sum(mask.astype(F32), axis=0, keepdims=True)
        return jnp.sum(c, axis=2, keepdims=True)

    cur = jnp.zeros((1, N_EXPERTS, 1), I32)
    for bit in range(30, -1, -1):
        cand = cur | (1 << bit)
        cur = jnp.where(count(a >= lax.bitcast_convert_type(cand, F32)) >= cap, cand, cur)
    tau = lax.bitcast_convert_type(cur, F32)
    gt = a > tau
    eq = a == tau
    need = cap - count(gt)

    def prefix(mask):
        m = mask.astype(BF16).reshape(nch * N_EXPERTS, LANES)
        incl = jnp.dot(m, tri_ref[...], preferred_element_type=F32).reshape(nch, N_EXPERTS, LANES)
        rs_scr[...] = jnp.dot(m, ones_ref[...], preferred_element_type=F32).reshape(nch, N_EXPERTS, LANES)
        for e in range(N_EXPERTS):
            off_scr[:, e, :] = jnp.dot(low_ref[...], rs_scr[:, e, :].astype(BF16), preferred_element_type=F32)
        off = off_scr[...]
        return incl + off - mask.astype(F32), off

    tie_rank, _ = prefix(eq)
    sel = jnp.logical_or(gt, jnp.logical_and(eq, tie_rank < need))
    pos, off = prefix(sel)
    slot = jnp.where(sel, pos, -1.0)
    slot_ref[...] = slot.astype(I32)
    offs_ref[...] = off.astype(I32)
    rs_scr[...] = slot
    pad = jnp.full((LANES - N_EXPERTS, LANES), -1.0, F32)

    def tr(c, carry):
        tile = jnp.concatenate([rs_scr[c], pad], axis=0)
        slot_tm_ref[pl.ds(pl.multiple_of(c * LANES, LANES), LANES), :] = tile.T.astype(I32)
        return carry

    lax.fori_loop(0, nch, tr, 0)


def _route_call(afft, n_groups, cap):
    nchunks = afft.shape[0]
    nch = nchunks // n_groups
    tri = jnp.asarray(np.triu(np.ones((LANES, LANES), np.float32)), BF16)
    ones = jnp.ones((LANES, LANES), BF16)
    low = jnp.asarray(np.tril(np.ones((nch, nch), np.float32), -1), BF16)
    body = functools.partial(_route_body, nch=nch, cap=cap)
    blk = pl.BlockSpec((nch, N_EXPERTS, LANES), lambda b: (b, 0, 0))
    return pl.pallas_call(
        body,
        grid=(n_groups,),
        in_specs=[blk, _const_spec((LANES, LANES)), _const_spec((LANES, LANES)), _const_spec((nch, nch))],
        out_specs=[blk, blk, pl.BlockSpec((nch * LANES, LANES), lambda b: (b, 0))],
        out_shape=[jax.ShapeDtypeStruct((nchunks, N_EXPERTS, LANES), I32),
                   jax.ShapeDtypeStruct((nchunks, N_EXPERTS, LANES), I32),
                   jax.ShapeDtypeStruct((nchunks * LANES, LANES), I32)],
        scratch_shapes=[pltpu.VMEM((nch, N_EXPERTS, LANES), F32), pltpu.VMEM((nch, N_EXPERTS, LANES), F32)],
        compiler_params=_cparams(("arbitrary",)),
        name="route",
    )(afft, tri, ones, low)


def _dispatch_body(starts_ref, ends_ref, hn_ref, slot_ref, x_ref, *, tm, sb, tiles_per_group):
    b, eg, j = pl.program_id(0), pl.program_id(1), pl.program_id(2)
    w = DISP_W

    @pl.when(j == 0)
    def _():
        x_ref[...] = jnp.zeros_like(x_ref)

    iota_w = lax.broadcasted_iota(I32, (w, 1), 0)
    cpt = tm // LANES
    for t in range(sb // tm):
        tile = j * (sb // tm) + t
        hn_t = hn_ref[t * tm:(t + 1) * tm, :]
        base, srows, ends = [], [], []
        for ee in range(DISP_EG):
            e = eg * DISP_EG + ee
            idx = (b * tiles_per_group + tile) * N_EXPERTS + e
            base.append(pl.multiple_of((starts_ref[idx] // 16) * 16, 16))
            ends.append(ends_ref[idx])
            srows.append(jnp.concatenate([slot_ref[t * cpt + c, pl.ds(e, 1), :] for c in range(cpt)], axis=1))
        oh = jnp.concatenate([(srows[ee] - base[ee] == iota_w).astype(BF16) for ee in range(DISP_EG)], axis=0)
        res = jnp.dot(oh, hn_t, preferred_element_type=F32)
        for ee in range(DISP_EG):
            x_ref[ee, pl.ds(base[ee], w), :] += res[ee * w:(ee + 1) * w].astype(x_ref.dtype)

            def more(k, carry, ee=ee):
                bk = pl.multiple_of(base[ee] + k * w, 16)
                ohk = (srows[ee] - bk == iota_w).astype(BF16)
                x_ref[ee, pl.ds(bk, w), :] += jnp.dot(ohk, hn_t, preferred_element_type=F32).astype(x_ref.dtype)
                return carry

            lax.fori_loop(1, (ends[ee] - base[ee] + w - 1) // w, more, 0)


def _dispatch_call(starts, ends, hn, slot, n_groups, cap):
    n = hn.shape[0]
    ng_rows = n // n_groups
    tm, sb = PROJ_TM, DISP_SB
    nsb = ng_rows // sb
    tiles_per_group = ng_rows // tm
    cps = sb // LANES
    body = functools.partial(_dispatch_body, tm=tm, sb=sb, tiles_per_group=tiles_per_group)
    sq = pl.Squeezed()
    grid_spec = pltpu.PrefetchScalarGridSpec(
        num_scalar_prefetch=2,
        grid=(n_groups, N_EXPERTS // DISP_EG, nsb),
        in_specs=[pl.BlockSpec((sb, D_MODEL), lambda b, g, j, *_: (b * nsb + j, 0)),
                  pl.BlockSpec((cps, N_EXPERTS, LANES), lambda b, g, j, *_: (b * nsb + j, 0, 0))],
        out_specs=pl.BlockSpec((sq, DISP_EG, cap + DISP_W, D_MODEL), lambda b, g, j, *_: (b, g, 0, 0)),
    )
    return pl.pallas_call(
        body,
        grid_spec=grid_spec,
        out_shape=jax.ShapeDtypeStruct((n_groups, N_EXPERTS, cap + DISP_W, D_MODEL), BF16),
        compiler_params=_cparams(("arbitrary", "arbitrary", "arbitrary")),
        name="dispatch",
    )(starts, ends, hn, slot)


def _ffn_body(x_ref, wg_ref, wu_ref, wd_ref, y_ref, acc, *, cap):
    f = pl.program_id(2)
    nf = pl.num_programs(2)

    @pl.when(f == 0)
    def _():
        acc[...] = jnp.zeros_like(acc)

    wg = wg_ref[...].astype(BF16)
    wu = wu_ref[...].astype(BF16)
    wd = wd_ref[...].astype(BF16)
    for rb in range(cap // FFN_RB):
        rows = slice(rb * FFN_RB, (rb + 1) * FFN_RB)
        xb = x_ref[rows, :]
        g = jnp.dot(xb, wg, preferred_element_type=F32)
        u = jnp.dot(xb, wu, preferred_element_type=F32)
        hid = (g * jax.nn.sigmoid(g) * u).astype(BF16)
        acc[rows, :] += jnp.dot(hid, wd, preferred_element_type=F32)

    @pl.when(f == nf - 1)
    def _():
        y_ref[...] = acc[...].astype(y_ref.dtype)


def _ffn_call(xe, w_gate, w_up, w_down, cap):
    n_groups = xe.shape[0]
    fc = FFN_FC
    nf = EXPERT_FF // fc
    body = functools.partial(_ffn_body, cap=cap)
    sq = pl.Squeezed()
    return pl.pallas_call(
        body,
        grid=(n_groups, N_EXPERTS, nf),
        in_specs=[pl.BlockSpec((sq, sq, cap, D_MODEL), lambda b, e, f: (b, e, 0, 0)),
                  pl.BlockSpec((sq, D_MODEL, fc), lambda b, e, f: (e, 0, f)),
                  pl.BlockSpec((sq, D_MODEL, fc), lambda b, e, f: (e, 0, f)),
                  pl.BlockSpec((sq, fc, D_MODEL), lambda b, e, f: (e, f, 0))],
        out_specs=pl.BlockSpec((sq, sq, cap, D_MODEL), lambda b, e, f: (b, e, 0, 0)),
        out_shape=jax.ShapeDtypeStruct((n_groups, N_EXPERTS, cap, D_MODEL), BF16),
        scratch_shapes=[pltpu.VMEM((cap, D_MODEL), F32)],
        compiler_params=_cparams(("arbitrary", "arbitrary", "arbitrary")),
        name="ffn",
    )(xe, w_gate, w_up, w_down)


def _combine_body(astart_ref, nrounds_ref, x1_ref, aff_ref, slot_ref, y_hbm, outa_ref, outb_ref,
                  ybuf, sem, yext, sem_ext, accs, *, tm, cap, tiles_per_group, n_first):
    i = pl.program_id(0)
    nt = pl.num_programs(0)
    w = COMB_W

    def window_copy(tile, e, k, dst, dsem):
        a0 = astart_ref[tile * N_EXPERTS + e]
        a = pl.multiple_of(jnp.minimum(a0 + k * w, cap - w), 16)
        return pltpu.make_async_copy(y_hbm.at[tile // tiles_per_group, e, pl.ds(a, w), :], dst, dsem)

    def first_copy(tile, e):
        sl = tile % 2
        return window_copy(tile, e, 0, ybuf.at[sl, pl.ds(e * w, w)], sem.at[sl, e])

    @pl.when(i == 0)
    def _():
        for e in range(N_EXPERTS):
            first_copy(i, e).start()

    @pl.when(i + 1 < nt)
    def _():
        for e in range(N_EXPERTS):
            first_copy(i + 1, e).start()

    lane = lax.broadcasted_iota(I32, (1, LANES), 1)
    lo_lane = lane < w
    slot_t = slot_ref[...]
    gate = aff_ref[...]
    g_hi = gate.astype(BF16).astype(F32)
    g_lo = gate - g_hi

    def expand(k, ywin):
        his, los = [], []
        for ep in range(N_EXPERTS // 2):
            rel, skip = [], []
            for e in (2 * ep, 2 * ep + 1):
                a0 = astart_ref[i * N_EXPERTS + e]
                lo = a0 + k * w
                a = jnp.minimum(lo, cap - w)
                rel.append(slot_t[:, e:e + 1] - a)
                skip.append(lo - a)
            fresh = lane >= jnp.where(lo_lane, skip[0], skip[1] + w)
            match = jnp.logical_and(jnp.where(lo_lane, rel[0], rel[1] + w) == lane, fresh)
            his.append(jnp.where(match, jnp.where(lo_lane, g_hi[:, 2 * ep:2 * ep + 1],
                                                  g_hi[:, 2 * ep + 1:2 * ep + 2]), 0.0).astype(BF16))
            los.append(jnp.where(match, jnp.where(lo_lane, g_lo[:, 2 * ep:2 * ep + 1],
                                                  g_lo[:, 2 * ep + 1:2 * ep + 2]), 0.0).astype(BF16))
        return (jnp.dot(jnp.concatenate(his, axis=1), ywin, preferred_element_type=F32)
                + jnp.dot(jnp.concatenate(los, axis=1), ywin, preferred_element_type=F32))

    for e in range(N_EXPERTS):
        first_copy(i, e).wait()
    accs[...] = x1_ref[...] + expand(0, ybuf[i % 2])

    def extra_round(k, carry):
        for e in range(N_EXPERTS):
            window_copy(i, e, k, yext.at[pl.ds(e * w, w)], sem_ext.at[e]).start()
        for e in range(N_EXPERTS):
            window_copy(i, e, k, yext.at[pl.ds(e * w, w)], sem_ext.at[e]).wait()
        accs[...] += expand(k, yext[...])
        return carry

    lax.fori_loop(1, nrounds_ref[i], extra_round, 0)

    @pl.when(i < n_first)
    def _():
        outa_ref[...] = accs[...]

    @pl.when(i >= n_first)
    def _():
        outb_ref[...] = accs[...]


def _combine_call(astart, nrounds, x1, aff, slot_tm, y, n_groups, cap, na):
    n = x1.shape[0]
    tm = PROJ_TM
    nt = n // tm
    n_first = na // tm
    tiles_per_group = nt // n_groups
    body = functools.partial(_combine_body, tm=tm, cap=cap, tiles_per_group=tiles_per_group, n_first=n_first)
    outa_spec, outb_spec = _two_array_specs(tm, n_first, D_MODEL, prefetch=True)
    grid_spec = pltpu.PrefetchScalarGridSpec(
        num_scalar_prefetch=2,
        grid=(nt,),
        in_specs=[pl.BlockSpec((tm, D_MODEL), lambda i, *_: (i, 0)),
                  pl.BlockSpec((tm, LANES), lambda i, *_: (i, 0)),
                  pl.BlockSpec((tm, LANES), lambda i, *_: (i, 0)),
                  pl.BlockSpec(memory_space=pl.ANY)],
        out_specs=[outa_spec, outb_spec],
        scratch_shapes=[pltpu.VMEM((2, N_EXPERTS * COMB_W, D_MODEL), BF16),
                        pltpu.SemaphoreType.DMA((2, N_EXPERTS)),
                        pltpu.VMEM((N_EXPERTS * COMB_W, D_MODEL), BF16),
                        pltpu.SemaphoreType.DMA((N_EXPERTS,)),
                        pltpu.VMEM((tm, D_MODEL), F32)],
    )
    return pl.pallas_call(
        body,
        grid_spec=grid_spec,
        out_shape=[jax.ShapeDtypeStruct((na, D_MODEL), F32), jax.ShapeDtypeStruct((n - na, D_MODEL), F32)],
        compiler_params=_cparams(("arbitrary",)),
        name="combine",
    )(astart, nrounds, x1, aff, slot_tm, y)


def _layer(xa, xb, seqs, n_groups, mix_norm_g, w_in, q_norm_g, k_norm_g, conv_w, w_attn_branch, w_conv_branch,
           w_out, ffn_norm_g, w_router, w_gate, w_up, w_down):
    na = xa.shape[0]
    n = na + xb.shape[0]
    group_rows = n // n_groups
    cap = CAPACITY_FACTOR * group_rows // N_EXPERTS
    tm = PROJ_TM

    qg = jnp.tile(q_norm_g, ATT_W // HEAD_DIM)[None, :]
    kg = jnp.tile(k_norm_g, ATT_W // HEAD_DIM)[None, :]
    outs = _proj_call(xa, xb, seqs, mix_norm_g[None, :], w_in.astype(BF16), qg, kg, conv_w,
                      w_conv_branch.astype(BF16))
    sga, gcb = outs[9], outs[10]
    o = []
    for gi, r in enumerate(DILATIONS):
        q, k, v = outs[3 * gi:3 * gi + 3]
        if r == 1:
            q, k, v = q[None], k[None], v[None]
        o.append(_attn_call(q, k, v, seqs, gi))

    wr = jnp.pad(w_router, ((0, 0), (0, LANES - N_EXPERTS)))
    wrh = wr.astype(BF16)
    wrl = (wr - wrh.astype(F32)).astype(BF16)
    x1, hn2, aff, afft = _merge_call(o[0], o[1], o[2], sga, gcb, xa, xb, w_attn_branch.astype(BF16),
                                     w_out.astype(BF16), ffn_norm_g[None, :], wrh, wrl)

    slot, offs, slot_tm = _route_call(afft, n_groups, cap)

    cpt = tm // LANES
    tiles_per_group = group_rows // tm
    starts = offs[::cpt, :, 0].reshape(n_groups, tiles_per_group, N_EXPERTS)
    ends = jnp.concatenate([starts[:, 1:], jnp.full((n_groups, 1, N_EXPERTS), cap, I32)], axis=1)
    xe = _dispatch_call(starts.reshape(-1), ends.reshape(-1), hn2, slot, n_groups, cap)
    y = _ffn_call(xe, w_gate, w_up, w_down, cap)

    astart = jnp.minimum((starts // 16) * 16, cap - COMB_W)
    nrounds = jnp.maximum(jnp.max((ends - astart + COMB_W - 1) // COMB_W, axis=-1), 1)
    return _combine_call(astart.reshape(-1), nrounds.reshape(-1), x1, aff, slot_tm, y, n_groups, cap, na)


def _run(x_prompt, x_sample, params):
    bp, sp, d = x_prompt.shape
    bs, ss, _ = x_sample.shape
    seqs, pos = [], 0
    for b, s in ((bp, sp), (bs, ss)):
        for _ in range(b):
            seqs.append((pos, s))
            pos += s
    assert bp * sp == bs * ss, "request groups are routed with one capacity"
    xa, xb = x_prompt.reshape(bp * sp, d), x_sample.reshape(bs * ss, d)
    depth = params[0].shape[0]
    for layer in range(depth):
        xa, xb = _layer(xa, xb, tuple(seqs), 2, *[p[layer] for p in params])
    return xa.reshape(bp, sp, d), xb.reshape(bs, ss, d)


def kernel(x_prompt, x_sample, mix_norm_g, w_in, q_norm_g, k_norm_g, conv_w, w_attn_branch, w_conv_branch, w_out,
           ffn_norm_g, w_router, w_gate, w_up, w_down):
    params = (mix_norm_g, w_in, q_norm_g, k_norm_g, conv_w, w_attn_branch, w_conv_branch, w_out,
              ffn_norm_g, w_router, w_gate, w_up, w_down)
    return _run(x_prompt, x_sample, params)
```

```python
import functools

import numpy as np
import jax
import jax.numpy as jnp
from jax import lax
from jax.experimental import pallas as pl
from jax.experimental.pallas import tpu as pltpu

F32 = jnp.float32
BF16 = jnp.bfloat16
I32 = jnp.int32

D_MODEL = 1024
HEAD_DIM = 64
N_SLOTS = 8
DILATIONS = (1, 4, 16)
KEYS_PER_SIDE = 64
GROUP_W = N_SLOTS * HEAD_DIM
ATT_W = len(DILATIONS) * GROUP_W
CONV_W = D_MODEL
IN_W = 3 * ATT_W + 3 * CONV_W + 2 * D_MODEL
N_EXPERTS = 16
EXPERT_FF = 2048
CAPACITY_FACTOR = 2
RMS_EPS = 1e-6
NEG = -1e30

LANES = 128
PAIR_SLABS = GROUP_W // LANES
VMEM_LIMIT = 58 * 1024 * 1024

PROJ_TM = 256
MERGE_TM = 256
PROJ_HALO = 16
ATT_TB_MAX = 2048
ATT_TQ = 128
FFN_FC = 512
FFN_RB = 512
DISP_SB = 2048
DISP_EG = 4
DISP_W = 96
COMB_W = 64
COMB_W_LOG2 = 6


def _cparams(sem):
    return pltpu.CompilerParams(dimension_semantics=sem, vmem_limit_bytes=VMEM_LIMIT)


def _any_eq(i, values):
    return functools.reduce(jnp.logical_or, [i == v for v in values])


def _const_spec(shape):
    nd = len(shape)
    return pl.BlockSpec(shape, lambda *a: (0,) * nd, pipeline_mode=pl.Buffered(1))


def _two_array_specs(block_rows, n_first_blocks, cols, prefetch=False):
    if prefetch:
        first = pl.BlockSpec((block_rows, cols), lambda i, *_: (jnp.minimum(i, n_first_blocks - 1), 0))
        second = pl.BlockSpec((block_rows, cols), lambda i, *_: (jnp.maximum(i - n_first_blocks, 0), 0))
    else:
        first = pl.BlockSpec((block_rows, cols), lambda i: (jnp.minimum(i, n_first_blocks - 1), 0))
        second = pl.BlockSpec((block_rows, cols), lambda i: (jnp.maximum(i - n_first_blocks, 0), 0))
    return first, second


def _proj_body(xa_ref, xb_ref, xpa_ref, xpb_ref, xna_ref, xnb_ref, ng_ref, w_ref, qg_ref, kg_ref, cw_ref, wcb_ref,
               q0_ref, k0_ref, v0_ref, q1_ref, k1_ref, v1_ref, q2_ref, k2_ref, v2_ref, sga_ref, gcb_ref, hs, hp,
               *, tm, halo, n_first, first_tiles, last_tiles):
    i = pl.program_id(0)
    is_first = _any_eq(i, first_tiles)
    is_last = _any_eq(i, last_tiles)
    in_a = i < n_first

    def pick(a_ref, b_ref):
        return jnp.where(in_a, a_ref[...], b_ref[...])

    x_ext = jnp.concatenate([pick(xpa_ref, xpb_ref), pick(xa_ref, xb_ref), pick(xna_ref, xnb_ref)], axis=0)
    ms = jnp.mean(x_ext * x_ext, axis=-1, keepdims=True)
    hn_f32 = x_ext * lax.rsqrt(ms + RMS_EPS) * ng_ref[...]
    hn_ext = hn_f32.astype(BF16)
    hn = hn_ext[halo:halo + tm]

    for sd in range(D_MODEL // LANES):
        hs[sd] = hn_f32[halo:halo + tm, sd * LANES:(sd + 1) * LANES]
    hn_of_group = [hn]
    for gi, r in enumerate(DILATIONS):
        if r == 1:
            continue
        for c in range(r):
            for sd in range(D_MODEL // LANES):
                hp[gi - 1, c * (tm // r):(c + 1) * (tm // r), sd * LANES:(sd + 1) * LANES] = (
                    hs[sd, pl.ds(c, tm // r, stride=r), :].astype(BF16))
        hn_of_group.append(hp[gi - 1])

    def proj(lhs, a, b):
        return jnp.dot(lhs, w_ref[:, a:b], preferred_element_type=F32)

    lo_lane = lax.broadcasted_iota(I32, (1, LANES), 1) < HEAD_DIM

    def head_norm(p, g_ref, col0, scale):
        out = []
        for s in range(PAIR_SLABS):
            xs = p[:, s * LANES:(s + 1) * LANES]
            x2 = xs * xs
            tot = jnp.sum(x2, axis=-1, keepdims=True)
            lo = jnp.sum(jnp.where(lo_lane, x2, 0.0), axis=-1, keepdims=True)
            hi = tot - lo
            r = jnp.where(lo_lane, lax.rsqrt(lo * (1.0 / HEAD_DIM) + RMS_EPS),
                          lax.rsqrt(hi * (1.0 / HEAD_DIM) + RMS_EPS))
            y = xs * r * g_ref[:, col0 + s * LANES:col0 + (s + 1) * LANES]
            out.append(y * scale if scale != 1.0 else y)
        return out

    def emit(gi, out_ref, slabs):
        r = DILATIONS[gi]
        for s in range(PAIR_SLABS):
            y = slabs[s].astype(out_ref.dtype)
            if r == 1:
                out_ref[:, s * LANES:(s + 1) * LANES] = y
            else:
                for c in range(r):
                    out_ref[c, :, s * LANES:(s + 1) * LANES] = y[c * (tm // r):(c + 1) * (tm // r)]

    q_refs, k_refs, v_refs = (q0_ref, q1_ref, q2_ref), (k0_ref, k1_ref, k2_ref), (v0_ref, v1_ref, v2_ref)
    for gi in range(len(DILATIONS)):
        c0 = gi * GROUP_W
        lhs = hn_of_group[gi]
        emit(gi, q_refs[gi], head_norm(proj(lhs, c0, c0 + GROUP_W), qg_ref, c0, HEAD_DIM ** -0.5))
        emit(gi, k_refs[gi], head_norm(proj(lhs, ATT_W + c0, ATT_W + c0 + GROUP_W), kg_ref, c0, 1.0))
        pv = proj(lhs, 2 * ATT_W + c0, 2 * ATT_W + c0 + GROUP_W)
        emit(gi, v_refs[gi], [pv[:, s * LANES:(s + 1) * LANES] for s in range(PAIR_SLABS)])

    c0 = 3 * ATT_W
    cb = proj(hn, c0, c0 + CONV_W)
    cc = proj(hn_ext, c0 + CONV_W, c0 + 2 * CONV_W)
    cx = proj(hn_ext, c0 + 2 * CONV_W, c0 + 3 * CONV_W)
    u_ext = cc * cx
    row = lax.broadcasted_iota(I32, (tm + 2 * halo, 1), 0)
    keep = jnp.logical_and(jnp.logical_or(row >= halo, jnp.logical_not(is_first)),
                           jnp.logical_or(row < halo + tm, jnp.logical_not(is_last)))
    u_ext = jnp.where(keep, u_ext, 0.0)
    cy = (u_ext[halo - 1:halo - 1 + tm] * cw_ref[0:1, :]
          + u_ext[halo:halo + tm] * cw_ref[1:2, :]
          + u_ext[halo + 1:halo + 1 + tm] * cw_ref[2:3, :])
    conv_in = (cb * cy).astype(BF16)
    cbr = jnp.dot(conv_in, wcb_ref[...], preferred_element_type=F32)
    g0 = c0 + 3 * CONV_W
    sga_ref[...] = jax.nn.sigmoid(proj(hn, g0, g0 + D_MODEL))
    gcb_ref[...] = jax.nn.sigmoid(proj(hn, g0 + D_MODEL, g0 + 2 * D_MODEL)) * cbr


def _proj_call(xa, xb, seqs, ng, w_in, qg, kg, cw, wcb):
    na, nb_rows = xa.shape[0], xb.shape[0]
    n = na + nb_rows
    tm, halo = PROJ_TM, PROJ_HALO
    nt = n // tm
    n_first = na // tm
    hb = tm // halo
    first_tiles = tuple(s // tm for s, _ in seqs)
    last_tiles = tuple((s + l) // tm - 1 for s, l in seqs)
    nha, nhb = na // halo, nb_rows // halo
    body = functools.partial(_proj_body, tm=tm, halo=halo, n_first=n_first,
                             first_tiles=first_tiles, last_tiles=last_tiles)
    cur_a, cur_b = _two_array_specs(tm, n_first, D_MODEL)
    prev_a = pl.BlockSpec((halo, D_MODEL), lambda i: (jnp.clip(i * hb - 1, 0, nha - 1), 0))
    prev_b = pl.BlockSpec((halo, D_MODEL), lambda i: (jnp.clip((i - n_first) * hb - 1, 0, nhb - 1), 0))
    next_a = pl.BlockSpec((halo, D_MODEL), lambda i: (jnp.clip((i + 1) * hb, 0, nha - 1), 0))
    next_b = pl.BlockSpec((halo, D_MODEL), lambda i: (jnp.clip((i + 1 - n_first) * hb, 0, nhb - 1), 0))
    out_specs, out_shape = [], []
    for gi, r in enumerate(DILATIONS):
        for _ in range(3):
            if r == 1:
                out_specs.append(pl.BlockSpec((tm, GROUP_W), lambda i: (i, 0)))
                out_shape.append(jax.ShapeDtypeStruct((n, GROUP_W), BF16))
            else:
                out_specs.append(pl.BlockSpec((r, tm // r, GROUP_W), lambda i: (0, i, 0)))
                out_shape.append(jax.ShapeDtypeStruct((r, n // r, GROUP_W), BF16))
    out_specs += [pl.BlockSpec((tm, D_MODEL), lambda i: (i, 0))] * 2
    out_shape += [jax.ShapeDtypeStruct((n, D_MODEL), F32)] * 2
    return pl.pallas_call(
        body,
        grid=(nt,),
        in_specs=[cur_a, cur_b, prev_a, prev_b, next_a, next_b,
                  _const_spec((1, D_MODEL)), _const_spec((D_MODEL, IN_W)), _const_spec((1, ATT_W)),
                  _const_spec((1, ATT_W)), _const_spec((3, CONV_W)), _const_spec((CONV_W, D_MODEL))],
        out_specs=out_specs,
        out_shape=out_shape,
        scratch_shapes=[pltpu.VMEM((D_MODEL // LANES, tm, LANES), F32),
                        pltpu.VMEM((len(DILATIONS) - 1, tm, D_MODEL), BF16)],
        compiler_params=_cparams(("arbitrary",)),
        name="proj",
    )(xa, xb, xa, xb, xa, xb, ng, w_in, qg, kg, cw, wcb)


def _attn_body(q_ref, kp_ref, kc_ref, kn_ref, vp_ref, vc_ref, vn_ref, bias_ref, o_ref, kw, vw,
               *, tb, first_tiles, last_tiles):
    i = pl.program_id(1)
    is_first = _any_eq(i, first_tiles)
    is_last = _any_eq(i, last_tiles)
    h = KEYS_PER_SIDE
    tq = ATT_TQ
    kwin = tq + 2 * h
    nq = tb // tq

    for s in range(PAIR_SLABS):
        sl = slice(s * LANES, (s + 1) * LANES)
        for dst, p_ref, c_ref, n_ref in ((kw, kp_ref, kc_ref, kn_ref), (vw, vp_ref, vc_ref, vn_ref)):
            dst[s, 0:h, :] = p_ref[:, sl]
            dst[s, h:h + tb, :] = c_ref[:, sl]
            dst[s, h + tb:2 * h + tb, :] = n_ref[:, sl]

    lane = lax.broadcasted_iota(I32, (1, LANES), 1)
    lo_lane = lane < HEAD_DIM
    col = lax.broadcasted_iota(I32, (1, kwin), 1)

    def head_pair(q2, k2, v2, biases):
        o_acc = None
        lse2 = None
        for half in range(2):
            mask = lo_lane if half == 0 else jnp.logical_not(lo_lane)
            qm = jnp.where(mask, q2, jnp.zeros_like(q2))
            sc = lax.dot_general(qm, k2, (((1,), (1,)), ((), ())), preferred_element_type=F32)
            sc = sc + biases[half]
            m = jnp.max(sc, axis=-1, keepdims=True)
            p = jnp.exp(sc - m)
            l = jnp.sum(p, axis=-1, keepdims=True)
            vm = jnp.where(mask, v2, jnp.zeros_like(v2))
            pv = jnp.dot(p.astype(BF16), vm, preferred_element_type=F32)
            contrib = pv * (1.0 / l)
            lse = m + jnp.log(l)
            o_acc = contrib if o_acc is None else o_acc + contrib
            lse2 = jnp.broadcast_to(lse, (tq, LANES)) if lse2 is None else jnp.where(mask, lse, lse2)
        return o_acc, lse2

    def step(qb, carry):
        pen_lo = jnp.where(jnp.logical_and(is_first, qb == 0), NEG, 0.0)
        pen_hi = jnp.where(jnp.logical_and(is_last, qb == nq - 1), NEG, 0.0)
        edge = (jnp.where(col < h, pen_lo, 0.0) + jnp.where(col >= h + tq, pen_hi, 0.0))
        q0 = pl.multiple_of(qb * tq, tq)
        for s in range(PAIR_SLABS):
            biases = [bias_ref[2 * s] + edge, bias_ref[2 * s + 1] + edge]
            o2, lse2 = head_pair(q_ref[pl.ds(q0, tq), s * LANES:(s + 1) * LANES],
                                 kw[s, pl.ds(q0, kwin), :], vw[s, pl.ds(q0, kwin), :], biases)
            o_ref[s, pl.ds(q0, tq), :] = o2
            o_ref[PAIR_SLABS + s, pl.ds(q0, tq), :] = lse2
        return carry

    lax.fori_loop(0, nq, step, 0)


def _alibi_bias(gi, r):
    n_heads = len(DILATIONS) * N_SLOTS
    slopes = (2.0 ** (-8.0 * (np.arange(n_heads, dtype=np.float32) + 1.0) / n_heads)).astype(np.float32)
    slopes = slopes.reshape(len(DILATIONS), N_SLOTS)[gi]
    qi = np.arange(ATT_TQ)[:, None]
    kj = np.arange(ATT_TQ + 2 * KEYS_PER_SIDE)[None, :]
    off = kj - KEYS_PER_SIDE - qi
    dist = (r * np.abs(off)).astype(np.float32)
    bias = -slopes[:, None, None] * dist[None]
    bias = np.where((np.abs(off) <= KEYS_PER_SIDE)[None], bias, np.float32(NEG))
    return jnp.asarray(bias, F32)


def _attn_call(q, k, v, seqs, gi):
    r, rows, _ = q.shape
    h = KEYS_PER_SIDE
    tb = min([ATT_TB_MAX] + [l // r for _, l in seqs])
    assert all((s // r) % tb == 0 and (l // r) % tb == 0 for s, l in seqs)
    nt = rows // tb
    hb = tb // h
    nhb = rows // h
    first_tiles = tuple((s // r) // tb for s, _ in seqs)
    last_tiles = tuple(((s + l) // r) // tb - 1 for s, l in seqs)
    kwin = ATT_TQ + 2 * h
    body = functools.partial(_attn_body, tb=tb, first_tiles=first_tiles, last_tiles=last_tiles)
    sq = pl.Squeezed()
    cur = pl.BlockSpec((sq, tb, GROUP_W), lambda c, i: (c, i, 0))
    prev = pl.BlockSpec((sq, h, GROUP_W), lambda c, i: (c, jnp.maximum(i * hb - 1, 0), 0))
    nxt = pl.BlockSpec((sq, h, GROUP_W), lambda c, i: (c, jnp.minimum((i + 1) * hb, nhb - 1), 0))
    return pl.pallas_call(
        body,
        grid=(r, nt),
        in_specs=[cur, prev, cur, nxt, prev, cur, nxt, _const_spec((N_SLOTS, ATT_TQ, kwin))],
        out_specs=pl.BlockSpec((2 * PAIR_SLABS, sq, tb, LANES), lambda c, i: (0, c, i, 0)),
        out_shape=jax.ShapeDtypeStruct((2 * PAIR_SLABS, r, rows, LANES), F32),
        scratch_shapes=[pltpu.VMEM((PAIR_SLABS, tb + 2 * h, LANES), BF16),
                        pltpu.VMEM((PAIR_SLABS, tb + 2 * h, LANES), BF16)],
        compiler_params=_cparams(("arbitrary", "arbitrary")),
        name=f"attn_d{r}",
    )(q, k, k, k, v, v, v, _alibi_bias(gi, r))


def _split_bf16(x):
    hi = x.astype(BF16)
    lo = (x - hi.astype(F32)).astype(BF16)
    return hi, lo


def _merge_body(o0_ref, o1_ref, o2_ref, sga_ref, gcb_ref, xa_ref, xb_ref, wab_ref, wout_ref, g2_ref,
                wrh_ref, wrl_ref, x1_ref, hn_ref, aff_ref, afft_ref, t1, t2, *, tm, n_first):
    i = pl.program_id(0)
    for o_ref, scr, r in ((o1_ref, t1, DILATIONS[1]), (o2_ref, t2, DILATIONS[2])):
        for s in range(2 * PAIR_SLABS):
            for c in range(r):
                scr[s, pl.ds(c, tm // r, stride=r), :] = o_ref[s, c]
    slabs = []
    for s in range(PAIR_SLABS):
        l0, l1, l2 = o0_ref[PAIR_SLABS + s, 0], t1[PAIR_SLABS + s], t2[PAIR_SLABS + s]
        mx = jnp.maximum(jnp.maximum(l0, l1), l2)
        e0, e1, e2 = jnp.exp(l0 - mx), jnp.exp(l1 - mx), jnp.exp(l2 - mx)
        num = e0 * o0_ref[s, 0] + e1 * t1[s] + e2 * t2[s]
        slabs.append(num / (e0 + e1 + e2))
    att = jnp.concatenate(slabs, axis=-1).astype(BF16)
    a_br = jnp.dot(att, wab_ref[...], preferred_element_type=F32)
    merged = (sga_ref[...] * a_br + gcb_ref[...]).astype(BF16)
    x = jnp.where(i < n_first, xa_ref[...], xb_ref[...])
    x1 = x + jnp.dot(merged, wout_ref[...], preferred_element_type=F32)
    x1_ref[...] = x1
    ms = jnp.mean(x1 * x1, axis=-1, keepdims=True)
    hn2 = x1 * lax.rsqrt(ms + RMS_EPS) * g2_ref[...]
    hn_ref[...] = hn2.astype(BF16)
    hh, hl = _split_bf16(hn2)
    logits = (jnp.dot(hh, wrh_ref[...], preferred_element_type=F32)
              + jnp.dot(hl, wrh_ref[...], preferred_element_type=F32)
              + jnp.dot(hh, wrl_ref[...], preferred_element_type=F32))
    lane = lax.broadcasted_iota(I32, (1, LANES), 1)
    logits = jnp.where(lane < N_EXPERTS, logits, NEG)
    e = jnp.exp(logits - jnp.max(logits, axis=-1, keepdims=True))
    aff = e / jnp.sum(e, axis=-1, keepdims=True)
    aff_ref[...] = aff
    aff_t = aff.T
    for c in range(tm // LANES):
        afft_ref[c] = aff_t[0:N_EXPERTS, c * LANES:(c + 1) * LANES]


def _merge_call(o0, o1, o2, sga, gcb, xa, xb, wab, wout, g2, wrh, wrl):
    n = sga.shape[0]
    tm = MERGE_TM
    nt = n // tm
    n_first = xa.shape[0] // tm
    body = functools.partial(_merge_body, tm=tm, n_first=n_first)
    ns = 2 * PAIR_SLABS
    o_specs = [pl.BlockSpec((ns, r, tm // r, LANES), lambda i: (0, 0, i, 0)) for r in DILATIONS]
    rowspec = pl.BlockSpec((tm, D_MODEL), lambda i: (i, 0))
    xa_spec, xb_spec = _two_array_specs(tm, n_first, D_MODEL)
    return pl.pallas_call(
        body,
        grid=(nt,),
        in_specs=o_specs + [rowspec, rowspec, xa_spec, xb_spec,
                            _const_spec((GROUP_W, D_MODEL)), _const_spec((D_MODEL, D_MODEL)),
                            _const_spec((1, D_MODEL)), _const_spec((D_MODEL, LANES)),
                            _const_spec((D_MODEL, LANES))],
        out_specs=[rowspec, rowspec,
                   pl.BlockSpec((tm, LANES), lambda i: (i, 0)),
                   pl.BlockSpec((tm // LANES, N_EXPERTS, LANES), lambda i: (i, 0, 0))],
        out_shape=[jax.ShapeDtypeStruct((n, D_MODEL), F32),
                   jax.ShapeDtypeStruct((n, D_MODEL), BF16),
                   jax.ShapeDtypeStruct((n, LANES), F32),
                   jax.ShapeDtypeStruct((n // LANES, N_EXPERTS, LANES), F32)],
        scratch_shapes=[pltpu.VMEM((ns, tm, LANES), F32), pltpu.VMEM((ns, tm, LANES), F32)],
        compiler_params=_cparams(("arbitrary",)),
        name="merge",
    )(o0, o1, o2, sga, gcb, xa, xb, wab, wout, g2, wrh, wrl)


def _route_body(aff_ref, tri_ref, ones_ref, low_ref, slot_ref, offs_ref, slot_tm_ref, rs_scr, off_scr,
                *, nch, cap):
    a = aff_ref[...]

    def count(mask):
        c = jnp.sum(mask.astype(F32), axis=0, keepdims=True)
        return jnp.sum(c, axis=2, keepdims=True)

    cur = jnp.zeros((1, N_EXPERTS, 1), I32)
    for bit in range(30, -1, -1):
        cand = cur | (1 << bit)
        cur = jnp.where(count(a >= lax.bitcast_convert_type(cand, F32)) >= cap, cand, cur)
    tau = lax.bitcast_convert_type(cur, F32)
    gt = a > tau
    eq = a == tau
    need = cap - count(gt)

    def prefix(mask):
        m = mask.astype(BF16).reshape(nch * N_EXPERTS, LANES)
        incl = jnp.dot(m, tri_ref[...], preferred_element_type=F32).reshape(nch, N_EXPERTS, LANES)
        rs_scr[...] = jnp.dot(m, ones_ref[...], preferred_element_type=F32).reshape(nch, N_EXPERTS, LANES)
        for e in range(N_EXPERTS):
            off_scr[:, e, :] = jnp.dot(low_ref[...], rs_scr[:, e, :].astype(BF16), preferred_element_type=F32)
        off = off_scr[...]
        return incl + off - mask.astype(F32), off

    tie_rank, _ = prefix(eq)
    sel = jnp.logical_or(gt, jnp.logical_and(eq, tie_rank < need))
    pos, off = prefix(sel)
    slot = jnp.where(sel, pos, -1.0)
    slot_ref[...] = slot.astype(I32)
    offs_ref[...] = off.astype(I32)
    rs_scr[...] = slot
    pad = jnp.full((LANES - N_EXPERTS, LANES), -1.0, F32)

    def tr(c, carry):
        tile = jnp.concatenate([rs_scr[c], pad], axis=0)
        slot_tm_ref[pl.ds(pl.multiple_of(c * LANES, LANES), LANES), :] = tile.T.astype(I32)
        return carry

    lax.fori_loop(0, nch, tr, 0)


def _route_call(afft, n_groups, cap):
    nchunks = afft.shape[0]
    nch = nchunks // n_groups
    tri = jnp.asarray(np.triu(np.ones((LANES, LANES), np.float32)), BF16)
    ones = jnp.ones((LANES, LANES), BF16)
    low = jnp.asarray(np.tril(np.ones((nch, nch), np.float32), -1), BF16)
    body = functools.partial(_route_body, nch=nch, cap=cap)
    blk = pl.BlockSpec((nch, N_EXPERTS, LANES), lambda b: (b, 0, 0))
    return pl.pallas_call(
        body,
        grid=(n_groups,),
        in_specs=[blk, _const_spec((LANES, LANES)), _const_spec((LANES, LANES)), _const_spec((nch, nch))],
        out_specs=[blk, blk, pl.BlockSpec((nch * LANES, LANES), lambda b: (b, 0))],
        out_shape=[jax.ShapeDtypeStruct((nchunks, N_EXPERTS, LANES), I32),
                   jax.ShapeDtypeStruct((nchunks, N_EXPERTS, LANES), I32),
                   jax.ShapeDtypeStruct((nchunks * LANES, LANES), I32)],
        scratch_shapes=[pltpu.VMEM((nch, N_EXPERTS, LANES), F32), pltpu.VMEM((nch, N_EXPERTS, LANES), F32)],
        compiler_params=_cparams(("arbitrary",)),
        name="route",
    )(afft, tri, ones, low)


def _dispatch_body(starts_ref, ends_ref, hn_ref, slot_ref, x_ref, *, tm, sb, tiles_per_group):
    b, eg, j = pl.program_id(0), pl.program_id(1), pl.program_id(2)
    w = DISP_W

    @pl.when(j == 0)
    def _():
        x_ref[...] = jnp.zeros_like(x_ref)

    iota_w = lax.broadcasted_iota(I32, (w, 1), 0)
    cpt = tm // LANES
    for t in range(sb // tm):
        tile = j * (sb // tm) + t
        hn_t = hn_ref[t * tm:(t + 1) * tm, :]
        base, srows, ends = [], [], []
        for ee in range(DISP_EG):
            e = eg * DISP_EG + ee
            idx = (b * tiles_per_group + tile) * N_EXPERTS + e
            base.append(pl.multiple_of((starts_ref[idx] // 16) * 16, 16))
            ends.append(ends_ref[idx])
            srows.append(jnp.concatenate([slot_ref[t * cpt + c, pl.ds(e, 1), :] for c in range(cpt)], axis=1))
        oh = jnp.concatenate([(srows[ee] - base[ee] == iota_w).astype(BF16) for ee in range(DISP_EG)], axis=0)
        res = jnp.dot(oh, hn_t, preferred_element_type=F32)
        for ee in range(DISP_EG):
            x_ref[ee, pl.ds(base[ee], w), :] += res[ee * w:(ee + 1) * w].astype(x_ref.dtype)

            def more(k, carry, ee=ee):
                bk = pl.multiple_of(base[ee] + k * w, 16)
                ohk = (srows[ee] - bk == iota_w).astype(BF16)
                x_ref[ee, pl.ds(bk, w), :] += jnp.dot(ohk, hn_t, preferred_element_type=F32).astype(x_ref.dtype)
                return carry

            lax.fori_loop(1, (ends[ee] - base[ee] + w - 1) // w, more, 0)


def _dispatch_call(starts, ends, hn, slot, n_groups, cap):
    n = hn.shape[0]
    ng_rows = n // n_groups
    tm, sb = PROJ_TM, DISP_SB
    nsb = ng_rows // sb
    tiles_per_group = ng_rows // tm
    cps = sb // LANES
    body = functools.partial(_dispatch_body, tm=tm, sb=sb, tiles_per_group=tiles_per_group)
    sq = pl.Squeezed()
    grid_spec = pltpu.PrefetchScalarGridSpec(
        num_scalar_prefetch=2,
        grid=(n_groups, N_EXPERTS // DISP_EG, nsb),
        in_specs=[pl.BlockSpec((sb, D_MODEL), lambda b, g, j, *_: (b * nsb + j, 0)),
                  pl.BlockSpec((cps, N_EXPERTS, LANES), lambda b, g, j, *_: (b * nsb + j, 0, 0))],
        out_specs=pl.BlockSpec((sq, DISP_EG, cap + DISP_W, D_MODEL), lambda b, g, j, *_: (b, g, 0, 0)),
    )
    return pl.pallas_call(
        body,
        grid_spec=grid_spec,
        out_shape=jax.ShapeDtypeStruct((n_groups, N_EXPERTS, cap + DISP_W, D_MODEL), BF16),
        compiler_params=_cparams(("arbitrary", "arbitrary", "arbitrary")),
        name="dispatch",
    )(starts, ends, hn, slot)


def _ffn_body(x_ref, wg_ref, wu_ref, wd_ref, y_ref, acc, *, cap):
    f = pl.program_id(2)
    nf = pl.num_programs(2)

    @pl.when(f == 0)
    def _():
        acc[...] = jnp.zeros_like(acc)

    wg = wg_ref[...].astype(BF16)
    wu = wu_ref[...].astype(BF16)
    wd = wd_ref[...].astype(BF16)
    for rb in range(cap // FFN_RB):
        rows = slice(rb * FFN_RB, (rb + 1) * FFN_RB)
        xb = x_ref[rows, :]
        g = jnp.dot(xb, wg, preferred_element_type=F32)
        u = jnp.dot(xb, wu, preferred_element_type=F32)
        hid = (g * jax.nn.sigmoid(g) * u).astype(BF16)
        acc[rows, :] += jnp.dot(hid, wd, preferred_element_type=F32)

    @pl.when(f == nf - 1)
    def _():
        y_ref[...] = acc[...].astype(y_ref.dtype)


def _ffn_call(xe, w_gate, w_up, w_down, cap):
    n_groups = xe.shape[0]
    fc = FFN_FC
    nf = EXPERT_FF // fc
    body = functools.partial(_ffn_body, cap=cap)
    sq = pl.Squeezed()
    return pl.pallas_call(
        body,
        grid=(n_groups, N_EXPERTS, nf),
        in_specs=[pl.BlockSpec((sq, sq, cap, D_MODEL), lambda b, e, f: (b, e, 0, 0)),
                  pl.BlockSpec((sq, D_MODEL, fc), lambda b, e, f: (e, 0, f)),
                  pl.BlockSpec((sq, D_MODEL, fc), lambda b, e, f: (e, 0, f)),
                  pl.BlockSpec((sq, fc, D_MODEL), lambda b, e, f: (e, f, 0))],
        out_specs=pl.BlockSpec((sq, sq, cap, D_MODEL), lambda b, e, f: (b, e, 0, 0)),
        out_shape=jax.ShapeDtypeStruct((n_groups, N_EXPERTS, cap, D_MODEL), BF16),
        scratch_shapes=[pltpu.VMEM((cap, D_MODEL), F32)],
        compiler_params=_cparams(("arbitrary", "arbitrary", "arbitrary")),
        name="ffn",
    )(xe, w_gate, w_up, w_down)


def _combine_body(astart_ref, nrounds_ref, x1_ref, aff_ref, slot_ref, spread_ref, y_hbm, outa_ref, outb_ref,
                  ybuf, sem, yext, sem_ext, accs, *, tm, cap, tiles_per_group, n_first):
    i = pl.program_id(0)
    nt = pl.num_programs(0)
    w = COMB_W

    def window_copy(tile, e, k, dst, dsem):
        a0 = astart_ref[tile * N_EXPERTS + e]
        a = pl.multiple_of(jnp.minimum(a0 + k * w, cap - w), 16)
        return pltpu.make_async_copy(y_hbm.at[tile // tiles_per_group, e, pl.ds(a, w), :], dst, dsem)

    def first_copy(tile, e):
        sl = tile % 2
        return window_copy(tile, e, 0, ybuf.at[sl, pl.ds(e * w, w)], sem.at[sl, e])

    @pl.when(i == 0)
    def _():
        for e in range(N_EXPERTS):
            first_copy(i, e).start()

    @pl.when(i + 1 < nt)
    def _():
        for e in range(N_EXPERTS):
            first_copy(i + 1, e).start()

    def spread(cols):
        return jnp.dot(cols.astype(BF16), spread_ref[...], preferred_element_type=F32)

    slot_t = slot_ref[...]
    enc = jnp.where(slot_t < 0, w * w - 1, slot_t)
    s_hi = spread((enc >> COMB_W_LOG2).astype(F32))
    s_lo = spread((enc & (w - 1)).astype(F32))
    gate = aff_ref[...]
    g_hi = gate.astype(BF16).astype(F32)
    g_hi_s = spread(g_hi)
    g_lo_s = spread(gate - g_hi)
    wide = lax.broadcasted_iota(I32, (1, N_EXPERTS * w), 1)
    owner = wide >> COMB_W_LOG2
    local = wide & (w - 1)

    def expand(k, ywin):
        row0 = jnp.zeros_like(wide)
        skip = jnp.zeros_like(wide)
        for e in range(N_EXPERTS):
            lo = astart_ref[i * N_EXPERTS + e] + k * w
            a = jnp.minimum(lo, cap - w)
            row0 = jnp.where(owner == e, a, row0)
            skip = jnp.where(owner == e, lo - a, skip)
        want = row0 + local
        want_hi = jnp.where(local >= skip, want >> COMB_W_LOG2, -1).astype(F32)
        want_lo = (want & (w - 1)).astype(F32)
        match = jnp.logical_and(s_hi == want_hi, s_lo == want_lo)
        return (jnp.dot(jnp.where(match, g_hi_s, 0.0).astype(BF16), ywin, preferred_element_type=F32)
                + jnp.dot(jnp.where(match, g_lo_s, 0.0).astype(BF16), ywin, preferred_element_type=F32))

    for e in range(N_EXPERTS):
        first_copy(i, e).wait()
    accs[...] = x1_ref[...] + expand(0, ybuf[i % 2])

    def extra_round(k, carry):
        for e in range(N_EXPERTS):
            window_copy(i, e, k, yext.at[pl.ds(e * w, w)], sem_ext.at[e]).start()
        for e in range(N_EXPERTS):
            window_copy(i, e, k, yext.at[pl.ds(e * w, w)], sem_ext.at[e]).wait()
        accs[...] += expand(k, yext[...])
        return carry

    lax.fori_loop(1, nrounds_ref[i], extra_round, 0)

    @pl.when(i < n_first)
    def _():
        outa_ref[...] = accs[...]

    @pl.when(i >= n_first)
    def _():
        outb_ref[...] = accs[...]


def _combine_call(astart, nrounds, x1, aff, slot_tm, y, n_groups, cap, na):
    n = x1.shape[0]
    tm = PROJ_TM
    nt = n // tm
    n_first = na // tm
    tiles_per_group = nt // n_groups
    body = functools.partial(_combine_body, tm=tm, cap=cap, tiles_per_group=tiles_per_group, n_first=n_first)
    outa_spec, outb_spec = _two_array_specs(tm, n_first, D_MODEL, prefetch=True)
    assert (1 << COMB_W_LOG2) == COMB_W and (cap - 1) // COMB_W < COMB_W - 1
    spread = np.zeros((LANES, N_EXPERTS * COMB_W), np.float32)
    for e in range(N_EXPERTS):
        spread[e, e * COMB_W:(e + 1) * COMB_W] = 1.0
    spread = jnp.asarray(spread, BF16)
    grid_spec = pltpu.PrefetchScalarGridSpec(
        num_scalar_prefetch=2,
        grid=(nt,),
        in_specs=[pl.BlockSpec((tm, D_MODEL), lambda i, *_: (i, 0)),
                  pl.BlockSpec((tm, LANES), lambda i, *_: (i, 0)),
                  pl.BlockSpec((tm, LANES), lambda i, *_: (i, 0)),
                  pl.BlockSpec((LANES, N_EXPERTS * COMB_W), lambda i, *_: (0, 0), pipeline_mode=pl.Buffered(1)),
                  pl.BlockSpec(memory_space=pl.ANY)],
        out_specs=[outa_spec, outb_spec],
        scratch_shapes=[pltpu.VMEM((2, N_EXPERTS * COMB_W, D_MODEL), BF16),
                        pltpu.SemaphoreType.DMA((2, N_EXPERTS)),
                        pltpu.VMEM((N_EXPERTS * COMB_W, D_MODEL), BF16),
                        pltpu.SemaphoreType.DMA((N_EXPERTS,)),
                        pltpu.VMEM((tm, D_MODEL), F32)],
    )
    return pl.pallas_call(
        body,
        grid_spec=grid_spec,
        out_shape=[jax.ShapeDtypeStruct((na, D_MODEL), F32), jax.ShapeDtypeStruct((n - na, D_MODEL), F32)],
        compiler_params=_cparams(("arbitrary",)),
        name="combine",
    )(astart, nrounds, x1, aff, slot_tm, spread, y)


def _layer(xa, xb, seqs, n_groups, mix_norm_g, w_in, q_norm_g, k_norm_g, conv_w, w_attn_branch, w_conv_branch,
           w_out, ffn_norm_g, w_router, w_gate, w_up, w_down):
    na = xa.shape[0]
    n = na + xb.shape[0]
    group_rows = n // n_groups
    cap = CAPACITY_FACTOR * group_rows // N_EXPERTS
    tm = PROJ_TM

    qg = jnp.tile(q_norm_g, ATT_W // HEAD_DIM)[None, :]
    kg = jnp.tile(k_norm_g, ATT_W // HEAD_DIM)[None, :]
    outs = _proj_call(xa, xb, seqs, mix_norm_g[None, :], w_in.astype(BF16), qg, kg, conv_w,
                      w_conv_branch.astype(BF16))
    sga, gcb = outs[9], outs[10]
    o = []
    for gi, r in enumerate(DILATIONS):
        q, k, v = outs[3 * gi:3 * gi + 3]
        if r == 1:
            q, k, v = q[None], k[None], v[None]
        o.append(_attn_call(q, k, v, seqs, gi))

    wr = jnp.pad(w_router, ((0, 0), (0, LANES - N_EXPERTS)))
    wrh = wr.astype(BF16)
    wrl = (wr - wrh.astype(F32)).astype(BF16)
    x1, hn2, aff, afft = _merge_call(o[0], o[1], o[2], sga, gcb, xa, xb, w_attn_branch.astype(BF16),
                                     w_out.astype(BF16), ffn_norm_g[None, :], wrh, wrl)

    slot, offs, slot_tm = _route_call(afft, n_groups, cap)

    cpt = tm // LANES
    tiles_per_group = group_rows // tm
    starts = offs[::cpt, :, 0].reshape(n_groups, tiles_per_group, N_EXPERTS)
    ends = jnp.concatenate([starts[:, 1:], jnp.full((n_groups, 1, N_EXPERTS), cap, I32)], axis=1)
    xe = _dispatch_call(starts.reshape(-1), ends.reshape(-1), hn2, slot, n_groups, cap)
    y = _ffn_call(xe, w_gate, w_up, w_down, cap)

    astart = jnp.minimum((starts // 16) * 16, cap - COMB_W)
    nrounds = jnp.maximum(jnp.max((ends - astart + COMB_W - 1) // COMB_W, axis=-1), 1)
    return _combine_call(astart.reshape(-1), nrounds.reshape(-1), x1, aff, slot_tm, y, n_groups, cap, na)


def _run(x_prompt, x_sample, params):
    bp, sp, d = x_prompt.shape
    bs, ss, _ = x_sample.shape
    seqs, pos = [], 0
    for b, s in ((bp, sp), (bs, ss)):
        for _ in range(b):
            seqs.append((pos, s))
            pos += s
    assert bp * sp == bs * ss, "request groups are routed with one capacity"
    xa, xb = x_prompt.reshape(bp * sp, d), x_sample.reshape(bs * ss, d)
    depth = params[0].shape[0]
    for layer in range(depth):
        xa, xb = _layer(xa, xb, tuple(seqs), 2, *[p[layer] for p in params])
    return xa.reshape(bp, sp, d), xb.reshape(bs, ss, d)


def kernel(x_prompt, x_sample, mix_norm_g, w_in, q_norm_g, k_norm_g, conv_w, w_attn_branch, w_conv_branch, w_out,
           ffn_norm_g, w_router, w_gate, w_up, w_down):
    params = (mix_norm_g, w_in, q_norm_g, k_norm_g, conv_w, w_attn_branch, w_conv_branch, w_out,
              ffn_norm_g, w_router, w_gate, w_up, w_down)
    return _run(x_prompt, x_sample, params)
```

```python
import functools

import numpy as np
import jax
import jax.numpy as jnp
from jax import lax
from jax.experimental import pallas as pl
from jax.experimental.pallas import tpu as pltpu

F32 = jnp.float32
BF16 = jnp.bfloat16
I32 = jnp.int32

D_MODEL = 1024
HEAD_DIM = 64
N_SLOTS = 8
DILATIONS = (1, 4, 16)
KEYS_PER_SIDE = 64
GROUP_W = N_SLOTS * HEAD_DIM
ATT_W = len(DILATIONS) * GROUP_W
CONV_W = D_MODEL
IN_W = 3 * ATT_W + 3 * CONV_W + 2 * D_MODEL
N_EXPERTS = 16
EXPERT_FF = 2048
CAPACITY_FACTOR = 2
RMS_EPS = 1e-6
NEG = -1e30

LANES = 128
PAIR_SLABS = GROUP_W // LANES
VMEM_LIMIT = 58 * 1024 * 1024

PROJ_TM = 256
MERGE_TM = 256
PROJ_HALO = 16
ATT_TB_MAX = 2048
ATT_TQ = 128
FFN_FC = 512
FFN_RB = 512
DISP_SB = 2048
DISP_EG = 4
DISP_W = 96
COMB_W = 64
COMB_W_LOG2 = 6


def _cparams(sem):
    return pltpu.CompilerParams(dimension_semantics=sem, vmem_limit_bytes=VMEM_LIMIT)


def _any_eq(i, values):
    return functools.reduce(jnp.logical_or, [i == v for v in values])


def _const_spec(shape):
    nd = len(shape)
    return pl.BlockSpec(shape, lambda *a: (0,) * nd, pipeline_mode=pl.Buffered(1))


def _two_array_specs(block_rows, n_first_blocks, cols, prefetch=False):
    if prefetch:
        first = pl.BlockSpec((block_rows, cols), lambda i, *_: (jnp.minimum(i, n_first_blocks - 1), 0))
        second = pl.BlockSpec((block_rows, cols), lambda i, *_: (jnp.maximum(i - n_first_blocks, 0), 0))
    else:
        first = pl.BlockSpec((block_rows, cols), lambda i: (jnp.minimum(i, n_first_blocks - 1), 0))
        second = pl.BlockSpec((block_rows, cols), lambda i: (jnp.maximum(i - n_first_blocks, 0), 0))
    return first, second


def _proj_body(xa_ref, xb_ref, xpa_ref, xpb_ref, xna_ref, xnb_ref, ng_ref, w_ref, qg_ref, kg_ref, cw_ref, wcb_ref,
               q0_ref, k0_ref, v0_ref, q1_ref, k1_ref, v1_ref, q2_ref, k2_ref, v2_ref, sga_ref, gcb_ref, hs, hp,
               *, tm, halo, n_first, first_tiles, last_tiles):
    i = pl.program_id(0)
    is_first = _any_eq(i, first_tiles)
    is_last = _any_eq(i, last_tiles)
    in_a = i < n_first

    def pick(a_ref, b_ref):
        return jnp.where(in_a, a_ref[...], b_ref[...])

    x_ext = jnp.concatenate([pick(xpa_ref, xpb_ref), pick(xa_ref, xb_ref), pick(xna_ref, xnb_ref)], axis=0)
    ms = jnp.mean(x_ext * x_ext, axis=-1, keepdims=True)
    hn_f32 = x_ext * lax.rsqrt(ms + RMS_EPS) * ng_ref[...]
    hn_ext = hn_f32.astype(BF16)
    hn = hn_ext[halo:halo + tm]

    for sd in range(D_MODEL // LANES):
        hs[sd] = hn_f32[halo:halo + tm, sd * LANES:(sd + 1) * LANES]
    hn_of_group = [hn]
    for gi, r in enumerate(DILATIONS):
        if r == 1:
            continue
        for c in range(r):
            for sd in range(D_MODEL // LANES):
                hp[gi - 1, c * (tm // r):(c + 1) * (tm // r), sd * LANES:(sd + 1) * LANES] = (
                    hs[sd, pl.ds(c, tm // r, stride=r), :].astype(BF16))
        hn_of_group.append(hp[gi - 1])

    def proj(lhs, a, b):
        return jnp.dot(lhs, w_ref[:, a:b], preferred_element_type=F32)

    lo_lane = lax.broadcasted_iota(I32, (1, LANES), 1) < HEAD_DIM

    def head_norm(p, g_ref, col0, scale):
        out = []
        for s in range(PAIR_SLABS):
            xs = p[:, s * LANES:(s + 1) * LANES]
            x2 = xs * xs
            tot = jnp.sum(x2, axis=-1, keepdims=True)
            lo = jnp.sum(jnp.where(lo_lane, x2, 0.0), axis=-1, keepdims=True)
            hi = tot - lo
            r = jnp.where(lo_lane, lax.rsqrt(lo * (1.0 / HEAD_DIM) + RMS_EPS),
                          lax.rsqrt(hi * (1.0 / HEAD_DIM) + RMS_EPS))
            y = xs * r * g_ref[:, col0 + s * LANES:col0 + (s + 1) * LANES]
            out.append(y * scale if scale != 1.0 else y)
        return out

    def emit(gi, out_ref, slabs):
        r = DILATIONS[gi]
        for s in range(PAIR_SLABS):
            y = slabs[s].astype(out_ref.dtype)
            if r == 1:
                out_ref[:, s * LANES:(s + 1) * LANES] = y
            else:
                for c in range(r):
                    out_ref[c, :, s * LANES:(s + 1) * LANES] = y[c * (tm // r):(c + 1) * (tm // r)]

    q_refs, k_refs, v_refs = (q0_ref, q1_ref, q2_ref), (k0_ref, k1_ref, k2_ref), (v0_ref, v1_ref, v2_ref)
    for gi in range(len(DILATIONS)):
        c0 = gi * GROUP_W
        lhs = hn_of_group[gi]
        emit(gi, q_refs[gi], head_norm(proj(lhs, c0, c0 + GROUP_W), qg_ref, c0, HEAD_DIM ** -0.5))
        emit(gi, k_refs[gi], head_norm(proj(lhs, ATT_W + c0, ATT_W + c0 + GROUP_W), kg_ref, c0, 1.0))
        pv = proj(lhs, 2 * ATT_W + c0, 2 * ATT_W + c0 + GROUP_W)
        emit(gi, v_refs[gi], [pv[:, s * LANES:(s + 1) * LANES] for s in range(PAIR_SLABS)])

    c0 = 3 * ATT_W
    cb = proj(hn, c0, c0 + CONV_W)
    cc = proj(hn_ext, c0 + CONV_W, c0 + 2 * CONV_W)
    cx = proj(hn_ext, c0 + 2 * CONV_W, c0 + 3 * CONV_W)
    u_ext = cc * cx
    row = lax.broadcasted_iota(I32, (tm + 2 * halo, 1), 0)
    keep = jnp.logical_and(jnp.logical_or(row >= halo, jnp.logical_not(is_first)),
                           jnp.logical_or(row < halo + tm, jnp.logical_not(is_last)))
    u_ext = jnp.where(keep, u_ext, 0.0)
    cy = (u_ext[halo - 1:halo - 1 + tm] * cw_ref[0:1, :]
          + u_ext[halo:halo + tm] * cw_ref[1:2, :]
          + u_ext[halo + 1:halo + 1 + tm] * cw_ref[2:3, :])
    conv_in = (cb * cy).astype(BF16)
    cbr = jnp.dot(conv_in, wcb_ref[...], preferred_element_type=F32)
    g0 = c0 + 3 * CONV_W
    sga_ref[...] = jax.nn.sigmoid(proj(hn, g0, g0 + D_MODEL))
    gcb_ref[...] = jax.nn.sigmoid(proj(hn, g0 + D_MODEL, g0 + 2 * D_MODEL)) * cbr


def _proj_call(xa, xb, seqs, ng, w_in, qg, kg, cw, wcb):
    na, nb_rows = xa.shape[0], xb.shape[0]
    n = na + nb_rows
    tm, halo = PROJ_TM, PROJ_HALO
    nt = n // tm
    n_first = na // tm
    hb = tm // halo
    first_tiles = tuple(s // tm for s, _ in seqs)
    last_tiles = tuple((s + l) // tm - 1 for s, l in seqs)
    nha, nhb = na // halo, nb_rows // halo
    body = functools.partial(_proj_body, tm=tm, halo=halo, n_first=n_first,
                             first_tiles=first_tiles, last_tiles=last_tiles)
    cur_a, cur_b = _two_array_specs(tm, n_first, D_MODEL)
    prev_a = pl.BlockSpec((halo, D_MODEL), lambda i: (jnp.clip(i * hb - 1, 0, nha - 1), 0))
    prev_b = pl.BlockSpec((halo, D_MODEL), lambda i: (jnp.clip((i - n_first) * hb - 1, 0, nhb - 1), 0))
    next_a = pl.BlockSpec((halo, D_MODEL), lambda i: (jnp.clip((i + 1) * hb, 0, nha - 1), 0))
    next_b = pl.BlockSpec((halo, D_MODEL), lambda i: (jnp.clip((i + 1 - n_first) * hb, 0, nhb - 1), 0))
    out_specs, out_shape = [], []
    for gi, r in enumerate(DILATIONS):
        for _ in range(3):
            if r == 1:
                out_specs.append(pl.BlockSpec((tm, GROUP_W), lambda i: (i, 0)))
                out_shape.append(jax.ShapeDtypeStruct((n, GROUP_W), BF16))
            else:
                out_specs.append(pl.BlockSpec((r, tm // r, GROUP_W), lambda i: (0, i, 0)))
                out_shape.append(jax.ShapeDtypeStruct((r, n // r, GROUP_W), BF16))
    out_specs += [pl.BlockSpec((tm, D_MODEL), lambda i: (i, 0))] * 2
    out_shape += [jax.ShapeDtypeStruct((n, D_MODEL), F32)] * 2
    return pl.pallas_call(
        body,
        grid=(nt,),
        in_specs=[cur_a, cur_b, prev_a, prev_b, next_a, next_b,
                  _const_spec((1, D_MODEL)), _const_spec((D_MODEL, IN_W)), _const_spec((1, ATT_W)),
                  _const_spec((1, ATT_W)), _const_spec((3, CONV_W)), _const_spec((CONV_W, D_MODEL))],
        out_specs=out_specs,
        out_shape=out_shape,
        scratch_shapes=[pltpu.VMEM((D_MODEL // LANES, tm, LANES), F32),
                        pltpu.VMEM((len(DILATIONS) - 1, tm, D_MODEL), BF16)],
        compiler_params=_cparams(("arbitrary",)),
        name="proj",
    )(xa, xb, xa, xb, xa, xb, ng, w_in, qg, kg, cw, wcb)


def _attn_body(q_ref, kp_ref, kc_ref, kn_ref, vp_ref, vc_ref, vn_ref, bias_ref, o_ref, kw, vw,
               *, tb, first_tiles, last_tiles):
    i = pl.program_id(1)
    is_first = _any_eq(i, first_tiles)
    is_last = _any_eq(i, last_tiles)
    h = KEYS_PER_SIDE
    tq = ATT_TQ
    kwin = tq + 2 * h
    nq = tb // tq

    for s in range(PAIR_SLABS):
        sl = slice(s * LANES, (s + 1) * LANES)
        for dst, p_ref, c_ref, n_ref in ((kw, kp_ref, kc_ref, kn_ref), (vw, vp_ref, vc_ref, vn_ref)):
            dst[s, 0:h, :] = p_ref[:, sl]
            dst[s, h:h + tb, :] = c_ref[:, sl]
            dst[s, h + tb:2 * h + tb, :] = n_ref[:, sl]

    lane = lax.broadcasted_iota(I32, (1, LANES), 1)
    lo_lane = lane < HEAD_DIM
    col = lax.broadcasted_iota(I32, (1, kwin), 1)

    def head_pair(q2, k2, v2, biases):
        o_acc = None
        lse2 = None
        for half in range(2):
            mask = lo_lane if half == 0 else jnp.logical_not(lo_lane)
            qm = jnp.where(mask, q2, jnp.zeros_like(q2))
            sc = lax.dot_general(qm, k2, (((1,), (1,)), ((), ())), preferred_element_type=F32)
            sc = sc + biases[half]
            m = jnp.max(sc, axis=-1, keepdims=True)
            p = jnp.exp(sc - m)
            l = jnp.sum(p, axis=-1, keepdims=True)
            vm = jnp.where(mask, v2, jnp.zeros_like(v2))
            pv = jnp.dot(p.astype(BF16), vm, preferred_element_type=F32)
            contrib = pv * (1.0 / l)
            lse = m + jnp.log(l)
            o_acc = contrib if o_acc is None else o_acc + contrib
            lse2 = jnp.broadcast_to(lse, (tq, LANES)) if lse2 is None else jnp.where(mask, lse, lse2)
        return o_acc, lse2

    def step(qb, carry):
        pen_lo = jnp.where(jnp.logical_and(is_first, qb == 0), NEG, 0.0)
        pen_hi = jnp.where(jnp.logical_and(is_last, qb == nq - 1), NEG, 0.0)
        edge = (jnp.where(col < h, pen_lo, 0.0) + jnp.where(col >= h + tq, pen_hi, 0.0))
        q0 = pl.multiple_of(qb * tq, tq)
        for s in range(PAIR_SLABS):
            biases = [bias_ref[2 * s] + edge, bias_ref[2 * s + 1] + edge]
            o2, lse2 = head_pair(q_ref[pl.ds(q0, tq), s * LANES:(s + 1) * LANES],
                                 kw[s, pl.ds(q0, kwin), :], vw[s, pl.ds(q0, kwin), :], biases)
            o_ref[s, pl.ds(q0, tq), :] = o2
            o_ref[PAIR_SLABS + s, pl.ds(q0, tq), :] = lse2
        return carry

    lax.fori_loop(0, nq, step, 0)


def _alibi_bias(gi, r):
    n_heads = len(DILATIONS) * N_SLOTS
    slopes = (2.0 ** (-8.0 * (np.arange(n_heads, dtype=np.float32) + 1.0) / n_heads)).astype(np.float32)
    slopes = slopes.reshape(len(DILATIONS), N_SLOTS)[gi]
    qi = np.arange(ATT_TQ)[:, None]
    kj = np.arange(ATT_TQ + 2 * KEYS_PER_SIDE)[None, :]
    off = kj - KEYS_PER_SIDE - qi
    dist = (r * np.abs(off)).astype(np.float32)
    bias = -slopes[:, None, None] * dist[None]
    bias = np.where((np.abs(off) <= KEYS_PER_SIDE)[None], bias, np.float32(NEG))
    return jnp.asarray(bias, F32)


def _attn_call(q, k, v, seqs, gi):
    r, rows, _ = q.shape
    h = KEYS_PER_SIDE
    tb = min([ATT_TB_MAX] + [l // r for _, l in seqs])
    assert all((s // r) % tb == 0 and (l // r) % tb == 0 for s, l in seqs)
    nt = rows // tb
    hb = tb // h
    nhb = rows // h
    first_tiles = tuple((s // r) // tb for s, _ in seqs)
    last_tiles = tuple(((s + l) // r) // tb - 1 for s, l in seqs)
    kwin = ATT_TQ + 2 * h
    body = functools.partial(_attn_body, tb=tb, first_tiles=first_tiles, last_tiles=last_tiles)
    sq = pl.Squeezed()
    cur = pl.BlockSpec((sq, tb, GROUP_W), lambda c, i: (c, i, 0))
    prev = pl.BlockSpec((sq, h, GROUP_W), lambda c, i: (c, jnp.maximum(i * hb - 1, 0), 0))
    nxt = pl.BlockSpec((sq, h, GROUP_W), lambda c, i: (c, jnp.minimum((i + 1) * hb, nhb - 1), 0))
    return pl.pallas_call(
        body,
        grid=(r, nt),
        in_specs=[cur, prev, cur, nxt, prev, cur, nxt, _const_spec((N_SLOTS, ATT_TQ, kwin))],
        out_specs=pl.BlockSpec((2 * PAIR_SLABS, sq, tb, LANES), lambda c, i: (0, c, i, 0)),
        out_shape=jax.ShapeDtypeStruct((2 * PAIR_SLABS, r, rows, LANES), F32),
        scratch_shapes=[pltpu.VMEM((PAIR_SLABS, tb + 2 * h, LANES), BF16),
                        pltpu.VMEM((PAIR_SLABS, tb + 2 * h, LANES), BF16)],
        compiler_params=_cparams(("arbitrary", "arbitrary")),
        name=f"attn_d{r}",
    )(q, k, k, k, v, v, v, _alibi_bias(gi, r))


def _split_bf16(x):
    hi = x.astype(BF16)
    lo = (x - hi.astype(F32)).astype(BF16)
    return hi, lo


def _merge_body(o0_ref, o1_ref, o2_ref, sga_ref, gcb_ref, xa_ref, xb_ref, wab_ref, wout_ref, g2_ref,
                wrh_ref, wrl_ref, x1_ref, hn_ref, aff_ref, afft_ref, t1, t2, *, tm, n_first):
    i = pl.program_id(0)
    for o_ref, scr, r in ((o1_ref, t1, DILATIONS[1]), (o2_ref, t2, DILATIONS[2])):
        for s in range(2 * PAIR_SLABS):
            for c in range(r):
                scr[s, pl.ds(c, tm // r, stride=r), :] = o_ref[s, c]
    slabs = []
    for s in range(PAIR_SLABS):
        l0, l1, l2 = o0_ref[PAIR_SLABS + s, 0], t1[PAIR_SLABS + s], t2[PAIR_SLABS + s]
        mx = jnp.maximum(jnp.maximum(l0, l1), l2)
        e0, e1, e2 = jnp.exp(l0 - mx), jnp.exp(l1 - mx), jnp.exp(l2 - mx)
        num = e0 * o0_ref[s, 0] + e1 * t1[s] + e2 * t2[s]
        slabs.append(num / (e0 + e1 + e2))
    att = jnp.concatenate(slabs, axis=-1).astype(BF16)
    a_br = jnp.dot(att, wab_ref[...], preferred_element_type=F32)
    merged = (sga_ref[...] * a_br + gcb_ref[...]).astype(BF16)
    x = jnp.where(i < n_first, xa_ref[...], xb_ref[...])
    x1 = x + jnp.dot(merged, wout_ref[...], preferred_element_type=F32)
    x1_ref[...] = x1
    ms = jnp.mean(x1 * x1, axis=-1, keepdims=True)
    hn2 = x1 * lax.rsqrt(ms + RMS_EPS) * g2_ref[...]
    hn_ref[...] = hn2.astype(BF16)
    hh, hl = _split_bf16(hn2)
    logits = (jnp.dot(hh, wrh_ref[...], preferred_element_type=F32)
              + jnp.dot(hl, wrh_ref[...], preferred_element_type=F32)
              + jnp.dot(hh, wrl_ref[...], preferred_element_type=F32))
    lane = lax.broadcasted_iota(I32, (1, LANES), 1)
    logits = jnp.where(lane < N_EXPERTS, logits, NEG)
    e = jnp.exp(logits - jnp.max(logits, axis=-1, keepdims=True))
    aff = e / jnp.sum(e, axis=-1, keepdims=True)
    aff_ref[...] = aff
    aff_t = aff.T
    for c in range(tm // LANES):
        afft_ref[c] = aff_t[0:N_EXPERTS, c * LANES:(c + 1) * LANES]


def _merge_call(o0, o1, o2, sga, gcb, xa, xb, wab, wout, g2, wrh, wrl):
    n = sga.shape[0]
    tm = MERGE_TM
    nt = n // tm
    n_first = xa.shape[0] // tm
    body = functools.partial(_merge_body, tm=tm, n_first=n_first)
    ns = 2 * PAIR_SLABS
    o_specs = [pl.BlockSpec((ns, r, tm // r, LANES), lambda i: (0, 0, i, 0)) for r in DILATIONS]
    rowspec = pl.BlockSpec((tm, D_MODEL), lambda i: (i, 0))
    xa_spec, xb_spec = _two_array_specs(tm, n_first, D_MODEL)
    return pl.pallas_call(
        body,
        grid=(nt,),
        in_specs=o_specs + [rowspec, rowspec, xa_spec, xb_spec,
                            _const_spec((GROUP_W, D_MODEL)), _const_spec((D_MODEL, D_MODEL)),
                            _const_spec((1, D_MODEL)), _const_spec((D_MODEL, LANES)),
                            _const_spec((D_MODEL, LANES))],
        out_specs=[rowspec, rowspec,
                   pl.BlockSpec((tm, LANES), lambda i: (i, 0)),
                   pl.BlockSpec((tm // LANES, N_EXPERTS, LANES), lambda i: (i, 0, 0))],
        out_shape=[jax.ShapeDtypeStruct((n, D_MODEL), F32),
                   jax.ShapeDtypeStruct((n, D_MODEL), BF16),
                   jax.ShapeDtypeStruct((n, LANES), F32),
                   jax.ShapeDtypeStruct((n // LANES, N_EXPERTS, LANES), F32)],
        scratch_shapes=[pltpu.VMEM((ns, tm, LANES), F32), pltpu.VMEM((ns, tm, LANES), F32)],
        compiler_params=_cparams(("arbitrary",)),
        name="merge",
    )(o0, o1, o2, sga, gcb, xa, xb, wab, wout, g2, wrh, wrl)


def _route_body(aff_ref, tri_ref, ones_ref, low_ref, slot_ref, offs_ref, slot_tm_ref, rs_scr, off_scr,
                *, nch, cap):
    a = aff_ref[...]

    def count(mask):
        c = jnp.sum(mask.astype(F32), axis=0, keepdims=True)
        return jnp.sum(c, axis=2, keepdims=True)

    cur = jnp.zeros((1, N_EXPERTS, 1), I32)
    for bit in range(30, -1, -1):
        cand = cur | (1 << bit)
        cur = jnp.where(count(a >= lax.bitcast_convert_type(cand, F32)) >= cap, cand, cur)
    tau = lax.bitcast_convert_type(cur, F32)
    gt = a > tau
    eq = a == tau
    need = cap - count(gt)

    def prefix(mask):
        m = mask.astype(BF16).reshape(nch * N_EXPERTS, LANES)
        incl = jnp.dot(m, tri_ref[...], preferred_element_type=F32).reshape(nch, N_EXPERTS, LANES)
        rs_scr[...] = jnp.dot(m, ones_ref[...], preferred_element_type=F32).reshape(nch, N_EXPERTS, LANES)
        for e in range(N_EXPERTS):
            off_scr[:, e, :] = jnp.dot(low_ref[...], rs_scr[:, e, :].astype(BF16), preferred_element_type=F32)
        off = off_scr[...]
        return incl + off - mask.astype(F32), off

    tie_rank, _ = prefix(eq)
    sel = jnp.logical_or(gt, jnp.logical_and(eq, tie_rank < need))
    pos, off = prefix(sel)
    slot = jnp.where(sel, pos, -1.0)
    slot_ref[...] = slot.astype(I32)
    offs_ref[...] = off.astype(I32)
    rs_scr[...] = slot
    pad = jnp.full((LANES - N_EXPERTS, LANES), -1.0, F32)

    def tr(c, carry):
        tile = jnp.concatenate([rs_scr[c], pad], axis=0)
        slot_tm_ref[pl.ds(pl.multiple_of(c * LANES, LANES), LANES), :] = tile.T.astype(I32)
        return carry

    lax.fori_loop(0, nch, tr, 0)


def _route_call(afft, n_groups, cap):
    nchunks = afft.shape[0]
    nch = nchunks // n_groups
    tri = jnp.asarray(np.triu(np.ones((LANES, LANES), np.float32)), BF16)
    ones = jnp.ones((LANES, LANES), BF16)
    low = jnp.asarray(np.tril(np.ones((nch, nch), np.float32), -1), BF16)
    body = functools.partial(_route_body, nch=nch, cap=cap)
    blk = pl.BlockSpec((nch, N_EXPERTS, LANES), lambda b: (b, 0, 0))
    return pl.pallas_call(
        body,
        grid=(n_groups,),
        in_specs=[blk, _const_spec((LANES, LANES)), _const_spec((LANES, LANES)), _const_spec((nch, nch))],
        out_specs=[blk, blk, pl.BlockSpec((nch * LANES, LANES), lambda b: (b, 0))],
        out_shape=[jax.ShapeDtypeStruct((nchunks, N_EXPERTS, LANES), I32),
                   jax.ShapeDtypeStruct((nchunks, N_EXPERTS, LANES), I32),
                   jax.ShapeDtypeStruct((nchunks * LANES, LANES), I32)],
        scratch_shapes=[pltpu.VMEM((nch, N_EXPERTS, LANES), F32), pltpu.VMEM((nch, N_EXPERTS, LANES), F32)],
        compiler_params=_cparams(("arbitrary",)),
        name="route",
    )(afft, tri, ones, low)


def _dispatch_body(starts_ref, ends_ref, dense_ref, hn_ref, slot_ref, x_ref, *, tm, sb, tiles_per_group):
    b, eg, j = pl.program_id(0), pl.program_id(1), pl.program_id(2)
    w = DISP_W

    @pl.when(j == 0)
    def _():
        x_ref[...] = jnp.zeros_like(x_ref)

    iota_w = lax.broadcasted_iota(I32, (w, 1), 0)
    cpt = tm // LANES

    def window(t, ee):
        e = eg * DISP_EG + ee
        idx = (b * tiles_per_group + j * (sb // tm) + t) * N_EXPERTS + e
        base = pl.multiple_of((starts_ref[idx] // 16) * 16, 16)
        srow = jnp.concatenate([slot_ref[t * cpt + c, pl.ds(e, 1), :] for c in range(cpt)], axis=1)
        return base, ends_ref[idx], srow

    for t in range(sb // tm):
        wins = [window(t, ee) for ee in range(DISP_EG)]
        oh = jnp.concatenate([(srow - base == iota_w).astype(BF16) for base, _, srow in wins], axis=0)
        res = jnp.dot(oh, hn_ref[t * tm:(t + 1) * tm, :], preferred_element_type=F32)
        for ee, (base, _, _) in enumerate(wins):
            x_ref[ee, pl.ds(base, w), :] += res[ee * w:(ee + 1) * w].astype(x_ref.dtype)

    @pl.when(dense_ref[(b * pl.num_programs(1) + eg) * pl.num_programs(2) + j] > 0)
    def _():
        def tile_body(t, carry):
            hn_t = hn_ref[pl.ds(pl.multiple_of(t * tm, tm), tm), :]
            for ee in range(DISP_EG):
                base, end, srow = window(t, ee)

                def more(k, c2, ee=ee, base=base, srow=srow):
                    bk = pl.multiple_of(base + k * w, 16)
                    ohk = (srow - bk == iota_w).astype(BF16)
                    x_ref[ee, pl.ds(bk, w), :] += jnp.dot(ohk, hn_t,
                                                          preferred_element_type=F32).astype(x_ref.dtype)
                    return c2

                lax.fori_loop(1, (end - base + w - 1) // w, more, 0)
            return carry

        lax.fori_loop(0, sb // tm, tile_body, 0)


def _dispatch_call(starts, ends, hn, slot, n_groups, cap):
    n = hn.shape[0]
    ng_rows = n // n_groups
    tm, sb = PROJ_TM, DISP_SB
    nsb = ng_rows // sb
    tiles_per_group = ng_rows // tm
    cps = sb // LANES
    n_eg = N_EXPERTS // DISP_EG
    rounds = (ends - (starts // 16) * 16 + DISP_W - 1) // DISP_W
    dense = jnp.max(rounds.reshape(n_groups, nsb, sb // tm, n_eg, DISP_EG), axis=(2, 4)) > 1
    dense = jnp.transpose(dense, (0, 2, 1)).astype(I32).reshape(-1)
    starts, ends = starts.reshape(-1), ends.reshape(-1)
    body = functools.partial(_dispatch_body, tm=tm, sb=sb, tiles_per_group=tiles_per_group)
    sq = pl.Squeezed()
    grid_spec = pltpu.PrefetchScalarGridSpec(
        num_scalar_prefetch=3,
        grid=(n_groups, N_EXPERTS // DISP_EG, nsb),
        in_specs=[pl.BlockSpec((sb, D_MODEL), lambda b, g, j, *_: (b * nsb + j, 0)),
                  pl.BlockSpec((cps, N_EXPERTS, LANES), lambda b, g, j, *_: (b * nsb + j, 0, 0))],
        out_specs=pl.BlockSpec((sq, DISP_EG, cap + DISP_W, D_MODEL), lambda b, g, j, *_: (b, g, 0, 0)),
    )
    return pl.pallas_call(
        body,
        grid_spec=grid_spec,
        out_shape=jax.ShapeDtypeStruct((n_groups, N_EXPERTS, cap + DISP_W, D_MODEL), BF16),
        compiler_params=_cparams(("arbitrary", "arbitrary", "arbitrary")),
        name="dispatch",
    )(starts, ends, dense, hn, slot)


def _ffn_body(x_ref, wg_ref, wu_ref, wd_ref, y_ref, acc, *, cap):
    f = pl.program_id(2)
    nf = pl.num_programs(2)

    @pl.when(f == 0)
    def _():
        acc[...] = jnp.zeros_like(acc)

    wg = wg_ref[...].astype(BF16)
    wu = wu_ref[...].astype(BF16)
    wd = wd_ref[...].astype(BF16)
    for rb in range(cap // FFN_RB):
        rows = slice(rb * FFN_RB, (rb + 1) * FFN_RB)
        xb = x_ref[rows, :]
        g = jnp.dot(xb, wg, preferred_element_type=F32)
        u = jnp.dot(xb, wu, preferred_element_type=F32)
        hid = (g * jax.nn.sigmoid(g) * u).astype(BF16)
        acc[rows, :] += jnp.dot(hid, wd, preferred_element_type=F32)

    @pl.when(f == nf - 1)
    def _():
        y_ref[...] = acc[...].astype(y_ref.dtype)


def _ffn_call(xe, w_gate, w_up, w_down, cap):
    n_groups = xe.shape[0]
    fc = FFN_FC
    nf = EXPERT_FF // fc
    body = functools.partial(_ffn_body, cap=cap)
    sq = pl.Squeezed()
    return pl.pallas_call(
        body,
        grid=(n_groups, N_EXPERTS, nf),
        in_specs=[pl.BlockSpec((sq, sq, cap, D_MODEL), lambda b, e, f: (b, e, 0, 0)),
                  pl.BlockSpec((sq, D_MODEL, fc), lambda b, e, f: (e, 0, f)),
                  pl.BlockSpec((sq, D_MODEL, fc), lambda b, e, f: (e, 0, f)),
                  pl.BlockSpec((sq, fc, D_MODEL), lambda b, e, f: (e, f, 0))],
        out_specs=pl.BlockSpec((sq, sq, cap, D_MODEL), lambda b, e, f: (b, e, 0, 0)),
        out_shape=jax.ShapeDtypeStruct((n_groups, N_EXPERTS, cap, D_MODEL), BF16),
        scratch_shapes=[pltpu.VMEM((cap, D_MODEL), F32)],
        compiler_params=_cparams(("arbitrary", "arbitrary", "arbitrary")),
        name="ffn",
    )(xe, w_gate, w_up, w_down)


def _combine_body(astart_ref, nrounds_ref, x1_ref, aff_ref, slot_ref, spread_ref, y_hbm, outa_ref, outb_ref,
                  ybuf, sem, yext, sem_ext, accs, *, tm, cap, tiles_per_group, n_first):
    i = pl.program_id(0)
    nt = pl.num_programs(0)
    w = COMB_W

    def window_copy(tile, e, k, dst, dsem):
        a0 = astart_ref[tile * N_EXPERTS + e]
        a = pl.multiple_of(jnp.minimum(a0 + k * w, cap - w), 16)
        return pltpu.make_async_copy(y_hbm.at[tile // tiles_per_group, e, pl.ds(a, w), :], dst, dsem)

    def first_copy(tile, e):
        sl = tile % 2
        return window_copy(tile, e, 0, ybuf.at[sl, pl.ds(e * w, w)], sem.at[sl, e])

    @pl.when(i == 0)
    def _():
        for e in range(N_EXPERTS):
            first_copy(i, e).start()

    @pl.when(i + 1 < nt)
    def _():
        for e in range(N_EXPERTS):
            first_copy(i + 1, e).start()

    def spread(cols):
        return jnp.dot(cols.astype(BF16), spread_ref[...], preferred_element_type=F32)

    slot_t = slot_ref[...]
    enc = jnp.where(slot_t < 0, w * w - 1, slot_t)
    s_hi = spread((enc >> COMB_W_LOG2).astype(F32))
    s_lo = spread((enc & (w - 1)).astype(F32))
    gate = aff_ref[...]
    g_hi = gate.astype(BF16).astype(F32)
    g_hi_s = spread(g_hi)
    g_lo_s = spread(gate - g_hi)
    wide = lax.broadcasted_iota(I32, (1, N_EXPERTS * w), 1)
    owner = wide >> COMB_W_LOG2
    local = wide & (w - 1)

    def expand(k, ywin):
        row0 = jnp.zeros_like(wide)
        skip = jnp.zeros_like(wide)
        for e in range(N_EXPERTS):
            lo = astart_ref[i * N_EXPERTS + e] + k * w
            a = jnp.minimum(lo, cap - w)
            row0 = jnp.where(owner == e, a, row0)
            skip = jnp.where(owner == e, lo - a, skip)
        want = row0 + local
        want_hi = jnp.where(local >= skip, want >> COMB_W_LOG2, -1).astype(F32)
        want_lo = (want & (w - 1)).astype(F32)
        match = jnp.logical_and(s_hi == want_hi, s_lo == want_lo)
        return (jnp.dot(jnp.where(match, g_hi_s, 0.0).astype(BF16), ywin, preferred_element_type=F32)
                + jnp.dot(jnp.where(match, g_lo_s, 0.0).astype(BF16), ywin, preferred_element_type=F32))

    for e in range(N_EXPERTS):
        first_copy(i, e).wait()
    accs[...] = x1_ref[...] + expand(0, ybuf[i % 2])

    def extra_round(k, carry):
        for e in range(N_EXPERTS):
            window_copy(i, e, k, yext.at[pl.ds(e * w, w)], sem_ext.at[e]).start()
        for e in range(N_EXPERTS):
            window_copy(i, e, k, yext.at[pl.ds(e * w, w)], sem_ext.at[e]).wait()
        accs[...] += expand(k, yext[...])
        return carry

    lax.fori_loop(1, nrounds_ref[i], extra_round, 0)

    @pl.when(i < n_first)
    def _():
        outa_ref[...] = accs[...]

    @pl.when(i >= n_first)
    def _():
        outb_ref[...] = accs[...]


def _combine_call(astart, nrounds, x1, aff, slot_tm, y, n_groups, cap, na):
    n = x1.shape[0]
    tm = PROJ_TM
    nt = n // tm
    n_first = na // tm
    tiles_per_group = nt // n_groups
    body = functools.partial(_combine_body, tm=tm, cap=cap, tiles_per_group=tiles_per_group, n_first=n_first)
    outa_spec, outb_spec = _two_array_specs(tm, n_first, D_MODEL, prefetch=True)
    assert (1 << COMB_W_LOG2) == COMB_W and (cap - 1) // COMB_W < COMB_W - 1
    spread = np.zeros((LANES, N_EXPERTS * COMB_W), np.float32)
    for e in range(N_EXPERTS):
        spread[e, e * COMB_W:(e + 1) * COMB_W] = 1.0
    spread = jnp.asarray(spread, BF16)
    grid_spec = pltpu.PrefetchScalarGridSpec(
        num_scalar_prefetch=2,
        grid=(nt,),
        in_specs=[pl.BlockSpec((tm, D_MODEL), lambda i, *_: (i, 0)),
                  pl.BlockSpec((tm, LANES), lambda i, *_: (i, 0)),
                  pl.BlockSpec((tm, LANES), lambda i, *_: (i, 0)),
                  pl.BlockSpec((LANES, N_EXPERTS * COMB_W), lambda i, *_: (0, 0), pipeline_mode=pl.Buffered(1)),
                  pl.BlockSpec(memory_space=pl.ANY)],
        out_specs=[outa_spec, outb_spec],
        scratch_shapes=[pltpu.VMEM((2, N_EXPERTS * COMB_W, D_MODEL), BF16),
                        pltpu.SemaphoreType.DMA((2, N_EXPERTS)),
                        pltpu.VMEM((N_EXPERTS * COMB_W, D_MODEL), BF16),
                        pltpu.SemaphoreType.DMA((N_EXPERTS,)),
                        pltpu.VMEM((tm, D_MODEL), F32)],
    )
    return pl.pallas_call(
        body,
        grid_spec=grid_spec,
        out_shape=[jax.ShapeDtypeStruct((na, D_MODEL), F32), jax.ShapeDtypeStruct((n - na, D_MODEL), F32)],
        compiler_params=_cparams(("arbitrary",)),
        name="combine",
    )(astart, nrounds, x1, aff, slot_tm, spread, y)


def _layer(xa, xb, seqs, n_groups, mix_norm_g, w_in, q_norm_g, k_norm_g, conv_w, w_attn_branch, w_conv_branch,
           w_out, ffn_norm_g, w_router, w_gate, w_up, w_down):
    na = xa.shape[0]
    n = na + xb.shape[0]
    group_rows = n // n_groups
    cap = CAPACITY_FACTOR * group_rows // N_EXPERTS
    tm = PROJ_TM

    qg = jnp.tile(q_norm_g, ATT_W // HEAD_DIM)[None, :]
    kg = jnp.tile(k_norm_g, ATT_W // HEAD_DIM)[None, :]
    outs = _proj_call(xa, xb, seqs, mix_norm_g[None, :], w_in.astype(BF16), qg, kg, conv_w,
                      w_conv_branch.astype(BF16))
    sga, gcb = outs[9], outs[10]
    o = []
    for gi, r in enumerate(DILATIONS):
        q, k, v = outs[3 * gi:3 * gi + 3]
        if r == 1:
            q, k, v = q[None], k[None], v[None]
        o.append(_attn_call(q, k, v, seqs, gi))

    wr = jnp.pad(w_router, ((0, 0), (0, LANES - N_EXPERTS)))
    wrh = wr.astype(BF16)
    wrl = (wr - wrh.astype(F32)).astype(BF16)
    x1, hn2, aff, afft = _merge_call(o[0], o[1], o[2], sga, gcb, xa, xb, w_attn_branch.astype(BF16),
                                     w_out.astype(BF16), ffn_norm_g[None, :], wrh, wrl)

    slot, offs, slot_tm = _route_call(afft, n_groups, cap)

    cpt = tm // LANES
    tiles_per_group = group_rows // tm
    starts = offs[::cpt, :, 0].reshape(n_groups, tiles_per_group, N_EXPERTS)
    ends = jnp.concatenate([starts[:, 1:], jnp.full((n_groups, 1, N_EXPERTS), cap, I32)], axis=1)
    xe = _dispatch_call(starts, ends, hn2, slot, n_groups, cap)
    y = _ffn_call(xe, w_gate, w_up, w_down, cap)

    astart = jnp.minimum((starts // 16) * 16, cap - COMB_W)
    nrounds = jnp.maximum(jnp.max((ends - astart + COMB_W - 1) // COMB_W, axis=-1), 1)
    return _combine_call(astart.reshape(-1), nrounds.reshape(-1), x1, aff, slot_tm, y, n_groups, cap, na)


def _run(x_prompt, x_sample, params):
    bp, sp, d = x_prompt.shape
    bs, ss, _ = x_sample.shape
    seqs, pos = [], 0
    for b, s in ((bp, sp), (bs, ss)):
        for _ in range(b):
            seqs.append((pos, s))
            pos += s
    assert bp * sp == bs * ss, "request groups are routed with one capacity"
    xa, xb = x_prompt.reshape(bp * sp, d), x_sample.reshape(bs * ss, d)
    depth = params[0].shape[0]
    for layer in range(depth):
        xa, xb = _layer(xa, xb, tuple(seqs), 2, *[p[layer] for p in params])
    return xa.reshape(bp, sp, d), xb.reshape(bs, ss, d)


def kernel(x_prompt, x_sample, mix_norm_g, w_in, q_norm_g, k_norm_g, conv_w, w_attn_branch, w_conv_branch, w_out,
           ffn_norm_g, w_router, w_gate, w_up, w_down):
    params = (mix_norm_g, w_in, q_norm_g, k_norm_g, conv_w, w_attn_branch, w_conv_branch, w_out,
              ffn_norm_g, w_router, w_gate, w_up, w_down)
    return _run(x_prompt, x_sample, params)
```

```python
import functools

import numpy as np
import jax
import jax.numpy as jnp
from jax import lax
from jax.experimental import pallas as pl
from jax.experimental.pallas import tpu as pltpu

F32 = jnp.float32
BF16 = jnp.bfloat16
I32 = jnp.int32

D_MODEL = 1024
HEAD_DIM = 64
N_SLOTS = 8
DILATIONS = (1, 4, 16)
KEYS_PER_SIDE = 64
GROUP_W = N_SLOTS * HEAD_DIM
ATT_W = len(DILATIONS) * GROUP_W
CONV_W = D_MODEL
IN_W = 3 * ATT_W + 3 * CONV_W + 2 * D_MODEL
N_EXPERTS = 16
EXPERT_FF = 2048
CAPACITY_FACTOR = 2
RMS_EPS = 1e-6
NEG = -1e30

LANES = 128
PAIR_SLABS = GROUP_W // LANES
VMEM_LIMIT = 58 * 1024 * 1024

PROJ_TM = 256
MERGE_TM = 256
PROJ_HALO = 16
ATT_TB_MAX = 2048
ATT_TQ = 128
ATT_UNROLL = 4
FFN_FC = 512
FFN_RB = 512
DISP_SB = 2048
DISP_EG = 4
DISP_W = 96
COMB_W = 64
COMB_W_LOG2 = 6


def _cparams(sem):
    return pltpu.CompilerParams(dimension_semantics=sem, vmem_limit_bytes=VMEM_LIMIT)


def _any_eq(i, values):
    return functools.reduce(jnp.logical_or, [i == v for v in values])


def _const_spec(shape):
    nd = len(shape)
    return pl.BlockSpec(shape, lambda *a: (0,) * nd, pipeline_mode=pl.Buffered(1))


def _two_array_specs(block_rows, n_first_blocks, cols, prefetch=False):
    if prefetch:
        first = pl.BlockSpec((block_rows, cols), lambda i, *_: (jnp.minimum(i, n_first_blocks - 1), 0))
        second = pl.BlockSpec((block_rows, cols), lambda i, *_: (jnp.maximum(i - n_first_blocks, 0), 0))
    else:
        first = pl.BlockSpec((block_rows, cols), lambda i: (jnp.minimum(i, n_first_blocks - 1), 0))
        second = pl.BlockSpec((block_rows, cols), lambda i: (jnp.maximum(i - n_first_blocks, 0), 0))
    return first, second


def _proj_body(xa_ref, xb_ref, xpa_ref, xpb_ref, xna_ref, xnb_ref, ng_ref, w_ref, qg_ref, kg_ref, cw_ref, wcb_ref,
               q0_ref, k0_ref, v0_ref, q1_ref, k1_ref, v1_ref, q2_ref, k2_ref, v2_ref, sga_ref, gcb_ref, hs, hp,
               *, tm, halo, n_first, first_tiles, last_tiles):
    i = pl.program_id(0)
    is_first = _any_eq(i, first_tiles)
    is_last = _any_eq(i, last_tiles)
    in_a = i < n_first

    def pick(a_ref, b_ref):
        return jnp.where(in_a, a_ref[...], b_ref[...])

    x_ext = jnp.concatenate([pick(xpa_ref, xpb_ref), pick(xa_ref, xb_ref), pick(xna_ref, xnb_ref)], axis=0)
    ms = jnp.mean(x_ext * x_ext, axis=-1, keepdims=True)
    hn_f32 = x_ext * lax.rsqrt(ms + RMS_EPS) * ng_ref[...]
    hn_ext = hn_f32.astype(BF16)
    hn = hn_ext[halo:halo + tm]

    for sd in range(D_MODEL // LANES):
        hs[sd] = hn_f32[halo:halo + tm, sd * LANES:(sd + 1) * LANES]
    hn_of_group = [hn]
    for gi, r in enumerate(DILATIONS):
        if r == 1:
            continue
        for c in range(r):
            for sd in range(D_MODEL // LANES):
                hp[gi - 1, c * (tm // r):(c + 1) * (tm // r), sd * LANES:(sd + 1) * LANES] = (
                    hs[sd, pl.ds(c, tm // r, stride=r), :].astype(BF16))
        hn_of_group.append(hp[gi - 1])

    def proj(lhs, a, b):
        return jnp.dot(lhs, w_ref[:, a:b], preferred_element_type=F32)

    lo_lane = lax.broadcasted_iota(I32, (1, LANES), 1) < HEAD_DIM

    def head_norm(p, g_ref, col0, scale):
        out = []
        for s in range(PAIR_SLABS):
            xs = p[:, s * LANES:(s + 1) * LANES]
            x2 = xs * xs
            tot = jnp.sum(x2, axis=-1, keepdims=True)
            lo = jnp.sum(jnp.where(lo_lane, x2, 0.0), axis=-1, keepdims=True)
            hi = tot - lo
            r = jnp.where(lo_lane, lax.rsqrt(lo * (1.0 / HEAD_DIM) + RMS_EPS),
                          lax.rsqrt(hi * (1.0 / HEAD_DIM) + RMS_EPS))
            y = xs * r * g_ref[:, col0 + s * LANES:col0 + (s + 1) * LANES]
            out.append(y * scale if scale != 1.0 else y)
        return out

    def emit(gi, out_ref, slabs):
        r = DILATIONS[gi]
        for s in range(PAIR_SLABS):
            y = slabs[s].astype(out_ref.dtype)
            if r == 1:
                out_ref[:, s * LANES:(s + 1) * LANES] = y
            else:
                for c in range(r):
                    out_ref[c, :, s * LANES:(s + 1) * LANES] = y[c * (tm // r):(c + 1) * (tm // r)]

    q_refs, k_refs, v_refs = (q0_ref, q1_ref, q2_ref), (k0_ref, k1_ref, k2_ref), (v0_ref, v1_ref, v2_ref)
    for gi in range(len(DILATIONS)):
        c0 = gi * GROUP_W
        lhs = hn_of_group[gi]
        emit(gi, q_refs[gi], head_norm(proj(lhs, c0, c0 + GROUP_W), qg_ref, c0, HEAD_DIM ** -0.5))
        emit(gi, k_refs[gi], head_norm(proj(lhs, ATT_W + c0, ATT_W + c0 + GROUP_W), kg_ref, c0, 1.0))
        pv = proj(lhs, 2 * ATT_W + c0, 2 * ATT_W + c0 + GROUP_W)
        emit(gi, v_refs[gi], [pv[:, s * LANES:(s + 1) * LANES] for s in range(PAIR_SLABS)])

    c0 = 3 * ATT_W
    cb = proj(hn, c0, c0 + CONV_W)
    cc = proj(hn_ext, c0 + CONV_W, c0 + 2 * CONV_W)
    cx = proj(hn_ext, c0 + 2 * CONV_W, c0 + 3 * CONV_W)
    u_ext = cc * cx
    row = lax.broadcasted_iota(I32, (tm + 2 * halo, 1), 0)
    keep = jnp.logical_and(jnp.logical_or(row >= halo, jnp.logical_not(is_first)),
                           jnp.logical_or(row < halo + tm, jnp.logical_not(is_last)))
    u_ext = jnp.where(keep, u_ext, 0.0)
    cy = (u_ext[halo - 1:halo - 1 + tm] * cw_ref[0:1, :]
          + u_ext[halo:halo + tm] * cw_ref[1:2, :]
          + u_ext[halo + 1:halo + 1 + tm] * cw_ref[2:3, :])
    conv_in = (cb * cy).astype(BF16)
    cbr = jnp.dot(conv_in, wcb_ref[...], preferred_element_type=F32)
    g0 = c0 + 3 * CONV_W
    sga_ref[...] = jax.nn.sigmoid(proj(hn, g0, g0 + D_MODEL))
    gcb_ref[...] = jax.nn.sigmoid(proj(hn, g0 + D_MODEL, g0 + 2 * D_MODEL)) * cbr


def _proj_call(xa, xb, seqs, ng, w_in, qg, kg, cw, wcb):
    na, nb_rows = xa.shape[0], xb.shape[0]
    n = na + nb_rows
    tm, halo = PROJ_TM, PROJ_HALO
    nt = n // tm
    n_first = na // tm
    hb = tm // halo
    first_tiles = tuple(s // tm for s, _ in seqs)
    last_tiles = tuple((s + l) // tm - 1 for s, l in seqs)
    nha, nhb = na // halo, nb_rows // halo
    body = functools.partial(_proj_body, tm=tm, halo=halo, n_first=n_first,
                             first_tiles=first_tiles, last_tiles=last_tiles)
    cur_a, cur_b = _two_array_specs(tm, n_first, D_MODEL)
    prev_a = pl.BlockSpec((halo, D_MODEL), lambda i: (jnp.clip(i * hb - 1, 0, nha - 1), 0))
    prev_b = pl.BlockSpec((halo, D_MODEL), lambda i: (jnp.clip((i - n_first) * hb - 1, 0, nhb - 1), 0))
    next_a = pl.BlockSpec((halo, D_MODEL), lambda i: (jnp.clip((i + 1) * hb, 0, nha - 1), 0))
    next_b = pl.BlockSpec((halo, D_MODEL), lambda i: (jnp.clip((i + 1 - n_first) * hb, 0, nhb - 1), 0))
    out_specs, out_shape = [], []
    for gi, r in enumerate(DILATIONS):
        for _ in range(3):
            if r == 1:
                out_specs.append(pl.BlockSpec((tm, GROUP_W), lambda i: (i, 0)))
                out_shape.append(jax.ShapeDtypeStruct((n, GROUP_W), BF16))
            else:
                out_specs.append(pl.BlockSpec((r, tm // r, GROUP_W), lambda i: (0, i, 0)))
                out_shape.append(jax.ShapeDtypeStruct((r, n // r, GROUP_W), BF16))
    out_specs += [pl.BlockSpec((tm, D_MODEL), lambda i: (i, 0))] * 2
    out_shape += [jax.ShapeDtypeStruct((n, D_MODEL), F32)] * 2
    return pl.pallas_call(
        body,
        grid=(nt,),
        in_specs=[cur_a, cur_b, prev_a, prev_b, next_a, next_b,
                  _const_spec((1, D_MODEL)), _const_spec((D_MODEL, IN_W)), _const_spec((1, ATT_W)),
                  _const_spec((1, ATT_W)), _const_spec((3, CONV_W)), _const_spec((CONV_W, D_MODEL))],
        out_specs=out_specs,
        out_shape=out_shape,
        scratch_shapes=[pltpu.VMEM((D_MODEL // LANES, tm, LANES), F32),
                        pltpu.VMEM((len(DILATIONS) - 1, tm, D_MODEL), BF16)],
        compiler_params=_cparams(("arbitrary",)),
        name="proj",
    )(xa, xb, xa, xb, xa, xb, ng, w_in, qg, kg, cw, wcb)


def _attn_body(q_ref, kp_ref, kc_ref, kn_ref, vp_ref, vc_ref, vn_ref, bias_ref, o_ref, kw, vw,
               *, tb, first_tiles, last_tiles):
    i = pl.program_id(1)
    is_first = _any_eq(i, first_tiles)
    is_last = _any_eq(i, last_tiles)
    h = KEYS_PER_SIDE
    tq = ATT_TQ
    kwin = tq + 2 * h
    nq = tb // tq

    for s in range(PAIR_SLABS):
        sl = slice(s * LANES, (s + 1) * LANES)
        for dst, p_ref, c_ref, n_ref in ((kw, kp_ref, kc_ref, kn_ref), (vw, vp_ref, vc_ref, vn_ref)):
            dst[s, 0:h, :] = p_ref[:, sl]
            dst[s, h:h + tb, :] = c_ref[:, sl]
            dst[s, h + tb:2 * h + tb, :] = n_ref[:, sl]

    lane = lax.broadcasted_iota(I32, (1, LANES), 1)
    lo_lane = lane < HEAD_DIM
    col = lax.broadcasted_iota(I32, (1, kwin), 1)

    def head_pair(q2, k2, v2, biases):
        o_acc = None
        lse2 = None
        for half in range(2):
            mask = lo_lane if half == 0 else jnp.logical_not(lo_lane)
            qm = jnp.where(mask, q2, jnp.zeros_like(q2))
            sc = lax.dot_general(qm, k2, (((1,), (1,)), ((), ())), preferred_element_type=F32)
            sc = sc + biases[half]
            m = jnp.max(sc, axis=-1, keepdims=True)
            p = jnp.exp(sc - m)
            l = jnp.sum(p, axis=-1, keepdims=True)
            vm = jnp.where(mask, v2, jnp.zeros_like(v2))
            pv = jnp.dot(p.astype(BF16), vm, preferred_element_type=F32)
            contrib = pv * (1.0 / l)
            lse = m + jnp.log(l)
            o_acc = contrib if o_acc is None else o_acc + contrib
            lse2 = jnp.broadcast_to(lse, (tq, LANES)) if lse2 is None else jnp.where(mask, lse, lse2)
        return o_acc, lse2

    def step(qb, carry):
        pen_lo = jnp.where(jnp.logical_and(is_first, qb == 0), NEG, 0.0)
        pen_hi = jnp.where(jnp.logical_and(is_last, qb == nq - 1), NEG, 0.0)
        edge = (jnp.where(col < h, pen_lo, 0.0) + jnp.where(col >= h + tq, pen_hi, 0.0))
        q0 = pl.multiple_of(qb * tq, tq)
        for s in range(PAIR_SLABS):
            biases = [bias_ref[2 * s] + edge, bias_ref[2 * s + 1] + edge]
            o2, lse2 = head_pair(q_ref[pl.ds(q0, tq), s * LANES:(s + 1) * LANES],
                                 kw[s, pl.ds(q0, kwin), :], vw[s, pl.ds(q0, kwin), :], biases)
            o_ref[s, pl.ds(q0, tq), :] = o2
            o_ref[PAIR_SLABS + s, pl.ds(q0, tq), :] = lse2
        return carry

    lax.fori_loop(0, nq, step, 0, unroll=min(ATT_UNROLL, nq))


def _alibi_bias(gi, r):
    n_heads = len(DILATIONS) * N_SLOTS
    slopes = (2.0 ** (-8.0 * (np.arange(n_heads, dtype=np.float32) + 1.0) / n_heads)).astype(np.float32)
    slopes = slopes.reshape(len(DILATIONS), N_SLOTS)[gi]
    qi = np.arange(ATT_TQ)[:, None]
    kj = np.arange(ATT_TQ + 2 * KEYS_PER_SIDE)[None, :]
    off = kj - KEYS_PER_SIDE - qi
    dist = (r * np.abs(off)).astype(np.float32)
    bias = -slopes[:, None, None] * dist[None]
    bias = np.where((np.abs(off) <= KEYS_PER_SIDE)[None], bias, np.float32(NEG))
    return jnp.asarray(bias, F32)


def _attn_call(q, k, v, seqs, gi):
    r, rows, _ = q.shape
    h = KEYS_PER_SIDE
    tb = min([ATT_TB_MAX] + [l // r for _, l in seqs])
    assert all((s // r) % tb == 0 and (l // r) % tb == 0 for s, l in seqs)
    nt = rows // tb
    hb = tb // h
    nhb = rows // h
    first_tiles = tuple((s // r) // tb for s, _ in seqs)
    last_tiles = tuple(((s + l) // r) // tb - 1 for s, l in seqs)
    kwin = ATT_TQ + 2 * h
    body = functools.partial(_attn_body, tb=tb, first_tiles=first_tiles, last_tiles=last_tiles)
    sq = pl.Squeezed()
    cur = pl.BlockSpec((sq, tb, GROUP_W), lambda c, i: (c, i, 0))
    prev = pl.BlockSpec((sq, h, GROUP_W), lambda c, i: (c, jnp.maximum(i * hb - 1, 0), 0))
    nxt = pl.BlockSpec((sq, h, GROUP_W), lambda c, i: (c, jnp.minimum((i + 1) * hb, nhb - 1), 0))
    return pl.pallas_call(
        body,
        grid=(r, nt),
        in_specs=[cur, prev, cur, nxt, prev, cur, nxt, _const_spec((N_SLOTS, ATT_TQ, kwin))],
        out_specs=pl.BlockSpec((2 * PAIR_SLABS, sq, tb, LANES), lambda c, i: (0, c, i, 0)),
        out_shape=jax.ShapeDtypeStruct((2 * PAIR_SLABS, r, rows, LANES), F32),
        scratch_shapes=[pltpu.VMEM((PAIR_SLABS, tb + 2 * h, LANES), BF16),
                        pltpu.VMEM((PAIR_SLABS, tb + 2 * h, LANES), BF16)],
        compiler_params=_cparams(("arbitrary", "arbitrary")),
        name=f"attn_d{r}",
    )(q, k, k, k, v, v, v, _alibi_bias(gi, r))


def _split_bf16(x):
    hi = x.astype(BF16)
    lo = (x - hi.astype(F32)).astype(BF16)
    return hi, lo


def _merge_body(o0_ref, o1_ref, o2_ref, sga_ref, gcb_ref, xa_ref, xb_ref, wab_ref, wout_ref, g2_ref,
                wrh_ref, wrl_ref, x1_ref, hn_ref, aff_ref, afft_ref, t1, t2, *, tm, n_first):
    i = pl.program_id(0)
    for o_ref, scr, r in ((o1_ref, t1, DILATIONS[1]), (o2_ref, t2, DILATIONS[2])):
        for s in range(2 * PAIR_SLABS):
            for c in range(r):
                scr[s, pl.ds(c, tm // r, stride=r), :] = o_ref[s, c]
    slabs = []
    for s in range(PAIR_SLABS):
        l0, l1, l2 = o0_ref[PAIR_SLABS + s, 0], t1[PAIR_SLABS + s], t2[PAIR_SLABS + s]
        mx = jnp.maximum(jnp.maximum(l0, l1), l2)
        e0, e1, e2 = jnp.exp(l0 - mx), jnp.exp(l1 - mx), jnp.exp(l2 - mx)
        num = e0 * o0_ref[s, 0] + e1 * t1[s] + e2 * t2[s]
        slabs.append(num / (e0 + e1 + e2))
    att = jnp.concatenate(slabs, axis=-1).astype(BF16)
    a_br = jnp.dot(att, wab_ref[...], preferred_element_type=F32)
    merged = (sga_ref[...] * a_br + gcb_ref[...]).astype(BF16)
    x = jnp.where(i < n_first, xa_ref[...], xb_ref[...])
    x1 = x + jnp.dot(merged, wout_ref[...], preferred_element_type=F32)
    x1_ref[...] = x1
    ms = jnp.mean(x1 * x1, axis=-1, keepdims=True)
    hn2 = x1 * lax.rsqrt(ms + RMS_EPS) * g2_ref[...]
    hn_ref[...] = hn2.astype(BF16)
    hh, hl = _split_bf16(hn2)
    logits = (jnp.dot(hh, wrh_ref[...], preferred_element_type=F32)
              + jnp.dot(hl, wrh_ref[...], preferred_element_type=F32)
              + jnp.dot(hh, wrl_ref[...], preferred_element_type=F32))
    lane = lax.broadcasted_iota(I32, (1, LANES), 1)
    logits = jnp.where(lane < N_EXPERTS, logits, NEG)
    e = jnp.exp(logits - jnp.max(logits, axis=-1, keepdims=True))
    aff = e / jnp.sum(e, axis=-1, keepdims=True)
    aff_ref[...] = aff
    aff_t = aff.T
    for c in range(tm // LANES):
        afft_ref[c] = aff_t[0:N_EXPERTS, c * LANES:(c + 1) * LANES]


def _merge_call(o0, o1, o2, sga, gcb, xa, xb, wab, wout, g2, wrh, wrl):
    n = sga.shape[0]
    tm = MERGE_TM
    nt = n // tm
    n_first = xa.shape[0] // tm
    body = functools.partial(_merge_body, tm=tm, n_first=n_first)
    ns = 2 * PAIR_SLABS
    o_specs = [pl.BlockSpec((ns, r, tm // r, LANES), lambda i: (0, 0, i, 0)) for r in DILATIONS]
    rowspec = pl.BlockSpec((tm, D_MODEL), lambda i: (i, 0))
    xa_spec, xb_spec = _two_array_specs(tm, n_first, D_MODEL)
    return pl.pallas_call(
        body,
        grid=(nt,),
        in_specs=o_specs + [rowspec, rowspec, xa_spec, xb_spec,
                            _const_spec((GROUP_W, D_MODEL)), _const_spec((D_MODEL, D_MODEL)),
                            _const_spec((1, D_MODEL)), _const_spec((D_MODEL, LANES)),
                            _const_spec((D_MODEL, LANES))],
        out_specs=[rowspec, rowspec,
                   pl.BlockSpec((tm, LANES), lambda i: (i, 0)),
                   pl.BlockSpec((tm // LANES, N_EXPERTS, LANES), lambda i: (i, 0, 0))],
        out_shape=[jax.ShapeDtypeStruct((n, D_MODEL), F32),
                   jax.ShapeDtypeStruct((n, D_MODEL), BF16),
                   jax.ShapeDtypeStruct((n, LANES), F32),
                   jax.ShapeDtypeStruct((n // LANES, N_EXPERTS, LANES), F32)],
        scratch_shapes=[pltpu.VMEM((ns, tm, LANES), F32), pltpu.VMEM((ns, tm, LANES), F32)],
        compiler_params=_cparams(("arbitrary",)),
        name="merge",
    )(o0, o1, o2, sga, gcb, xa, xb, wab, wout, g2, wrh, wrl)


def _route_body(aff_ref, tri_ref, ones_ref, low_ref, slot_ref, offs_ref, slot_tm_ref, rs_scr, off_scr,
                *, nch, cap):
    a = aff_ref[...]

    def count(mask):
        c = jnp.sum(mask.astype(F32), axis=0, keepdims=True)
        return jnp.sum(c, axis=2, keepdims=True)

    cur = jnp.zeros((1, N_EXPERTS, 1), I32)
    for bit in range(30, -1, -1):
        cand = cur | (1 << bit)
        cur = jnp.where(count(a >= lax.bitcast_convert_type(cand, F32)) >= cap, cand, cur)
    tau = lax.bitcast_convert_type(cur, F32)
    gt = a > tau
    eq = a == tau
    need = cap - count(gt)

    def prefix(mask):
        m = mask.astype(BF16).reshape(nch * N_EXPERTS, LANES)
        incl = jnp.dot(m, tri_ref[...], preferred_element_type=F32).reshape(nch, N_EXPERTS, LANES)
        rs_scr[...] = jnp.dot(m, ones_ref[...], preferred_element_type=F32).reshape(nch, N_EXPERTS, LANES)
        for e in range(N_EXPERTS):
            off_scr[:, e, :] = jnp.dot(low_ref[...], rs_scr[:, e, :].astype(BF16), preferred_element_type=F32)
        off = off_scr[...]
        return incl + off - mask.astype(F32), off

    tie_rank, _ = prefix(eq)
    sel = jnp.logical_or(gt, jnp.logical_and(eq, tie_rank < need))
    pos, off = prefix(sel)
    slot = jnp.where(sel, pos, -1.0)
    slot_ref[...] = slot.astype(I32)
    offs_ref[...] = off.astype(I32)
    rs_scr[...] = slot
    pad = jnp.full((LANES - N_EXPERTS, LANES), -1.0, F32)

    def tr(c, carry):
        tile = jnp.concatenate([rs_scr[c], pad], axis=0)
        slot_tm_ref[pl.ds(pl.multiple_of(c * LANES, LANES), LANES), :] = tile.T.astype(I32)
        return carry

    lax.fori_loop(0, nch, tr, 0)


def _route_call(afft, n_groups, cap):
    nchunks = afft.shape[0]
    nch = nchunks // n_groups
    tri = jnp.asarray(np.triu(np.ones((LANES, LANES), np.float32)), BF16)
    ones = jnp.ones((LANES, LANES), BF16)
    low = jnp.asarray(np.tril(np.ones((nch, nch), np.float32), -1), BF16)
    body = functools.partial(_route_body, nch=nch, cap=cap)
    blk = pl.BlockSpec((nch, N_EXPERTS, LANES), lambda b: (b, 0, 0))
    return pl.pallas_call(
        body,
        grid=(n_groups,),
        in_specs=[blk, _const_spec((LANES, LANES)), _const_spec((LANES, LANES)), _const_spec((nch, nch))],
        out_specs=[blk, blk, pl.BlockSpec((nch * LANES, LANES), lambda b: (b, 0))],
        out_shape=[jax.ShapeDtypeStruct((nchunks, N_EXPERTS, LANES), I32),
                   jax.ShapeDtypeStruct((nchunks, N_EXPERTS, LANES), I32),
                   jax.ShapeDtypeStruct((nchunks * LANES, LANES), I32)],
        scratch_shapes=[pltpu.VMEM((nch, N_EXPERTS, LANES), F32), pltpu.VMEM((nch, N_EXPERTS, LANES), F32)],
        compiler_params=_cparams(("arbitrary",)),
        name="route",
    )(afft, tri, ones, low)


def _dispatch_body(starts_ref, ends_ref, dense_ref, hn_ref, slot_ref, x_ref, *, tm, sb, tiles_per_group):
    b, eg, j = pl.program_id(0), pl.program_id(1), pl.program_id(2)
    w = DISP_W

    @pl.when(j == 0)
    def _():
        x_ref[...] = jnp.zeros_like(x_ref)

    iota_w = lax.broadcasted_iota(I32, (w, 1), 0)
    cpt = tm // LANES

    def window(t, ee):
        e = eg * DISP_EG + ee
        idx = (b * tiles_per_group + j * (sb // tm) + t) * N_EXPERTS + e
        base = pl.multiple_of((starts_ref[idx] // 16) * 16, 16)
        srow = jnp.concatenate([slot_ref[t * cpt + c, pl.ds(e, 1), :] for c in range(cpt)], axis=1)
        return base, ends_ref[idx], srow

    for t in range(sb // tm):
        wins = [window(t, ee) for ee in range(DISP_EG)]
        oh = jnp.concatenate([(srow - base == iota_w).astype(BF16) for base, _, srow in wins], axis=0)
        res = jnp.dot(oh, hn_ref[t * tm:(t + 1) * tm, :], preferred_element_type=F32)
        for ee, (base, _, _) in enumerate(wins):
            x_ref[ee, pl.ds(base, w), :] += res[ee * w:(ee + 1) * w].astype(x_ref.dtype)

    @pl.when(dense_ref[(b * pl.num_programs(1) + eg) * pl.num_programs(2) + j] > 0)
    def _():
        def tile_body(t, carry):
            hn_t = hn_ref[pl.ds(pl.multiple_of(t * tm, tm), tm), :]
            for ee in range(DISP_EG):
                base, end, srow = window(t, ee)

                def more(k, c2, ee=ee, base=base, srow=srow):
                    bk = pl.multiple_of(base + k * w, 16)
                    ohk = (srow - bk == iota_w).astype(BF16)
                    x_ref[ee, pl.ds(bk, w), :] += jnp.dot(ohk, hn_t,
                                                          preferred_element_type=F32).astype(x_ref.dtype)
                    return c2

                lax.fori_loop(1, (end - base + w - 1) // w, more, 0)
            return carry

        lax.fori_loop(0, sb // tm, tile_body, 0)


def _dispatch_call(starts, ends, hn, slot, n_groups, cap):
    n = hn.shape[0]
    ng_rows = n // n_groups
    tm, sb = PROJ_TM, DISP_SB
    nsb = ng_rows // sb
    tiles_per_group = ng_rows // tm
    cps = sb // LANES
    n_eg = N_EXPERTS // DISP_EG
    rounds = (ends - (starts // 16) * 16 + DISP_W - 1) // DISP_W
    dense = jnp.max(rounds.reshape(n_groups, nsb, sb // tm, n_eg, DISP_EG), axis=(2, 4)) > 1
    dense = jnp.transpose(dense, (0, 2, 1)).astype(I32).reshape(-1)
    starts, ends = starts.reshape(-1), ends.reshape(-1)
    body = functools.partial(_dispatch_body, tm=tm, sb=sb, tiles_per_group=tiles_per_group)
    sq = pl.Squeezed()
    grid_spec = pltpu.PrefetchScalarGridSpec(
        num_scalar_prefetch=3,
        grid=(n_groups, N_EXPERTS // DISP_EG, nsb),
        in_specs=[pl.BlockSpec((sb, D_MODEL), lambda b, g, j, *_: (b * nsb + j, 0)),
                  pl.BlockSpec((cps, N_EXPERTS, LANES), lambda b, g, j, *_: (b * nsb + j, 0, 0))],
        out_specs=pl.BlockSpec((sq, DISP_EG, cap + DISP_W, D_MODEL), lambda b, g, j, *_: (b, g, 0, 0)),
    )
    return pl.pallas_call(
        body,
        grid_spec=grid_spec,
        out_shape=jax.ShapeDtypeStruct((n_groups, N_EXPERTS, cap + DISP_W, D_MODEL), BF16),
        compiler_params=_cparams(("arbitrary", "arbitrary", "arbitrary")),
        name="dispatch",
    )(starts, ends, dense, hn, slot)


def _ffn_body(x_ref, wg_ref, wu_ref, wd_ref, y_ref, acc, *, cap):
    f = pl.program_id(2)
    nf = pl.num_programs(2)

    @pl.when(f == 0)
    def _():
        acc[...] = jnp.zeros_like(acc)

    wg = wg_ref[...].astype(BF16)
    wu = wu_ref[...].astype(BF16)
    wd = wd_ref[...].astype(BF16)
    for rb in range(cap // FFN_RB):
        rows = slice(rb * FFN_RB, (rb + 1) * FFN_RB)
        xb = x_ref[rows, :]
        g = jnp.dot(xb, wg, preferred_element_type=F32)
        u = jnp.dot(xb, wu, preferred_element_type=F32)
        hid = (g * jax.nn.sigmoid(g) * u).astype(BF16)
        acc[rows, :] += jnp.dot(hid, wd, preferred_element_type=F32)

    @pl.when(f == nf - 1)
    def _():
        y_ref[...] = acc[...].astype(y_ref.dtype)


def _ffn_call(xe, w_gate, w_up, w_down, cap):
    n_groups = xe.shape[0]
    fc = FFN_FC
    nf = EXPERT_FF // fc
    body = functools.partial(_ffn_body, cap=cap)
    sq = pl.Squeezed()
    return pl.pallas_call(
        body,
        grid=(n_groups, N_EXPERTS, nf),
        in_specs=[pl.BlockSpec((sq, sq, cap, D_MODEL), lambda b, e, f: (b, e, 0, 0)),
                  pl.BlockSpec((sq, D_MODEL, fc), lambda b, e, f: (e, 0, f)),
                  pl.BlockSpec((sq, D_MODEL, fc), lambda b, e, f: (e, 0, f)),
                  pl.BlockSpec((sq, fc, D_MODEL), lambda b, e, f: (e, f, 0))],
        out_specs=pl.BlockSpec((sq, sq, cap, D_MODEL), lambda b, e, f: (b, e, 0, 0)),
        out_shape=jax.ShapeDtypeStruct((n_groups, N_EXPERTS, cap, D_MODEL), BF16),
        scratch_shapes=[pltpu.VMEM((cap, D_MODEL), F32)],
        compiler_params=_cparams(("arbitrary", "arbitrary", "arbitrary")),
        name="ffn",
    )(xe, w_gate, w_up, w_down)


def _combine_body(astart_ref, nrounds_ref, x1_ref, aff_ref, slot_ref, spread_ref, y_hbm, outa_ref, outb_ref,
                  ybuf, sem, yext, sem_ext, accs, *, tm, cap, tiles_per_group, n_first):
    i = pl.program_id(0)
    nt = pl.num_programs(0)
    w = COMB_W

    def window_copy(tile, e, k, dst, dsem):
        a0 = astart_ref[tile * N_EXPERTS + e]
        a = pl.multiple_of(jnp.minimum(a0 + k * w, cap - w), 16)
        return pltpu.make_async_copy(y_hbm.at[tile // tiles_per_group, e, pl.ds(a, w), :], dst, dsem)

    def first_copy(tile, e):
        sl = tile % 2
        return window_copy(tile, e, 0, ybuf.at[sl, pl.ds(e * w, w)], sem.at[sl, e])

    @pl.when(i == 0)
    def _():
        for e in range(N_EXPERTS):
            first_copy(i, e).start()

    @pl.when(i + 1 < nt)
    def _():
        for e in range(N_EXPERTS):
            first_copy(i + 1, e).start()

    def spread(cols):
        return jnp.dot(cols.astype(BF16), spread_ref[...], preferred_element_type=F32)

    slot_t = slot_ref[...]
    enc = jnp.where(slot_t < 0, w * w - 1, slot_t)
    s_hi = spread((enc >> COMB_W_LOG2).astype(F32))
    s_lo = spread((enc & (w - 1)).astype(F32))
    gate = aff_ref[...]
    g_hi = gate.astype(BF16).astype(F32)
    g_hi_s = spread(g_hi)
    g_lo_s = spread(gate - g_hi)
    wide = lax.broadcasted_iota(I32, (1, N_EXPERTS * w), 1)
    owner = wide >> COMB_W_LOG2
    local = wide & (w - 1)

    def expand(k, ywin):
        row0 = jnp.zeros_like(wide)
        skip = jnp.zeros_like(wide)
        for e in range(N_EXPERTS):
            lo = astart_ref[i * N_EXPERTS + e] + k * w
            a = jnp.minimum(lo, cap - w)
            row0 = jnp.where(owner == e, a, row0)
            skip = jnp.where(owner == e, lo - a, skip)
        want = row0 + local
        want_hi = jnp.where(local >= skip, want >> COMB_W_LOG2, -1).astype(F32)
        want_lo = (want & (w - 1)).astype(F32)
        match = jnp.logical_and(s_hi == want_hi, s_lo == want_lo)
        return (jnp.dot(jnp.where(match, g_hi_s, 0.0).astype(BF16), ywin, preferred_element_type=F32)
                + jnp.dot(jnp.where(match, g_lo_s, 0.0).astype(BF16), ywin, preferred_element_type=F32))

    for e in range(N_EXPERTS):
        first_copy(i, e).wait()
    accs[...] = x1_ref[...] + expand(0, ybuf[i % 2])

    def extra_round(k, carry):
        for e in range(N_EXPERTS):
            window_copy(i, e, k, yext.at[pl.ds(e * w, w)], sem_ext.at[e]).start()
        for e in range(N_EXPERTS):
            window_copy(i, e, k, yext.at[pl.ds(e * w, w)], sem_ext.at[e]).wait()
        accs[...] += expand(k, yext[...])
        return carry

    lax.fori_loop(1, nrounds_ref[i], extra_round, 0)

    @pl.when(i < n_first)
    def _():
        outa_ref[...] = accs[...]

    @pl.when(i >= n_first)
    def _():
        outb_ref[...] = accs[...]


def _combine_call(astart, nrounds, x1, aff, slot_tm, y, n_groups, cap, na):
    n = x1.shape[0]
    tm = PROJ_TM
    nt = n // tm
    n_first = na // tm
    tiles_per_group = nt // n_groups
    body = functools.partial(_combine_body, tm=tm, cap=cap, tiles_per_group=tiles_per_group, n_first=n_first)
    outa_spec, outb_spec = _two_array_specs(tm, n_first, D_MODEL, prefetch=True)
    assert (1 << COMB_W_LOG2) == COMB_W and (cap - 1) // COMB_W < COMB_W - 1
    spread = np.zeros((LANES, N_EXPERTS * COMB_W), np.float32)
    for e in range(N_EXPERTS):
        spread[e, e * COMB_W:(e + 1) * COMB_W] = 1.0
    spread = jnp.asarray(spread, BF16)
    grid_spec = pltpu.PrefetchScalarGridSpec(
        num_scalar_prefetch=2,
        grid=(nt,),
        in_specs=[pl.BlockSpec((tm, D_MODEL), lambda i, *_: (i, 0)),
                  pl.BlockSpec((tm, LANES), lambda i, *_: (i, 0)),
                  pl.BlockSpec((tm, LANES), lambda i, *_: (i, 0)),
                  pl.BlockSpec((LANES, N_EXPERTS * COMB_W), lambda i, *_: (0, 0), pipeline_mode=pl.Buffered(1)),
                  pl.BlockSpec(memory_space=pl.ANY)],
        out_specs=[outa_spec, outb_spec],
        scratch_shapes=[pltpu.VMEM((2, N_EXPERTS * COMB_W, D_MODEL), BF16),
                        pltpu.SemaphoreType.DMA((2, N_EXPERTS)),
                        pltpu.VMEM((N_EXPERTS * COMB_W, D_MODEL), BF16),
                        pltpu.SemaphoreType.DMA((N_EXPERTS,)),
                        pltpu.VMEM((tm, D_MODEL), F32)],
    )
    return pl.pallas_call(
        body,
        grid_spec=grid_spec,
        out_shape=[jax.ShapeDtypeStruct((na, D_MODEL), F32), jax.ShapeDtypeStruct((n - na, D_MODEL), F32)],
        compiler_params=_cparams(("arbitrary",)),
        name="combine",
    )(astart, nrounds, x1, aff, slot_tm, spread, y)


def _layer(xa, xb, seqs, n_groups, mix_norm_g, w_in, q_norm_g, k_norm_g, conv_w, w_attn_branch, w_conv_branch,
           w_out, ffn_norm_g, w_router, w_gate, w_up, w_down):
    na = xa.shape[0]
    n = na + xb.shape[0]
    group_rows = n // n_groups
    cap = CAPACITY_FACTOR * group_rows // N_EXPERTS
    tm = PROJ_TM

    qg = jnp.tile(q_norm_g, ATT_W // HEAD_DIM)[None, :]
    kg = jnp.tile(k_norm_g, ATT_W // HEAD_DIM)[None, :]
    outs = _proj_call(xa, xb, seqs, mix_norm_g[None, :], w_in.astype(BF16), qg, kg, conv_w,
                      w_conv_branch.astype(BF16))
    sga, gcb = outs[9], outs[10]
    o = []
    for gi, r in enumerate(DILATIONS):
        q, k, v = outs[3 * gi:3 * gi + 3]
        if r == 1:
            q, k, v = q[None], k[None], v[None]
        o.append(_attn_call(q, k, v, seqs, gi))

    wr = jnp.pad(w_router, ((0, 0), (0, LANES - N_EXPERTS)))
    wrh = wr.astype(BF16)
    wrl = (wr - wrh.astype(F32)).astype(BF16)
    x1, hn2, aff, afft = _merge_call(o[0], o[1], o[2], sga, gcb, xa, xb, w_attn_branch.astype(BF16),
                                     w_out.astype(BF16), ffn_norm_g[None, :], wrh, wrl)

    slot, offs, slot_tm = _route_call(afft, n_groups, cap)

    cpt = tm // LANES
    tiles_per_group = group_rows // tm
    starts = offs[::cpt, :, 0].reshape(n_groups, tiles_per_group, N_EXPERTS)
    ends = jnp.concatenate([starts[:, 1:], jnp.full((n_groups, 1, N_EXPERTS), cap, I32)], axis=1)
    xe = _dispatch_call(starts, ends, hn2, slot, n_groups, cap)
    y = _ffn_call(xe, w_gate, w_up, w_down, cap)

    astart = jnp.minimum((starts // 16) * 16, cap - COMB_W)
    nrounds = jnp.maximum(jnp.max((ends - astart + COMB_W - 1) // COMB_W, axis=-1), 1)
    return _combine_call(astart.reshape(-1), nrounds.reshape(-1), x1, aff, slot_tm, y, n_groups, cap, na)


def _run(x_prompt, x_sample, params):
    bp, sp, d = x_prompt.shape
    bs, ss, _ = x_sample.shape
    seqs, pos = [], 0
    for b, s in ((bp, sp), (bs, ss)):
        for _ in range(b):
            seqs.append((pos, s))
            pos += s
    assert bp * sp == bs * ss, "request groups are routed with one capacity"
    xa, xb = x_prompt.reshape(bp * sp, d), x_sample.reshape(bs * ss, d)
    depth = params[0].shape[0]
    for layer in range(depth):
        xa, xb = _layer(xa, xb, tuple(seqs), 2, *[p[layer] for p in params])
    return xa.reshape(bp, sp, d), xb.reshape(bs, ss, d)


def kernel(x_prompt, x_sample, mix_norm_g, w_in, q_norm_g, k_norm_g, conv_w, w_attn_branch, w_conv_branch, w_out,
           ffn_norm_g, w_router, w_gate, w_up, w_down):
    params = (mix_norm_g, w_in, q_norm_g, k_norm_g, conv_w, w_attn_branch, w_conv_branch, w_out,
              ffn_norm_g, w_router, w_gate, w_up, w_down)
    return _run(x_prompt, x_sample, params)
```

```python
import functools

import numpy as np
import jax
import jax.numpy as jnp
from jax import lax
from jax.experimental import pallas as pl
from jax.experimental.pallas import tpu as pltpu

F32 = jnp.float32
BF16 = jnp.bfloat16
I32 = jnp.int32

D_MODEL = 1024
HEAD_DIM = 64
N_SLOTS = 8
DILATIONS = (1, 4, 16)
KEYS_PER_SIDE = 64
GROUP_W = N_SLOTS * HEAD_DIM
ATT_W = len(DILATIONS) * GROUP_W
CONV_W = D_MODEL
IN_W = 3 * ATT_W + 3 * CONV_W + 2 * D_MODEL
N_EXPERTS = 16
EXPERT_FF = 2048
CAPACITY_FACTOR = 2
RMS_EPS = 1e-6
NEG = -1e30

LANES = 128
PAIR_SLABS = GROUP_W // LANES
VMEM_LIMIT = 58 * 1024 * 1024

PROJ_ROWS = 256
PROJ_TM = 256
MERGE_TM = 256
PROJ_HALO = 16
ATT_TB_MAX = 2048
ATT_TQ = 128
ATT_UNROLL = 16
FFN_FC = 512
FFN_RB = 512
DISP_SB = 2048
DISP_EG = 4
DISP_W = 96
COMB_W = 64
COMB_W_LOG2 = 6


def _cparams(sem):
    return pltpu.CompilerParams(dimension_semantics=sem, vmem_limit_bytes=VMEM_LIMIT)


def _any_eq(i, values):
    return functools.reduce(jnp.logical_or, [i == v for v in values])


def _const_spec(shape):
    nd = len(shape)
    return pl.BlockSpec(shape, lambda *a: (0,) * nd, pipeline_mode=pl.Buffered(1))


def _two_array_specs(block_rows, n_first_blocks, cols, prefetch=False):
    if prefetch:
        first = pl.BlockSpec((block_rows, cols), lambda i, *_: (jnp.minimum(i, n_first_blocks - 1), 0))
        second = pl.BlockSpec((block_rows, cols), lambda i, *_: (jnp.maximum(i - n_first_blocks, 0), 0))
    else:
        first = pl.BlockSpec((block_rows, cols), lambda i: (jnp.minimum(i, n_first_blocks - 1), 0))
        second = pl.BlockSpec((block_rows, cols), lambda i: (jnp.maximum(i - n_first_blocks, 0), 0))
    return first, second


def _proj_body(xa_ref, xb_ref, xpa_ref, xpb_ref, xna_ref, xnb_ref, ng_ref, w_ref, qg_ref, kg_ref, cw_ref, wcb_ref,
               q0_ref, k0_ref, v0_ref, q1_ref, k1_ref, v1_ref, q2_ref, k2_ref, v2_ref, sga_ref, gcb_ref, hs, hp,
               *, tm, halo, n_first, first_tiles, last_tiles):
    i = pl.program_id(0)
    is_first = _any_eq(i, first_tiles)
    is_last = _any_eq(i, last_tiles)
    in_a = i < n_first

    def pick(a_ref, b_ref):
        return jnp.where(in_a, a_ref[...], b_ref[...])

    x_ext = jnp.concatenate([pick(xpa_ref, xpb_ref), pick(xa_ref, xb_ref), pick(xna_ref, xnb_ref)], axis=0)
    ms = jnp.mean(x_ext * x_ext, axis=-1, keepdims=True)
    hn_f32 = x_ext * lax.rsqrt(ms + RMS_EPS) * ng_ref[...]
    hn_ext = hn_f32.astype(BF16)
    hn = hn_ext[halo:halo + tm]

    for sd in range(D_MODEL // LANES):
        hs[sd] = hn_f32[halo:halo + tm, sd * LANES:(sd + 1) * LANES]
    hn_of_group = [hn]
    for gi, r in enumerate(DILATIONS):
        if r == 1:
            continue
        for c in range(r):
            for sd in range(D_MODEL // LANES):
                hp[gi - 1, c * (tm // r):(c + 1) * (tm // r), sd * LANES:(sd + 1) * LANES] = (
                    hs[sd, pl.ds(c, tm // r, stride=r), :].astype(BF16))
        hn_of_group.append(hp[gi - 1])

    def proj(lhs, a, b):
        return jnp.dot(lhs, w_ref[:, a:b], preferred_element_type=F32)

    lo_lane = lax.broadcasted_iota(I32, (1, LANES), 1) < HEAD_DIM

    def head_norm(p, g_ref, col0, scale):
        out = []
        for s in range(PAIR_SLABS):
            xs = p[:, s * LANES:(s + 1) * LANES]
            x2 = xs * xs
            tot = jnp.sum(x2, axis=-1, keepdims=True)
            lo = jnp.sum(jnp.where(lo_lane, x2, 0.0), axis=-1, keepdims=True)
            hi = tot - lo
            r = jnp.where(lo_lane, lax.rsqrt(lo * (1.0 / HEAD_DIM) + RMS_EPS),
                          lax.rsqrt(hi * (1.0 / HEAD_DIM) + RMS_EPS))
            y = xs * r * g_ref[:, col0 + s * LANES:col0 + (s + 1) * LANES]
            out.append(y * scale if scale != 1.0 else y)
        return out

    def emit(gi, out_ref, slabs):
        r = DILATIONS[gi]
        for s in range(PAIR_SLABS):
            y = slabs[s].astype(out_ref.dtype)
            if r == 1:
                out_ref[:, s * LANES:(s + 1) * LANES] = y
            else:
                for c in range(r):
                    out_ref[c, :, s * LANES:(s + 1) * LANES] = y[c * (tm // r):(c + 1) * (tm // r)]

    q_refs, k_refs, v_refs = (q0_ref, q1_ref, q2_ref), (k0_ref, k1_ref, k2_ref), (v0_ref, v1_ref, v2_ref)
    for gi in range(len(DILATIONS)):
        c0 = gi * GROUP_W
        lhs = hn_of_group[gi]
        emit(gi, q_refs[gi], head_norm(proj(lhs, c0, c0 + GROUP_W), qg_ref, c0, HEAD_DIM ** -0.5))
        emit(gi, k_refs[gi], head_norm(proj(lhs, ATT_W + c0, ATT_W + c0 + GROUP_W), kg_ref, c0, 1.0))
        pv = proj(lhs, 2 * ATT_W + c0, 2 * ATT_W + c0 + GROUP_W)
        emit(gi, v_refs[gi], [pv[:, s * LANES:(s + 1) * LANES] for s in range(PAIR_SLABS)])

    c0 = 3 * ATT_W
    cb = proj(hn, c0, c0 + CONV_W)
    cc = proj(hn_ext, c0 + CONV_W, c0 + 2 * CONV_W)
    cx = proj(hn_ext, c0 + 2 * CONV_W, c0 + 3 * CONV_W)
    u_ext = cc * cx
    row = lax.broadcasted_iota(I32, (tm + 2 * halo, 1), 0)
    keep = jnp.logical_and(jnp.logical_or(row >= halo, jnp.logical_not(is_first)),
                           jnp.logical_or(row < halo + tm, jnp.logical_not(is_last)))
    u_ext = jnp.where(keep, u_ext, 0.0)
    cy = (u_ext[halo - 1:halo - 1 + tm] * cw_ref[0:1, :]
          + u_ext[halo:halo + tm] * cw_ref[1:2, :]
          + u_ext[halo + 1:halo + 1 + tm] * cw_ref[2:3, :])
    conv_in = (cb * cy).astype(BF16)
    cbr = jnp.dot(conv_in, wcb_ref[...], preferred_element_type=F32)
    g0 = c0 + 3 * CONV_W
    sga_ref[...] = jax.nn.sigmoid(proj(hn, g0, g0 + D_MODEL))
    gcb_ref[...] = jax.nn.sigmoid(proj(hn, g0 + D_MODEL, g0 + 2 * D_MODEL)) * cbr


def _proj_call(xa, xb, seqs, ng, w_in, qg, kg, cw, wcb):
    na, nb_rows = xa.shape[0], xb.shape[0]
    n = na + nb_rows
    tm, halo = PROJ_ROWS, PROJ_HALO
    nt = n // tm
    n_first = na // tm
    hb = tm // halo
    first_tiles = tuple(s // tm for s, _ in seqs)
    last_tiles = tuple((s + l) // tm - 1 for s, l in seqs)
    nha, nhb = na // halo, nb_rows // halo
    body = functools.partial(_proj_body, tm=tm, halo=halo, n_first=n_first,
                             first_tiles=first_tiles, last_tiles=last_tiles)
    cur_a, cur_b = _two_array_specs(tm, n_first, D_MODEL)
    prev_a = pl.BlockSpec((halo, D_MODEL), lambda i: (jnp.clip(i * hb - 1, 0, nha - 1), 0))
    prev_b = pl.BlockSpec((halo, D_MODEL), lambda i: (jnp.clip((i - n_first) * hb - 1, 0, nhb - 1), 0))
    next_a = pl.BlockSpec((halo, D_MODEL), lambda i: (jnp.clip((i + 1) * hb, 0, nha - 1), 0))
    next_b = pl.BlockSpec((halo, D_MODEL), lambda i: (jnp.clip((i + 1 - n_first) * hb, 0, nhb - 1), 0))
    out_specs, out_shape = [], []
    for gi, r in enumerate(DILATIONS):
        for _ in range(3):
            if r == 1:
                out_specs.append(pl.BlockSpec((tm, GROUP_W), lambda i: (i, 0)))
                out_shape.append(jax.ShapeDtypeStruct((n, GROUP_W), BF16))
            else:
                out_specs.append(pl.BlockSpec((r, tm // r, GROUP_W), lambda i: (0, i, 0)))
                out_shape.append(jax.ShapeDtypeStruct((r, n // r, GROUP_W), BF16))
    out_specs += [pl.BlockSpec((tm, D_MODEL), lambda i: (i, 0))] * 2
    out_shape += [jax.ShapeDtypeStruct((n, D_MODEL), F32)] * 2
    return pl.pallas_call(
        body,
        grid=(nt,),
        in_specs=[cur_a, cur_b, prev_a, prev_b, next_a, next_b,
                  _const_spec((1, D_MODEL)), _const_spec((D_MODEL, IN_W)), _const_spec((1, ATT_W)),
                  _const_spec((1, ATT_W)), _const_spec((3, CONV_W)), _const_spec((CONV_W, D_MODEL))],
        out_specs=out_specs,
        out_shape=out_shape,
        scratch_shapes=[pltpu.VMEM((D_MODEL // LANES, tm, LANES), F32),
                        pltpu.VMEM((len(DILATIONS) - 1, tm, D_MODEL), BF16)],
        compiler_params=_cparams(("arbitrary",)),
        name="proj",
    )(xa, xb, xa, xb, xa, xb, ng, w_in, qg, kg, cw, wcb)


def _attn_body(q_ref, kp_ref, kc_ref, kn_ref, vp_ref, vc_ref, vn_ref, bias_ref, o_ref, kw, vw,
               *, tb, first_tiles, last_tiles):
    i = pl.program_id(1)
    is_first = _any_eq(i, first_tiles)
    is_last = _any_eq(i, last_tiles)
    h = KEYS_PER_SIDE
    tq = ATT_TQ
    kwin = tq + 2 * h
    nq = tb // tq

    for s in range(PAIR_SLABS):
        sl = slice(s * LANES, (s + 1) * LANES)
        for dst, p_ref, c_ref, n_ref in ((kw, kp_ref, kc_ref, kn_ref), (vw, vp_ref, vc_ref, vn_ref)):
            dst[s, 0:h, :] = p_ref[:, sl]
            dst[s, h:h + tb, :] = c_ref[:, sl]
            dst[s, h + tb:2 * h + tb, :] = n_ref[:, sl]

    lane = lax.broadcasted_iota(I32, (1, LANES), 1)
    lo_lane = lane < HEAD_DIM
    col = lax.broadcasted_iota(I32, (1, kwin), 1)

    def head_pair(q2, k2, v2, biases):
        o_acc = None
        lse2 = None
        for half in range(2):
            mask = lo_lane if half == 0 else jnp.logical_not(lo_lane)
            qm = jnp.where(mask, q2, jnp.zeros_like(q2))
            sc = lax.dot_general(qm, k2, (((1,), (1,)), ((), ())), preferred_element_type=F32)
            sc = sc + biases[half]
            m = jnp.max(sc, axis=-1, keepdims=True)
            p = jnp.exp(sc - m)
            l = jnp.sum(p, axis=-1, keepdims=True)
            vm = jnp.where(mask, v2, jnp.zeros_like(v2))
            pv = jnp.dot(p.astype(BF16), vm, preferred_element_type=F32)
            contrib = pv * (1.0 / l)
            lse = m + jnp.log(l)
            o_acc = contrib if o_acc is None else o_acc + contrib
            lse2 = jnp.broadcast_to(lse, (tq, LANES)) if lse2 is None else jnp.where(mask, lse, lse2)
        return o_acc, lse2

    def step(qb, carry):
        pen_lo = jnp.where(jnp.logical_and(is_first, qb == 0), NEG, 0.0)
        pen_hi = jnp.where(jnp.logical_and(is_last, qb == nq - 1), NEG, 0.0)
        edge = (jnp.where(col < h, pen_lo, 0.0) + jnp.where(col >= h + tq, pen_hi, 0.0))
        q0 = pl.multiple_of(qb * tq, tq)
        for s in range(PAIR_SLABS):
            biases = [bias_ref[2 * s] + edge, bias_ref[2 * s + 1] + edge]
            o2, lse2 = head_pair(q_ref[pl.ds(q0, tq), s * LANES:(s + 1) * LANES],
                                 kw[s, pl.ds(q0, kwin), :], vw[s, pl.ds(q0, kwin), :], biases)
            o_ref[s, pl.ds(q0, tq), :] = o2
            o_ref[PAIR_SLABS + s, pl.ds(q0, tq), :] = lse2
        return carry

    lax.fori_loop(0, nq, step, 0, unroll=min(ATT_UNROLL, nq))


def _alibi_bias(gi, r):
    n_heads = len(DILATIONS) * N_SLOTS
    slopes = (2.0 ** (-8.0 * (np.arange(n_heads, dtype=np.float32) + 1.0) / n_heads)).astype(np.float32)
    slopes = slopes.reshape(len(DILATIONS), N_SLOTS)[gi]
    qi = np.arange(ATT_TQ)[:, None]
    kj = np.arange(ATT_TQ + 2 * KEYS_PER_SIDE)[None, :]
    off = kj - KEYS_PER_SIDE - qi
    dist = (r * np.abs(off)).astype(np.float32)
    bias = -slopes[:, None, None] * dist[None]
    bias = np.where((np.abs(off) <= KEYS_PER_SIDE)[None], bias, np.float32(NEG))
    return jnp.asarray(bias, F32)


def _attn_call(q, k, v, seqs, gi):
    r, rows, _ = q.shape
    h = KEYS_PER_SIDE
    tb = min([ATT_TB_MAX] + [l // r for _, l in seqs])
    assert all((s // r) % tb == 0 and (l // r) % tb == 0 for s, l in seqs)
    nt = rows // tb
    hb = tb // h
    nhb = rows // h
    first_tiles = tuple((s // r) // tb for s, _ in seqs)
    last_tiles = tuple(((s + l) // r) // tb - 1 for s, l in seqs)
    kwin = ATT_TQ + 2 * h
    body = functools.partial(_attn_body, tb=tb, first_tiles=first_tiles, last_tiles=last_tiles)
    sq = pl.Squeezed()
    cur = pl.BlockSpec((sq, tb, GROUP_W), lambda c, i: (c, i, 0))
    prev = pl.BlockSpec((sq, h, GROUP_W), lambda c, i: (c, jnp.maximum(i * hb - 1, 0), 0))
    nxt = pl.BlockSpec((sq, h, GROUP_W), lambda c, i: (c, jnp.minimum((i + 1) * hb, nhb - 1), 0))
    return pl.pallas_call(
        body,
        grid=(r, nt),
        in_specs=[cur, prev, cur, nxt, prev, cur, nxt, _const_spec((N_SLOTS, ATT_TQ, kwin))],
        out_specs=pl.BlockSpec((2 * PAIR_SLABS, sq, tb, LANES), lambda c, i: (0, c, i, 0)),
        out_shape=jax.ShapeDtypeStruct((2 * PAIR_SLABS, r, rows, LANES), F32),
        scratch_shapes=[pltpu.VMEM((PAIR_SLABS, tb + 2 * h, LANES), BF16),
                        pltpu.VMEM((PAIR_SLABS, tb + 2 * h, LANES), BF16)],
        compiler_params=_cparams(("arbitrary", "arbitrary")),
        name=f"attn_d{r}",
    )(q, k, k, k, v, v, v, _alibi_bias(gi, r))


def _split_bf16(x):
    hi = x.astype(BF16)
    lo = (x - hi.astype(F32)).astype(BF16)
    return hi, lo


def _merge_body(o0_ref, o1_ref, o2_ref, sga_ref, gcb_ref, xa_ref, xb_ref, wab_ref, wout_ref, g2_ref,
                wrh_ref, wrl_ref, x1_ref, hn_ref, aff_ref, afft_ref, t1, t2, *, tm, n_first):
    i = pl.program_id(0)
    for o_ref, scr, r in ((o1_ref, t1, DILATIONS[1]), (o2_ref, t2, DILATIONS[2])):
        for s in range(2 * PAIR_SLABS):
            for c in range(r):
                scr[s, pl.ds(c, tm // r, stride=r), :] = o_ref[s, c]
    slabs = []
    for s in range(PAIR_SLABS):
        l0, l1, l2 = o0_ref[PAIR_SLABS + s, 0], t1[PAIR_SLABS + s], t2[PAIR_SLABS + s]
        mx = jnp.maximum(jnp.maximum(l0, l1), l2)
        e0, e1, e2 = jnp.exp(l0 - mx), jnp.exp(l1 - mx), jnp.exp(l2 - mx)
        num = e0 * o0_ref[s, 0] + e1 * t1[s] + e2 * t2[s]
        slabs.append(num / (e0 + e1 + e2))
    att = jnp.concatenate(slabs, axis=-1).astype(BF16)
    a_br = jnp.dot(att, wab_ref[...], preferred_element_type=F32)
    merged = (sga_ref[...] * a_br + gcb_ref[...]).astype(BF16)
    x = jnp.where(i < n_first, xa_ref[...], xb_ref[...])
    x1 = x + jnp.dot(merged, wout_ref[...], preferred_element_type=F32)
    x1_ref[...] = x1
    ms = jnp.mean(x1 * x1, axis=-1, keepdims=True)
    hn2 = x1 * lax.rsqrt(ms + RMS_EPS) * g2_ref[...]
    hn_ref[...] = hn2.astype(BF16)
    hh, hl = _split_bf16(hn2)
    logits = (jnp.dot(hh, wrh_ref[...], preferred_element_type=F32)
              + jnp.dot(hl, wrh_ref[...], preferred_element_type=F32)
              + jnp.dot(hh, wrl_ref[...], preferred_element_type=F32))
    lane = lax.broadcasted_iota(I32, (1, LANES), 1)
    logits = jnp.where(lane < N_EXPERTS, logits, NEG)
    e = jnp.exp(logits - jnp.max(logits, axis=-1, keepdims=True))
    aff = e / jnp.sum(e, axis=-1, keepdims=True)
    aff_ref[...] = aff
    aff_t = aff.T
    for c in range(tm // LANES):
        afft_ref[c] = aff_t[0:N_EXPERTS, c * LANES:(c + 1) * LANES]


def _merge_call(o0, o1, o2, sga, gcb, xa, xb, wab, wout, g2, wrh, wrl):
    n = sga.shape[0]
    tm = MERGE_TM
    nt = n // tm
    n_first = xa.shape[0] // tm
    body = functools.partial(_merge_body, tm=tm, n_first=n_first)
    ns = 2 * PAIR_SLABS
    o_specs = [pl.BlockSpec((ns, r, tm // r, LANES), lambda i: (0, 0, i, 0)) for r in DILATIONS]
    rowspec = pl.BlockSpec((tm, D_MODEL), lambda i: (i, 0))
    xa_spec, xb_spec = _two_array_specs(tm, n_first, D_MODEL)
    return pl.pallas_call(
        body,
        grid=(nt,),
        in_specs=o_specs + [rowspec, rowspec, xa_spec, xb_spec,
                            _const_spec((GROUP_W, D_MODEL)), _const_spec((D_MODEL, D_MODEL)),
                            _const_spec((1, D_MODEL)), _const_spec((D_MODEL, LANES)),
                            _const_spec((D_MODEL, LANES))],
        out_specs=[rowspec, rowspec,
                   pl.BlockSpec((tm, LANES), lambda i: (i, 0)),
                   pl.BlockSpec((tm // LANES, N_EXPERTS, LANES), lambda i: (i, 0, 0))],
        out_shape=[jax.ShapeDtypeStruct((n, D_MODEL), F32),
                   jax.ShapeDtypeStruct((n, D_MODEL), BF16),
                   jax.ShapeDtypeStruct((n, LANES), F32),
                   jax.ShapeDtypeStruct((n // LANES, N_EXPERTS, LANES), F32)],
        scratch_shapes=[pltpu.VMEM((ns, tm, LANES), F32), pltpu.VMEM((ns, tm, LANES), F32)],
        compiler_params=_cparams(("arbitrary",)),
        name="merge",
    )(o0, o1, o2, sga, gcb, xa, xb, wab, wout, g2, wrh, wrl)


def _route_body(aff_ref, tri_ref, ones_ref, low_ref, slot_ref, offs_ref, slot_tm_ref, rs_scr, off_scr,
                *, nch, cap):
    a = aff_ref[...]

    def count(mask):
        c = jnp.sum(mask.astype(F32), axis=0, keepdims=True)
        return jnp.sum(c, axis=2, keepdims=True)

    cur = jnp.zeros((1, N_EXPERTS, 1), I32)
    for bit in range(30, -1, -1):
        cand = cur | (1 << bit)
        cur = jnp.where(count(a >= lax.bitcast_convert_type(cand, F32)) >= cap, cand, cur)
    tau = lax.bitcast_convert_type(cur, F32)
    gt = a > tau
    eq = a == tau
    need = cap - count(gt)

    def prefix(mask):
        m = mask.astype(BF16).reshape(nch * N_EXPERTS, LANES)
        incl = jnp.dot(m, tri_ref[...], preferred_element_type=F32).reshape(nch, N_EXPERTS, LANES)
        rs_scr[...] = jnp.dot(m, ones_ref[...], preferred_element_type=F32).reshape(nch, N_EXPERTS, LANES)
        for e in range(N_EXPERTS):
            off_scr[:, e, :] = jnp.dot(low_ref[...], rs_scr[:, e, :].astype(BF16), preferred_element_type=F32)
        off = off_scr[...]
        return incl + off - mask.astype(F32), off

    tie_rank, _ = prefix(eq)
    sel = jnp.logical_or(gt, jnp.logical_and(eq, tie_rank < need))
    pos, off = prefix(sel)
    slot = jnp.where(sel, pos, -1.0)
    slot_ref[...] = slot.astype(I32)
    offs_ref[...] = off.astype(I32)
    rs_scr[...] = slot
    pad = jnp.full((LANES - N_EXPERTS, LANES), -1.0, F32)

    def tr(c, carry):
        tile = jnp.concatenate([rs_scr[c], pad], axis=0)
        slot_tm_ref[pl.ds(pl.multiple_of(c * LANES, LANES), LANES), :] = tile.T.astype(I32)
        return carry

    lax.fori_loop(0, nch, tr, 0, unroll=8)


def _route_call(afft, n_groups, cap):
    nchunks = afft.shape[0]
    nch = nchunks // n_groups
    tri = jnp.asarray(np.triu(np.ones((LANES, LANES), np.float32)), BF16)
    ones = jnp.ones((LANES, LANES), BF16)
    low = jnp.asarray(np.tril(np.ones((nch, nch), np.float32), -1), BF16)
    body = functools.partial(_route_body, nch=nch, cap=cap)
    blk = pl.BlockSpec((nch, N_EXPERTS, LANES), lambda b: (b, 0, 0))
    return pl.pallas_call(
        body,
        grid=(n_groups,),
        in_specs=[blk, _const_spec((LANES, LANES)), _const_spec((LANES, LANES)), _const_spec((nch, nch))],
        out_specs=[blk, blk, pl.BlockSpec((nch * LANES, LANES), lambda b: (b, 0))],
        out_shape=[jax.ShapeDtypeStruct((nchunks, N_EXPERTS, LANES), I32),
                   jax.ShapeDtypeStruct((nchunks, N_EXPERTS, LANES), I32),
                   jax.ShapeDtypeStruct((nchunks * LANES, LANES), I32)],
        scratch_shapes=[pltpu.VMEM((nch, N_EXPERTS, LANES), F32), pltpu.VMEM((nch, N_EXPERTS, LANES), F32)],
        compiler_params=_cparams(("arbitrary",)),
        name="route",
    )(afft, tri, ones, low)


def _dispatch_body(starts_ref, ends_ref, dense_ref, hn_ref, slot_ref, x_ref, *, tm, sb, tiles_per_group):
    b, eg, j = pl.program_id(0), pl.program_id(1), pl.program_id(2)
    w = DISP_W

    @pl.when(j == 0)
    def _():
        x_ref[...] = jnp.zeros_like(x_ref)

    iota_w = lax.broadcasted_iota(I32, (w, 1), 0)
    cpt = tm // LANES

    def window(t, ee):
        e = eg * DISP_EG + ee
        idx = (b * tiles_per_group + j * (sb // tm) + t) * N_EXPERTS + e
        base = pl.multiple_of((starts_ref[idx] // 16) * 16, 16)
        srow = jnp.concatenate([slot_ref[t * cpt + c, pl.ds(e, 1), :] for c in range(cpt)], axis=1)
        return base, ends_ref[idx], srow

    for t in range(sb // tm):
        wins = [window(t, ee) for ee in range(DISP_EG)]
        oh = jnp.concatenate([(srow - base == iota_w).astype(BF16) for base, _, srow in wins], axis=0)
        res = jnp.dot(oh, hn_ref[t * tm:(t + 1) * tm, :], preferred_element_type=F32)
        for ee, (base, _, _) in enumerate(wins):
            x_ref[ee, pl.ds(base, w), :] += res[ee * w:(ee + 1) * w].astype(x_ref.dtype)

    @pl.when(dense_ref[(b * pl.num_programs(1) + eg) * pl.num_programs(2) + j] > 0)
    def _():
        def tile_body(t, carry):
            hn_t = hn_ref[pl.ds(pl.multiple_of(t * tm, tm), tm), :]
            for ee in range(DISP_EG):
                base, end, srow = window(t, ee)

                def more(k, c2, ee=ee, base=base, srow=srow):
                    bk = pl.multiple_of(base + k * w, 16)
                    ohk = (srow - bk == iota_w).astype(BF16)
                    x_ref[ee, pl.ds(bk, w), :] += jnp.dot(ohk, hn_t,
                                                          preferred_element_type=F32).astype(x_ref.dtype)
                    return c2

                lax.fori_loop(1, (end - base + w - 1) // w, more, 0)
            return carry

        lax.fori_loop(0, sb // tm, tile_body, 0)


def _dispatch_call(starts, ends, hn, slot, n_groups, cap):
    n = hn.shape[0]
    ng_rows = n // n_groups
    tm, sb = PROJ_TM, DISP_SB
    nsb = ng_rows // sb
    tiles_per_group = ng_rows // tm
    cps = sb // LANES
    n_eg = N_EXPERTS // DISP_EG
    rounds = (ends - (starts // 16) * 16 + DISP_W - 1) // DISP_W
    dense = jnp.max(rounds.reshape(n_groups, nsb, sb // tm, n_eg, DISP_EG), axis=(2, 4)) > 1
    dense = jnp.transpose(dense, (0, 2, 1)).astype(I32).reshape(-1)
    starts, ends = starts.reshape(-1), ends.reshape(-1)
    body = functools.partial(_dispatch_body, tm=tm, sb=sb, tiles_per_group=tiles_per_group)
    sq = pl.Squeezed()
    grid_spec = pltpu.PrefetchScalarGridSpec(
        num_scalar_prefetch=3,
        grid=(n_groups, N_EXPERTS // DISP_EG, nsb),
        in_specs=[pl.BlockSpec((sb, D_MODEL), lambda b, g, j, *_: (b * nsb + j, 0)),
                  pl.BlockSpec((cps, N_EXPERTS, LANES), lambda b, g, j, *_: (b * nsb + j, 0, 0))],
        out_specs=pl.BlockSpec((sq, DISP_EG, cap + DISP_W, D_MODEL), lambda b, g, j, *_: (b, g, 0, 0)),
    )
    return pl.pallas_call(
        body,
        grid_spec=grid_spec,
        out_shape=jax.ShapeDtypeStruct((n_groups, N_EXPERTS, cap + DISP_W, D_MODEL), BF16),
        compiler_params=_cparams(("arbitrary", "arbitrary", "arbitrary")),
        name="dispatch",
    )(starts, ends, dense, hn, slot)


def _ffn_body(x_ref, wg_ref, wu_ref, wd_ref, y_ref, acc, *, cap):
    f = pl.program_id(2)
    nf = pl.num_programs(2)

    @pl.when(f == 0)
    def _():
        acc[...] = jnp.zeros_like(acc)

    wg = wg_ref[...].astype(BF16)
    wu = wu_ref[...].astype(BF16)
    wd = wd_ref[...].astype(BF16)
    for rb in range(cap // FFN_RB):
        rows = slice(rb * FFN_RB, (rb + 1) * FFN_RB)
        xb = x_ref[rows, :]
        g = jnp.dot(xb, wg, preferred_element_type=F32)
        u = jnp.dot(xb, wu, preferred_element_type=F32)
        hid = (g * jax.nn.sigmoid(g) * u).astype(BF16)
        acc[rows, :] += jnp.dot(hid, wd, preferred_element_type=F32)

    @pl.when(f == nf - 1)
    def _():
        y_ref[...] = acc[...].astype(y_ref.dtype)


def _ffn_call(xe, w_gate, w_up, w_down, cap):
    n_groups = xe.shape[0]
    fc = FFN_FC
    nf = EXPERT_FF // fc
    body = functools.partial(_ffn_body, cap=cap)
    sq = pl.Squeezed()
    return pl.pallas_call(
        body,
        grid=(n_groups, N_EXPERTS, nf),
        in_specs=[pl.BlockSpec((sq, sq, cap, D_MODEL), lambda b, e, f: (b, e, 0, 0)),
                  pl.BlockSpec((sq, D_MODEL, fc), lambda b, e, f: (e, 0, f)),
                  pl.BlockSpec((sq, D_MODEL, fc), lambda b, e, f: (e, 0, f)),
                  pl.BlockSpec((sq, fc, D_MODEL), lambda b, e, f: (e, f, 0))],
        out_specs=pl.BlockSpec((sq, sq, cap, D_MODEL), lambda b, e, f: (b, e, 0, 0)),
        out_shape=jax.ShapeDtypeStruct((n_groups, N_EXPERTS, cap, D_MODEL), BF16),
        scratch_shapes=[pltpu.VMEM((cap, D_MODEL), F32)],
        compiler_params=_cparams(("arbitrary", "arbitrary", "arbitrary")),
        name="ffn",
    )(xe, w_gate, w_up, w_down)


def _combine_body(astart_ref, nrounds_ref, x1_ref, aff_ref, slot_ref, spread_ref, y_hbm, outa_ref, outb_ref,
                  ybuf, sem, yext, sem_ext, accs, *, tm, cap, tiles_per_group, n_first):
    i = pl.program_id(0)
    nt = pl.num_programs(0)
    w = COMB_W

    def window_copy(tile, e, k, dst, dsem):
        a0 = astart_ref[tile * N_EXPERTS + e]
        a = pl.multiple_of(jnp.minimum(a0 + k * w, cap - w), 16)
        return pltpu.make_async_copy(y_hbm.at[tile // tiles_per_group, e, pl.ds(a, w), :], dst, dsem)

    def first_copy(tile, e):
        sl = tile % 2
        return window_copy(tile, e, 0, ybuf.at[sl, pl.ds(e * w, w)], sem.at[sl, e])

    @pl.when(i == 0)
    def _():
        for e in range(N_EXPERTS):
            first_copy(i, e).start()

    @pl.when(i + 1 < nt)
    def _():
        for e in range(N_EXPERTS):
            first_copy(i + 1, e).start()

    def spread(cols):
        return jnp.dot(cols.astype(BF16), spread_ref[...], preferred_element_type=F32)

    slot_t = slot_ref[...]
    enc = jnp.where(slot_t < 0, w * w - 1, slot_t)
    s_hi = spread((enc >> COMB_W_LOG2).astype(F32))
    s_lo = spread((enc & (w - 1)).astype(F32))
    gate_s = spread(aff_ref[...])
    wide = lax.broadcasted_iota(I32, (1, N_EXPERTS * w), 1)
    owner = wide >> COMB_W_LOG2
    local = wide & (w - 1)

    def expand(k, ywin):
        row0 = jnp.zeros_like(wide)
        skip = jnp.zeros_like(wide)
        for e in range(N_EXPERTS):
            lo = astart_ref[i * N_EXPERTS + e] + k * w
            a = jnp.minimum(lo, cap - w)
            row0 = jnp.where(owner == e, a, row0)
            skip = jnp.where(owner == e, lo - a, skip)
        want = row0 + local
        want_hi = jnp.where(local >= skip, want >> COMB_W_LOG2, -1).astype(F32)
        want_lo = (want & (w - 1)).astype(F32)
        match = jnp.logical_and(s_hi == want_hi, s_lo == want_lo)
        return jnp.dot(jnp.where(match, gate_s, 0.0).astype(BF16), ywin, preferred_element_type=F32)

    for e in range(N_EXPERTS):
        first_copy(i, e).wait()
    accs[...] = x1_ref[...] + expand(0, ybuf[i % 2])

    def extra_round(k, carry):
        for e in range(N_EXPERTS):
            window_copy(i, e, k, yext.at[pl.ds(e * w, w)], sem_ext.at[e]).start()
        for e in range(N_EXPERTS):
            window_copy(i, e, k, yext.at[pl.ds(e * w, w)], sem_ext.at[e]).wait()
        accs[...] += expand(k, yext[...])
        return carry

    lax.fori_loop(1, nrounds_ref[i], extra_round, 0)

    @pl.when(i < n_first)
    def _():
        outa_ref[...] = accs[...]

    @pl.when(i >= n_first)
    def _():
        outb_ref[...] = accs[...]


def _combine_call(astart, nrounds, x1, aff, slot_tm, y, n_groups, cap, na):
    n = x1.shape[0]
    tm = PROJ_TM
    nt = n // tm
    n_first = na // tm
    tiles_per_group = nt // n_groups
    body = functools.partial(_combine_body, tm=tm, cap=cap, tiles_per_group=tiles_per_group, n_first=n_first)
    outa_spec, outb_spec = _two_array_specs(tm, n_first, D_MODEL, prefetch=True)
    assert (1 << COMB_W_LOG2) == COMB_W and (cap - 1) // COMB_W < COMB_W - 1
    spread = np.zeros((LANES, N_EXPERTS * COMB_W), np.float32)
    for e in range(N_EXPERTS):
        spread[e, e * COMB_W:(e + 1) * COMB_W] = 1.0
    spread = jnp.asarray(spread, BF16)
    grid_spec = pltpu.PrefetchScalarGridSpec(
        num_scalar_prefetch=2,
        grid=(nt,),
        in_specs=[pl.BlockSpec((tm, D_MODEL), lambda i, *_: (i, 0)),
                  pl.BlockSpec((tm, LANES), lambda i, *_: (i, 0)),
                  pl.BlockSpec((tm, LANES), lambda i, *_: (i, 0)),
                  pl.BlockSpec((LANES, N_EXPERTS * COMB_W), lambda i, *_: (0, 0), pipeline_mode=pl.Buffered(1)),
                  pl.BlockSpec(memory_space=pl.ANY)],
        out_specs=[outa_spec, outb_spec],
        scratch_shapes=[pltpu.VMEM((2, N_EXPERTS * COMB_W, D_MODEL), BF16),
                        pltpu.SemaphoreType.DMA((2, N_EXPERTS)),
                        pltpu.VMEM((N_EXPERTS * COMB_W, D_MODEL), BF16),
                        pltpu.SemaphoreType.DMA((N_EXPERTS,)),
                        pltpu.VMEM((tm, D_MODEL), F32)],
    )
    return pl.pallas_call(
        body,
        grid_spec=grid_spec,
        out_shape=[jax.ShapeDtypeStruct((na, D_MODEL), F32), jax.ShapeDtypeStruct((n - na, D_MODEL), F32)],
        compiler_params=_cparams(("arbitrary",)),
        name="combine",
    )(astart, nrounds, x1, aff, slot_tm, spread, y)


def _layer(xa, xb, seqs, n_groups, mix_norm_g, w_in, q_norm_g, k_norm_g, conv_w, w_attn_branch, w_conv_branch,
           w_out, ffn_norm_g, w_router, w_gate, w_up, w_down):
    na = xa.shape[0]
    n = na + xb.shape[0]
    group_rows = n // n_groups
    cap = CAPACITY_FACTOR * group_rows // N_EXPERTS
    tm = PROJ_TM

    qg = jnp.tile(q_norm_g, ATT_W // HEAD_DIM)[None, :]
    kg = jnp.tile(k_norm_g, ATT_W // HEAD_DIM)[None, :]
    outs = _proj_call(xa, xb, seqs, mix_norm_g[None, :], w_in.astype(BF16), qg, kg, conv_w,
                      w_conv_branch.astype(BF16))
    sga, gcb = outs[9], outs[10]
    o = []
    for gi, r in enumerate(DILATIONS):
        q, k, v = outs[3 * gi:3 * gi + 3]
        if r == 1:
            q, k, v = q[None], k[None], v[None]
        o.append(_attn_call(q, k, v, seqs, gi))

    wr = jnp.pad(w_router, ((0, 0), (0, LANES - N_EXPERTS)))
    wrh = wr.astype(BF16)
    wrl = (wr - wrh.astype(F32)).astype(BF16)
    x1, hn2, aff, afft = _merge_call(o[0], o[1], o[2], sga, gcb, xa, xb, w_attn_branch.astype(BF16),
                                     w_out.astype(BF16), ffn_norm_g[None, :], wrh, wrl)

    slot, offs, slot_tm = _route_call(afft, n_groups, cap)

    cpt = tm // LANES
    tiles_per_group = group_rows // tm
    starts = offs[::cpt, :, 0].reshape(n_groups, tiles_per_group, N_EXPERTS)
    ends = jnp.concatenate([starts[:, 1:], jnp.full((n_groups, 1, N_EXPERTS), cap, I32)], axis=1)
    xe = _dispatch_call(starts, ends, hn2, slot, n_groups, cap)
    y = _ffn_call(xe, w_gate, w_up, w_down, cap)

    astart = jnp.minimum((starts // 16) * 16, cap - COMB_W)
    nrounds = jnp.maximum(jnp.max((ends - astart + COMB_W - 1) // COMB_W, axis=-1), 1)
    return _combine_call(astart.reshape(-1), nrounds.reshape(-1), x1, aff, slot_tm, y, n_groups, cap, na)


def _run(x_prompt, x_sample, params):
    bp, sp, d = x_prompt.shape
    bs, ss, _ = x_sample.shape
    seqs, pos = [], 0
    for b, s in ((bp, sp), (bs, ss)):
        for _ in range(b):
            seqs.append((pos, s))
            pos += s
    assert bp * sp == bs * ss, "request groups are routed with one capacity"
    xa, xb = x_prompt.reshape(bp * sp, d), x_sample.reshape(bs * ss, d)
    depth = params[0].shape[0]
    for layer in range(depth):
        xa, xb = _layer(xa, xb, tuple(seqs), 2, *[p[layer] for p in params])
    return xa.reshape(bp, sp, d), xb.reshape(bs, ss, d)


def kernel(x_prompt, x_sample, mix_norm_g, w_in, q_norm_g, k_norm_g, conv_w, w_attn_branch, w_conv_branch, w_out,
           ffn_norm_g, w_router, w_gate, w_up, w_down):
    params = (mix_norm_g, w_in, q_norm_g, k_norm_g, conv_w, w_attn_branch, w_conv_branch, w_out,
              ffn_norm_g, w_router, w_gate, w_up, w_down)
    return _run(x_prompt, x_sample, params)
```

```python
import functools

import numpy as np
import jax
import jax.numpy as jnp
from jax import lax
from jax.experimental import pallas as pl
from jax.experimental.pallas import tpu as pltpu

F32 = jnp.float32
BF16 = jnp.bfloat16
I32 = jnp.int32

D_MODEL = 1024
HEAD_DIM = 64
N_SLOTS = 8
DILATIONS = (1, 4, 16)
KEYS_PER_SIDE = 64
GROUP_W = N_SLOTS * HEAD_DIM
ATT_W = len(DILATIONS) * GROUP_W
CONV_W = D_MODEL
IN_W = 3 * ATT_W + 3 * CONV_W + 2 * D_MODEL
N_EXPERTS = 16
EXPERT_FF = 2048
CAPACITY_FACTOR = 2
RMS_EPS = 1e-6
NEG = -1e30

LANES = 128
PAIR_SLABS = GROUP_W // LANES
VMEM_LIMIT = 58 * 1024 * 1024

PROJ_ROWS = 256
PROJ_TM = 256
MERGE_TM = 256
PROJ_HALO = 16
ATT_TB_MAX = 2048
ATT_TQ = 128
ATT_UNROLL = 16
FFN_FC = 512
FFN_RB = 512
DISP_SB = 2048
DISP_EG = 4
DISP_W = 96
COMB_W = 64
COMB_W_LOG2 = 6


def _cparams(sem):
    return pltpu.CompilerParams(dimension_semantics=sem, vmem_limit_bytes=VMEM_LIMIT)


def _any_eq(i, values):
    return functools.reduce(jnp.logical_or, [i == v for v in values])


def _const_spec(shape):
    nd = len(shape)
    return pl.BlockSpec(shape, lambda *a: (0,) * nd, pipeline_mode=pl.Buffered(1))


def _two_array_specs(block_rows, n_first_blocks, cols, prefetch=False):
    if prefetch:
        first = pl.BlockSpec((block_rows, cols), lambda i, *_: (jnp.minimum(i, n_first_blocks - 1), 0))
        second = pl.BlockSpec((block_rows, cols), lambda i, *_: (jnp.maximum(i - n_first_blocks, 0), 0))
    else:
        first = pl.BlockSpec((block_rows, cols), lambda i: (jnp.minimum(i, n_first_blocks - 1), 0))
        second = pl.BlockSpec((block_rows, cols), lambda i: (jnp.maximum(i - n_first_blocks, 0), 0))
    return first, second


def _proj_body(xa_ref, xb_ref, xpa_ref, xpb_ref, xna_ref, xnb_ref, ng_ref, w_ref, qg_ref, kg_ref, cw_ref, wcb_ref,
               q0_ref, k0_ref, v0_ref, q1_ref, k1_ref, v1_ref, q2_ref, k2_ref, v2_ref, sga_ref, gcb_ref, hs, hp,
               *, tm, halo, n_first, first_tiles, last_tiles):
    i = pl.program_id(0)
    is_first = _any_eq(i, first_tiles)
    is_last = _any_eq(i, last_tiles)
    in_a = i < n_first

    def pick(a_ref, b_ref):
        return jnp.where(in_a, a_ref[...], b_ref[...])

    x_ext = jnp.concatenate([pick(xpa_ref, xpb_ref), pick(xa_ref, xb_ref), pick(xna_ref, xnb_ref)], axis=0)
    ms = jnp.mean(x_ext * x_ext, axis=-1, keepdims=True)
    hn_f32 = x_ext * lax.rsqrt(ms + RMS_EPS) * ng_ref[...]
    hn_ext = hn_f32.astype(BF16)
    hn = hn_ext[halo:halo + tm]

    for sd in range(D_MODEL // LANES):
        hs[sd] = hn_f32[halo:halo + tm, sd * LANES:(sd + 1) * LANES]
    hn_of_group = [hn]
    for gi, r in enumerate(DILATIONS):
        if r == 1:
            continue
        for c in range(r):
            for sd in range(D_MODEL // LANES):
                hp[gi - 1, c * (tm // r):(c + 1) * (tm // r), sd * LANES:(sd + 1) * LANES] = (
                    hs[sd, pl.ds(c, tm // r, stride=r), :].astype(BF16))
        hn_of_group.append(hp[gi - 1])

    def proj(lhs, a, b):
        return jnp.dot(lhs, w_ref[:, a:b], preferred_element_type=F32)

    lo_lane = lax.broadcasted_iota(I32, (1, LANES), 1) < HEAD_DIM

    def head_norm(p, g_ref, col0, scale):
        out = []
        for s in range(PAIR_SLABS):
            xs = p[:, s * LANES:(s + 1) * LANES]
            x2 = xs * xs
            tot = jnp.sum(x2, axis=-1, keepdims=True)
            lo = jnp.sum(jnp.where(lo_lane, x2, 0.0), axis=-1, keepdims=True)
            hi = tot - lo
            r = jnp.where(lo_lane, lax.rsqrt(lo * (1.0 / HEAD_DIM) + RMS_EPS),
                          lax.rsqrt(hi * (1.0 / HEAD_DIM) + RMS_EPS))
            y = xs * r * g_ref[:, col0 + s * LANES:col0 + (s + 1) * LANES]
            out.append(y * scale if scale != 1.0 else y)
        return out

    def emit(gi, out_ref, slabs):
        r = DILATIONS[gi]
        for s in range(PAIR_SLABS):
            y = slabs[s].astype(out_ref.dtype)
            if r == 1:
                out_ref[:, s * LANES:(s + 1) * LANES] = y
            else:
                for c in range(r):
                    out_ref[c, :, s * LANES:(s + 1) * LANES] = y[c * (tm // r):(c + 1) * (tm // r)]

    q_refs, k_refs, v_refs = (q0_ref, q1_ref, q2_ref), (k0_ref, k1_ref, k2_ref), (v0_ref, v1_ref, v2_ref)
    for gi in range(len(DILATIONS)):
        c0 = gi * GROUP_W
        lhs = hn_of_group[gi]
        emit(gi, q_refs[gi], head_norm(proj(lhs, c0, c0 + GROUP_W), qg_ref, c0, HEAD_DIM ** -0.5))
        emit(gi, k_refs[gi], head_norm(proj(lhs, ATT_W + c0, ATT_W + c0 + GROUP_W), kg_ref, c0, 1.0))
        pv = proj(lhs, 2 * ATT_W + c0, 2 * ATT_W + c0 + GROUP_W)
        emit(gi, v_refs[gi], [pv[:, s * LANES:(s + 1) * LANES] for s in range(PAIR_SLABS)])

    c0 = 3 * ATT_W
    cb = proj(hn, c0, c0 + CONV_W)
    cc = proj(hn_ext, c0 + CONV_W, c0 + 2 * CONV_W)
    cx = proj(hn_ext, c0 + 2 * CONV_W, c0 + 3 * CONV_W)
    u_ext = cc * cx
    row = lax.broadcasted_iota(I32, (tm + 2 * halo, 1), 0)
    keep = jnp.logical_and(jnp.logical_or(row >= halo, jnp.logical_not(is_first)),
                           jnp.logical_or(row < halo + tm, jnp.logical_not(is_last)))
    u_ext = jnp.where(keep, u_ext, 0.0)
    cy = (u_ext[halo - 1:halo - 1 + tm] * cw_ref[0:1, :]
          + u_ext[halo:halo + tm] * cw_ref[1:2, :]
          + u_ext[halo + 1:halo + 1 + tm] * cw_ref[2:3, :])
    conv_in = (cb * cy).astype(BF16)
    cbr = jnp.dot(conv_in, wcb_ref[...], preferred_element_type=F32)
    g0 = c0 + 3 * CONV_W
    sga_ref[...] = jax.nn.sigmoid(proj(hn, g0, g0 + D_MODEL))
    gcb_ref[...] = jax.nn.sigmoid(proj(hn, g0 + D_MODEL, g0 + 2 * D_MODEL)) * cbr


def _proj_call(xa, xb, seqs, ng, w_in, qg, kg, cw, wcb):
    na, nb_rows = xa.shape[0], xb.shape[0]
    n = na + nb_rows
    tm, halo = PROJ_ROWS, PROJ_HALO
    nt = n // tm
    n_first = na // tm
    hb = tm // halo
    first_tiles = tuple(s // tm for s, _ in seqs)
    last_tiles = tuple((s + l) // tm - 1 for s, l in seqs)
    nha, nhb = na // halo, nb_rows // halo
    body = functools.partial(_proj_body, tm=tm, halo=halo, n_first=n_first,
                             first_tiles=first_tiles, last_tiles=last_tiles)
    cur_a, cur_b = _two_array_specs(tm, n_first, D_MODEL)
    prev_a = pl.BlockSpec((halo, D_MODEL), lambda i: (jnp.clip(i * hb - 1, 0, nha - 1), 0))
    prev_b = pl.BlockSpec((halo, D_MODEL), lambda i: (jnp.clip((i - n_first) * hb - 1, 0, nhb - 1), 0))
    next_a = pl.BlockSpec((halo, D_MODEL), lambda i: (jnp.clip((i + 1) * hb, 0, nha - 1), 0))
    next_b = pl.BlockSpec((halo, D_MODEL), lambda i: (jnp.clip((i + 1 - n_first) * hb, 0, nhb - 1), 0))
    out_specs, out_shape = [], []
    for gi, r in enumerate(DILATIONS):
        for _ in range(3):
            if r == 1:
                out_specs.append(pl.BlockSpec((tm, GROUP_W), lambda i: (i, 0)))
                out_shape.append(jax.ShapeDtypeStruct((n, GROUP_W), BF16))
            else:
                out_specs.append(pl.BlockSpec((r, tm // r, GROUP_W), lambda i: (0, i, 0)))
                out_shape.append(jax.ShapeDtypeStruct((r, n // r, GROUP_W), BF16))
    out_specs += [pl.BlockSpec((tm, D_MODEL), lambda i: (i, 0))] * 2
    out_shape += [jax.ShapeDtypeStruct((n, D_MODEL), F32)] * 2
    return pl.pallas_call(
        body,
        grid=(nt,),
        in_specs=[cur_a, cur_b, prev_a, prev_b, next_a, next_b,
                  _const_spec((1, D_MODEL)), _const_spec((D_MODEL, IN_W)), _const_spec((1, ATT_W)),
                  _const_spec((1, ATT_W)), _const_spec((3, CONV_W)), _const_spec((CONV_W, D_MODEL))],
        out_specs=out_specs,
        out_shape=out_shape,
        scratch_shapes=[pltpu.VMEM((D_MODEL // LANES, tm, LANES), F32),
                        pltpu.VMEM((len(DILATIONS) - 1, tm, D_MODEL), BF16)],
        compiler_params=_cparams(("arbitrary",)),
        name="proj",
    )(xa, xb, xa, xb, xa, xb, ng, w_in, qg, kg, cw, wcb)


def _attn_body(q_ref, kp_ref, kc_ref, kn_ref, vp_ref, vc_ref, vn_ref, bias_ref, o_ref, kw, vw,
               *, tb, first_tiles, last_tiles):
    i = pl.program_id(1)
    is_first = _any_eq(i, first_tiles)
    is_last = _any_eq(i, last_tiles)
    h = KEYS_PER_SIDE
    tq = ATT_TQ
    kwin = tq + 2 * h
    nq = tb // tq

    for s in range(PAIR_SLABS):
        sl = slice(s * LANES, (s + 1) * LANES)
        for dst, p_ref, c_ref, n_ref in ((kw, kp_ref, kc_ref, kn_ref), (vw, vp_ref, vc_ref, vn_ref)):
            dst[s, 0:h, :] = p_ref[:, sl]
            dst[s, h:h + tb, :] = c_ref[:, sl]
            dst[s, h + tb:2 * h + tb, :] = n_ref[:, sl]

    lane = lax.broadcasted_iota(I32, (1, LANES), 1)
    lo_lane = lane < HEAD_DIM
    col = lax.broadcasted_iota(I32, (1, kwin), 1)

    def head_pair(q2, k2, v2, biases):
        o_acc = None
        lse2 = None
        for half in range(2):
            mask = lo_lane if half == 0 else jnp.logical_not(lo_lane)
            qm = jnp.where(mask, q2, jnp.zeros_like(q2))
            sc = lax.dot_general(qm, k2, (((1,), (1,)), ((), ())), preferred_element_type=F32)
            sc = sc + biases[half]
            m = jnp.max(sc, axis=-1, keepdims=True)
            p = jnp.exp(sc - m)
            l = jnp.sum(p, axis=-1, keepdims=True)
            vm = jnp.where(mask, v2, jnp.zeros_like(v2))
            pv = jnp.dot(p.astype(BF16), vm, preferred_element_type=F32)
            contrib = pv * (1.0 / l)
            lse = m + jnp.log(l)
            o_acc = contrib if o_acc is None else o_acc + contrib
            lse2 = jnp.broadcast_to(lse, (tq, LANES)) if lse2 is None else jnp.where(mask, lse, lse2)
        return o_acc, lse2

    def step(qb, carry):
        pen_lo = jnp.where(jnp.logical_and(is_first, qb == 0), NEG, 0.0)
        pen_hi = jnp.where(jnp.logical_and(is_last, qb == nq - 1), NEG, 0.0)
        edge = (jnp.where(col < h, pen_lo, 0.0) + jnp.where(col >= h + tq, pen_hi, 0.0))
        q0 = pl.multiple_of(qb * tq, tq)
        for s in range(PAIR_SLABS):
            biases = [bias_ref[2 * s] + edge, bias_ref[2 * s + 1] + edge]
            o2, lse2 = head_pair(q_ref[pl.ds(q0, tq), s * LANES:(s + 1) * LANES],
                                 kw[s, pl.ds(q0, kwin), :], vw[s, pl.ds(q0, kwin), :], biases)
            o_ref[s, pl.ds(q0, tq), :] = o2
            o_ref[PAIR_SLABS + s, pl.ds(q0, tq), :] = lse2
        return carry

    lax.fori_loop(0, nq, step, 0, unroll=min(ATT_UNROLL, nq))


def _alibi_bias(gi, r):
    n_heads = len(DILATIONS) * N_SLOTS
    slopes = (2.0 ** (-8.0 * (np.arange(n_heads, dtype=np.float32) + 1.0) / n_heads)).astype(np.float32)
    slopes = slopes.reshape(len(DILATIONS), N_SLOTS)[gi]
    qi = np.arange(ATT_TQ)[:, None]
    kj = np.arange(ATT_TQ + 2 * KEYS_PER_SIDE)[None, :]
    off = kj - KEYS_PER_SIDE - qi
    dist = (r * np.abs(off)).astype(np.float32)
    bias = -slopes[:, None, None] * dist[None]
    bias = np.where((np.abs(off) <= KEYS_PER_SIDE)[None], bias, np.float32(NEG))
    return jnp.asarray(bias, F32)


def _attn_call(q, k, v, seqs, gi):
    r, rows, _ = q.shape
    h = KEYS_PER_SIDE
    tb = min([ATT_TB_MAX] + [l // r for _, l in seqs])
    assert all((s // r) % tb == 0 and (l // r) % tb == 0 for s, l in seqs)
    nt = rows // tb
    hb = tb // h
    nhb = rows // h
    first_tiles = tuple((s // r) // tb for s, _ in seqs)
    last_tiles = tuple(((s + l) // r) // tb - 1 for s, l in seqs)
    kwin = ATT_TQ + 2 * h
    body = functools.partial(_attn_body, tb=tb, first_tiles=first_tiles, last_tiles=last_tiles)
    sq = pl.Squeezed()
    cur = pl.BlockSpec((sq, tb, GROUP_W), lambda c, i: (c, i, 0))
    prev = pl.BlockSpec((sq, h, GROUP_W), lambda c, i: (c, jnp.maximum(i * hb - 1, 0), 0))
    nxt = pl.BlockSpec((sq, h, GROUP_W), lambda c, i: (c, jnp.minimum((i + 1) * hb, nhb - 1), 0))
    return pl.pallas_call(
        body,
        grid=(r, nt),
        in_specs=[cur, prev, cur, nxt, prev, cur, nxt, _const_spec((N_SLOTS, ATT_TQ, kwin))],
        out_specs=pl.BlockSpec((2 * PAIR_SLABS, sq, tb, LANES), lambda c, i: (0, c, i, 0)),
        out_shape=jax.ShapeDtypeStruct((2 * PAIR_SLABS, r, rows, LANES), F32),
        scratch_shapes=[pltpu.VMEM((PAIR_SLABS, tb + 2 * h, LANES), BF16),
                        pltpu.VMEM((PAIR_SLABS, tb + 2 * h, LANES), BF16)],
        compiler_params=_cparams(("arbitrary", "arbitrary")),
        name=f"attn_d{r}",
    )(q, k, k, k, v, v, v, _alibi_bias(gi, r))


def _split_bf16(x):
    hi = x.astype(BF16)
    lo = (x - hi.astype(F32)).astype(BF16)
    return hi, lo


def _merge_body(o0_ref, o1_ref, o2_ref, sga_ref, gcb_ref, xa_ref, xb_ref, wab_ref, wout_ref, g2_ref,
                wrh_ref, wrl_ref, x1_ref, hn_ref, aff_ref, afft_ref, t1, t2, *, tm, n_first):
    i = pl.program_id(0)
    for o_ref, scr, r in ((o1_ref, t1, DILATIONS[1]), (o2_ref, t2, DILATIONS[2])):
        for s in range(2 * PAIR_SLABS):
            for c in range(r):
                scr[s, pl.ds(c, tm // r, stride=r), :] = o_ref[s, c]
    slabs = []
    for s in range(PAIR_SLABS):
        l0, l1, l2 = o0_ref[PAIR_SLABS + s, 0], t1[PAIR_SLABS + s], t2[PAIR_SLABS + s]
        mx = jnp.maximum(jnp.maximum(l0, l1), l2)
        e0, e1, e2 = jnp.exp(l0 - mx), jnp.exp(l1 - mx), jnp.exp(l2 - mx)
        num = e0 * o0_ref[s, 0] + e1 * t1[s] + e2 * t2[s]
        slabs.append(num / (e0 + e1 + e2))
    att = jnp.concatenate(slabs, axis=-1).astype(BF16)
    a_br = jnp.dot(att, wab_ref[...], preferred_element_type=F32)
    merged = (sga_ref[...] * a_br + gcb_ref[...]).astype(BF16)
    x = jnp.where(i < n_first, xa_ref[...], xb_ref[...])
    x1 = x + jnp.dot(merged, wout_ref[...], preferred_element_type=F32)
    x1_ref[...] = x1
    ms = jnp.mean(x1 * x1, axis=-1, keepdims=True)
    hn2 = x1 * lax.rsqrt(ms + RMS_EPS) * g2_ref[...]
    hn_ref[...] = hn2.astype(BF16)
    hh, hl = _split_bf16(hn2)
    both = jnp.dot(hh, wrl_ref[...], preferred_element_type=F32)
    logits = (both[:, :LANES] + jnp.dot(hl, wrh_ref[...], preferred_element_type=F32)
              + both[:, LANES:])
    lane = lax.broadcasted_iota(I32, (1, LANES), 1)
    logits = jnp.where(lane < N_EXPERTS, logits, NEG)
    e = jnp.exp(logits - jnp.max(logits, axis=-1, keepdims=True))
    aff = e / jnp.sum(e, axis=-1, keepdims=True)
    aff_ref[...] = aff
    aff_t = aff.T
    for c in range(tm // LANES):
        afft_ref[c] = aff_t[0:N_EXPERTS, c * LANES:(c + 1) * LANES]


def _merge_call(o0, o1, o2, sga, gcb, xa, xb, wab, wout, g2, wrh, wrl):
    n = sga.shape[0]
    tm = MERGE_TM
    nt = n // tm
    n_first = xa.shape[0] // tm
    body = functools.partial(_merge_body, tm=tm, n_first=n_first)
    ns = 2 * PAIR_SLABS
    o_specs = [pl.BlockSpec((ns, r, tm // r, LANES), lambda i: (0, 0, i, 0)) for r in DILATIONS]
    rowspec = pl.BlockSpec((tm, D_MODEL), lambda i: (i, 0))
    xa_spec, xb_spec = _two_array_specs(tm, n_first, D_MODEL)
    return pl.pallas_call(
        body,
        grid=(nt,),
        in_specs=o_specs + [rowspec, rowspec, xa_spec, xb_spec,
                            _const_spec((GROUP_W, D_MODEL)), _const_spec((D_MODEL, D_MODEL)),
                            _const_spec((1, D_MODEL)), _const_spec((D_MODEL, LANES)),
                            _const_spec((D_MODEL, 2 * LANES))],
        out_specs=[rowspec, rowspec,
                   pl.BlockSpec((tm, LANES), lambda i: (i, 0)),
                   pl.BlockSpec((tm // LANES, N_EXPERTS, LANES), lambda i: (i, 0, 0))],
        out_shape=[jax.ShapeDtypeStruct((n, D_MODEL), F32),
                   jax.ShapeDtypeStruct((n, D_MODEL), BF16),
                   jax.ShapeDtypeStruct((n, LANES), F32),
                   jax.ShapeDtypeStruct((n // LANES, N_EXPERTS, LANES), F32)],
        scratch_shapes=[pltpu.VMEM((ns, tm, LANES), F32), pltpu.VMEM((ns, tm, LANES), F32)],
        compiler_params=_cparams(("arbitrary",)),
        name="merge",
    )(o0, o1, o2, sga, gcb, xa, xb, wab, wout, g2, wrh, wrl)


def _route_body(aff_ref, tri_ref, ones_ref, low_ref, slot_ref, offs_ref, slot_tm_ref, rs_scr, off_scr,
                *, nch, cap):
    a = aff_ref[...]

    def count(mask):
        c = jnp.sum(mask.astype(F32), axis=0, keepdims=True)
        return jnp.sum(c, axis=2, keepdims=True)

    cur = jnp.zeros((1, N_EXPERTS, 1), I32)
    for bit in range(30, -1, -1):
        cand = cur | (1 << bit)
        cur = jnp.where(count(a >= lax.bitcast_convert_type(cand, F32)) >= cap, cand, cur)
    tau = lax.bitcast_convert_type(cur, F32)
    gt = a > tau
    eq = a == tau
    need = cap - count(gt)

    def prefix(mask):
        m = mask.astype(BF16).reshape(nch * N_EXPERTS, LANES)
        incl = jnp.dot(m, tri_ref[...], preferred_element_type=F32).reshape(nch, N_EXPERTS, LANES)
        rs_scr[...] = jnp.dot(m, ones_ref[...], preferred_element_type=F32).reshape(nch, N_EXPERTS, LANES)
        for e in range(N_EXPERTS):
            off_scr[:, e, :] = jnp.dot(low_ref[...], rs_scr[:, e, :].astype(BF16), preferred_element_type=F32)
        off = off_scr[...]
        return incl + off - mask.astype(F32), off

    tie_rank, _ = prefix(eq)
    sel = jnp.logical_or(gt, jnp.logical_and(eq, tie_rank < need))
    pos, off = prefix(sel)
    slot = jnp.where(sel, pos, -1.0)
    slot_ref[...] = slot.astype(I32)
    offs_ref[...] = off.astype(I32)
    rs_scr[...] = slot
    pad = jnp.full((LANES - N_EXPERTS, LANES), -1.0, F32)

    def tr(c, carry):
        tile = jnp.concatenate([rs_scr[c], pad], axis=0)
        slot_tm_ref[pl.ds(pl.multiple_of(c * LANES, LANES), LANES), :] = tile.T.astype(I32)
        return carry

    lax.fori_loop(0, nch, tr, 0, unroll=8)


def _route_call(afft, n_groups, cap):
    nchunks = afft.shape[0]
    nch = nchunks // n_groups
    tri = jnp.asarray(np.triu(np.ones((LANES, LANES), np.float32)), BF16)
    ones = jnp.ones((LANES, LANES), BF16)
    low = jnp.asarray(np.tril(np.ones((nch, nch), np.float32), -1), BF16)
    body = functools.partial(_route_body, nch=nch, cap=cap)
    blk = pl.BlockSpec((nch, N_EXPERTS, LANES), lambda b: (b, 0, 0))
    return pl.pallas_call(
        body,
        grid=(n_groups,),
        in_specs=[blk, _const_spec((LANES, LANES)), _const_spec((LANES, LANES)), _const_spec((nch, nch))],
        out_specs=[blk, blk, pl.BlockSpec((nch * LANES, LANES), lambda b: (b, 0))],
        out_shape=[jax.ShapeDtypeStruct((nchunks, N_EXPERTS, LANES), I32),
                   jax.ShapeDtypeStruct((nchunks, N_EXPERTS, LANES), I32),
                   jax.ShapeDtypeStruct((nchunks * LANES, LANES), I32)],
        scratch_shapes=[pltpu.VMEM((nch, N_EXPERTS, LANES), F32), pltpu.VMEM((nch, N_EXPERTS, LANES), F32)],
        compiler_params=_cparams(("arbitrary",)),
        name="route",
    )(afft, tri, ones, low)


def _dispatch_body(starts_ref, ends_ref, dense_ref, hn_ref, slot_ref, x_ref, *, tm, sb, tiles_per_group):
    b, eg, j = pl.program_id(0), pl.program_id(1), pl.program_id(2)
    w = DISP_W

    @pl.when(j == 0)
    def _():
        x_ref[...] = jnp.zeros_like(x_ref)

    iota_w = lax.broadcasted_iota(I32, (w, 1), 0)
    cpt = tm // LANES

    def window(t, ee):
        e = eg * DISP_EG + ee
        idx = (b * tiles_per_group + j * (sb // tm) + t) * N_EXPERTS + e
        base = pl.multiple_of((starts_ref[idx] // 16) * 16, 16)
        srow = jnp.concatenate([slot_ref[t * cpt + c, pl.ds(e, 1), :] for c in range(cpt)], axis=1)
        return base, ends_ref[idx], srow

    for t in range(sb // tm):
        wins = [window(t, ee) for ee in range(DISP_EG)]
        oh = jnp.concatenate([(srow - base == iota_w).astype(BF16) for base, _, srow in wins], axis=0)
        res = jnp.dot(oh, hn_ref[t * tm:(t + 1) * tm, :], preferred_element_type=F32)
        for ee, (base, _, _) in enumerate(wins):
            x_ref[ee, pl.ds(base, w), :] += res[ee * w:(ee + 1) * w].astype(x_ref.dtype)

    @pl.when(dense_ref[(b * pl.num_programs(1) + eg) * pl.num_programs(2) + j] > 0)
    def _():
        def tile_body(t, carry):
            hn_t = hn_ref[pl.ds(pl.multiple_of(t * tm, tm), tm), :]
            for ee in range(DISP_EG):
                base, end, srow = window(t, ee)

                def more(k, c2, ee=ee, base=base, srow=srow):
                    bk = pl.multiple_of(base + k * w, 16)
                    ohk = (srow - bk == iota_w).astype(BF16)
                    x_ref[ee, pl.ds(bk, w), :] += jnp.dot(ohk, hn_t,
                                                          preferred_element_type=F32).astype(x_ref.dtype)
                    return c2

                lax.fori_loop(1, (end - base + w - 1) // w, more, 0)
            return carry

        lax.fori_loop(0, sb // tm, tile_body, 0)


def _dispatch_call(starts, ends, hn, slot, n_groups, cap):
    n = hn.shape[0]
    ng_rows = n // n_groups
    tm, sb = PROJ_TM, DISP_SB
    nsb = ng_rows // sb
    tiles_per_group = ng_rows // tm
    cps = sb // LANES
    n_eg = N_EXPERTS // DISP_EG
    rounds = (ends - (starts // 16) * 16 + DISP_W - 1) // DISP_W
    dense = jnp.max(rounds.reshape(n_groups, nsb, sb // tm, n_eg, DISP_EG), axis=(2, 4)) > 1
    dense = jnp.transpose(dense, (0, 2, 1)).astype(I32).reshape(-1)
    starts, ends = starts.reshape(-1), ends.reshape(-1)
    body = functools.partial(_dispatch_body, tm=tm, sb=sb, tiles_per_group=tiles_per_group)
    sq = pl.Squeezed()
    grid_spec = pltpu.PrefetchScalarGridSpec(
        num_scalar_prefetch=3,
        grid=(n_groups, N_EXPERTS // DISP_EG, nsb),
        in_specs=[pl.BlockSpec((sb, D_MODEL), lambda b, g, j, *_: (b * nsb + j, 0)),
                  pl.BlockSpec((cps, N_EXPERTS, LANES), lambda b, g, j, *_: (b * nsb + j, 0, 0))],
        out_specs=pl.BlockSpec((sq, DISP_EG, cap + DISP_W, D_MODEL), lambda b, g, j, *_: (b, g, 0, 0)),
    )
    return pl.pallas_call(
        body,
        grid_spec=grid_spec,
        out_shape=jax.ShapeDtypeStruct((n_groups, N_EXPERTS, cap + DISP_W, D_MODEL), BF16),
        compiler_params=_cparams(("arbitrary", "arbitrary", "arbitrary")),
        name="dispatch",
    )(starts, ends, dense, hn, slot)


def _ffn_body(x_ref, wg_ref, wu_ref, wd_ref, y_ref, acc, *, cap):
    f = pl.program_id(2)
    nf = pl.num_programs(2)

    @pl.when(f == 0)
    def _():
        acc[...] = jnp.zeros_like(acc)

    wg = wg_ref[...].astype(BF16)
    wu = wu_ref[...].astype(BF16)
    wd = wd_ref[...].astype(BF16)
    for rb in range(cap // FFN_RB):
        rows = slice(rb * FFN_RB, (rb + 1) * FFN_RB)
        xb = x_ref[rows, :]
        g = jnp.dot(xb, wg, preferred_element_type=F32)
        u = jnp.dot(xb, wu, preferred_element_type=F32)
        hid = (g * jax.nn.sigmoid(g) * u).astype(BF16)
        acc[rows, :] += jnp.dot(hid, wd, preferred_element_type=F32)

    @pl.when(f == nf - 1)
    def _():
        y_ref[...] = acc[...].astype(y_ref.dtype)


def _ffn_call(xe, w_gate, w_up, w_down, cap):
    n_groups = xe.shape[0]
    fc = FFN_FC
    nf = EXPERT_FF // fc
    body = functools.partial(_ffn_body, cap=cap)
    sq = pl.Squeezed()
    return pl.pallas_call(
        body,
        grid=(n_groups, N_EXPERTS, nf),
        in_specs=[pl.BlockSpec((sq, sq, cap, D_MODEL), lambda b, e, f: (b, e, 0, 0)),
                  pl.BlockSpec((sq, D_MODEL, fc), lambda b, e, f: (e, 0, f)),
                  pl.BlockSpec((sq, D_MODEL, fc), lambda b, e, f: (e, 0, f)),
                  pl.BlockSpec((sq, fc, D_MODEL), lambda b, e, f: (e, f, 0))],
        out_specs=pl.BlockSpec((sq, sq, cap, D_MODEL), lambda b, e, f: (b, e, 0, 0)),
        out_shape=jax.ShapeDtypeStruct((n_groups, N_EXPERTS, cap, D_MODEL), BF16),
        scratch_shapes=[pltpu.VMEM((cap, D_MODEL), F32)],
        compiler_params=_cparams(("arbitrary", "arbitrary", "arbitrary")),
        name="ffn",
    )(xe, w_gate, w_up, w_down)


def _combine_body(astart_ref, nrounds_ref, x1_ref, aff_ref, slot_ref, spread_ref, y_hbm, outa_ref, outb_ref,
                  ybuf, sem, yext, sem_ext, accs, *, tm, cap, tiles_per_group, n_first):
    i = pl.program_id(0)
    nt = pl.num_programs(0)
    w = COMB_W

    def window_copy(tile, e, k, dst, dsem):
        a0 = astart_ref[tile * N_EXPERTS + e]
        a = pl.multiple_of(jnp.minimum(a0 + k * w, cap - w), 16)
        return pltpu.make_async_copy(y_hbm.at[tile // tiles_per_group, e, pl.ds(a, w), :], dst, dsem)

    def first_copy(tile, e):
        sl = tile % 2
        return window_copy(tile, e, 0, ybuf.at[sl, pl.ds(e * w, w)], sem.at[sl, e])

    @pl.when(i == 0)
    def _():
        for e in range(N_EXPERTS):
            first_copy(i, e).start()

    @pl.when(i + 1 < nt)
    def _():
        for e in range(N_EXPERTS):
            first_copy(i + 1, e).start()

    def spread(cols):
        return jnp.dot(cols.astype(BF16), spread_ref[...], preferred_element_type=F32)

    slot_t = slot_ref[...]
    enc = jnp.where(slot_t < 0, w * w - 1, slot_t)
    s_hi = spread((enc >> COMB_W_LOG2).astype(F32))
    s_lo = spread((enc & (w - 1)).astype(F32))
    gate_s = spread(aff_ref[...])
    wide = lax.broadcasted_iota(I32, (1, N_EXPERTS * w), 1)
    owner = wide >> COMB_W_LOG2
    local = wide & (w - 1)

    def expand(k, ywin):
        row0 = jnp.zeros_like(wide)
        skip = jnp.zeros_like(wide)
        for e in range(N_EXPERTS):
            lo = astart_ref[i * N_EXPERTS + e] + k * w
            a = jnp.minimum(lo, cap - w)
            row0 = jnp.where(owner == e, a, row0)
            skip = jnp.where(owner == e, lo - a, skip)
        want = row0 + local
        want_hi = jnp.where(local >= skip, want >> COMB_W_LOG2, -1).astype(F32)
        want_lo = (want & (w - 1)).astype(F32)
        match = jnp.logical_and(s_hi == want_hi, s_lo == want_lo)
        return jnp.dot(jnp.where(match, gate_s, 0.0).astype(BF16), ywin, preferred_element_type=F32)

    for e in range(N_EXPERTS):
        first_copy(i, e).wait()
    accs[...] = x1_ref[...] + expand(0, ybuf[i % 2])

    def extra_round(k, carry):
        for e in range(N_EXPERTS):
            window_copy(i, e, k, yext.at[pl.ds(e * w, w)], sem_ext.at[e]).start()
        for e in range(N_EXPERTS):
            window_copy(i, e, k, yext.at[pl.ds(e * w, w)], sem_ext.at[e]).wait()
        accs[...] += expand(k, yext[...])
        return carry

    lax.fori_loop(1, nrounds_ref[i], extra_round, 0)

    @pl.when(i < n_first)
    def _():
        outa_ref[...] = accs[...]

    @pl.when(i >= n_first)
    def _():
        outb_ref[...] = accs[...]


def _combine_call(astart, nrounds, x1, aff, slot_tm, y, n_groups, cap, na):
    n = x1.shape[0]
    tm = PROJ_TM
    nt = n // tm
    n_first = na // tm
    tiles_per_group = nt // n_groups
    body = functools.partial(_combine_body, tm=tm, cap=cap, tiles_per_group=tiles_per_group, n_first=n_first)
    outa_spec, outb_spec = _two_array_specs(tm, n_first, D_MODEL, prefetch=True)
    assert (1 << COMB_W_LOG2) == COMB_W and (cap - 1) // COMB_W < COMB_W - 1
    spread = np.zeros((LANES, N_EXPERTS * COMB_W), np.float32)
    for e in range(N_EXPERTS):
        spread[e, e * COMB_W:(e + 1) * COMB_W] = 1.0
    spread = jnp.asarray(spread, BF16)
    grid_spec = pltpu.PrefetchScalarGridSpec(
        num_scalar_prefetch=2,
        grid=(nt,),
        in_specs=[pl.BlockSpec((tm, D_MODEL), lambda i, *_: (i, 0)),
                  pl.BlockSpec((tm, LANES), lambda i, *_: (i, 0)),
                  pl.BlockSpec((tm, LANES), lambda i, *_: (i, 0)),
                  pl.BlockSpec((LANES, N_EXPERTS * COMB_W), lambda i, *_: (0, 0), pipeline_mode=pl.Buffered(1)),
                  pl.BlockSpec(memory_space=pl.ANY)],
        out_specs=[outa_spec, outb_spec],
        scratch_shapes=[pltpu.VMEM((2, N_EXPERTS * COMB_W, D_MODEL), BF16),
                        pltpu.SemaphoreType.DMA((2, N_EXPERTS)),
                        pltpu.VMEM((N_EXPERTS * COMB_W, D_MODEL), BF16),
                        pltpu.SemaphoreType.DMA((N_EXPERTS,)),
                        pltpu.VMEM((tm, D_MODEL), F32)],
    )
    return pl.pallas_call(
        body,
        grid_spec=grid_spec,
        out_shape=[jax.ShapeDtypeStruct((na, D_MODEL), F32), jax.ShapeDtypeStruct((n - na, D_MODEL), F32)],
        compiler_params=_cparams(("arbitrary",)),
        name="combine",
    )(astart, nrounds, x1, aff, slot_tm, spread, y)


def _layer(xa, xb, seqs, n_groups, mix_norm_g, w_in, q_norm_g, k_norm_g, conv_w, w_attn_branch, w_conv_branch,
           w_out, ffn_norm_g, w_router, w_gate, w_up, w_down):
    na = xa.shape[0]
    n = na + xb.shape[0]
    group_rows = n // n_groups
    cap = CAPACITY_FACTOR * group_rows // N_EXPERTS
    tm = PROJ_TM

    qg = jnp.tile(q_norm_g, ATT_W // HEAD_DIM)[None, :]
    kg = jnp.tile(k_norm_g, ATT_W // HEAD_DIM)[None, :]
    outs = _proj_call(xa, xb, seqs, mix_norm_g[None, :], w_in.astype(BF16), qg, kg, conv_w,
                      w_conv_branch.astype(BF16))
    sga, gcb = outs[9], outs[10]
    o = []
    for gi, r in enumerate(DILATIONS):
        q, k, v = outs[3 * gi:3 * gi + 3]
        if r == 1:
            q, k, v = q[None], k[None], v[None]
        o.append(_attn_call(q, k, v, seqs, gi))

    wr = jnp.pad(w_router, ((0, 0), (0, LANES - N_EXPERTS)))
    wrh = wr.astype(BF16)
    wrl = jnp.concatenate([wrh, (wr - wrh.astype(F32)).astype(BF16)], axis=1)
    x1, hn2, aff, afft = _merge_call(o[0], o[1], o[2], sga, gcb, xa, xb, w_attn_branch.astype(BF16),
                                     w_out.astype(BF16), ffn_norm_g[None, :], wrh, wrl)

    slot, offs, slot_tm = _route_call(afft, n_groups, cap)

    cpt = tm // LANES
    tiles_per_group = group_rows // tm
    starts = offs[::cpt, :, 0].reshape(n_groups, tiles_per_group, N_EXPERTS)
    ends = jnp.concatenate([starts[:, 1:], jnp.full((n_groups, 1, N_EXPERTS), cap, I32)], axis=1)
    xe = _dispatch_call(starts, ends, hn2, slot, n_groups, cap)
    y = _ffn_call(xe, w_gate, w_up, w_down, cap)

    astart = jnp.minimum((starts // 16) * 16, cap - COMB_W)
    nrounds = jnp.maximum(jnp.max((ends - astart + COMB_W - 1) // COMB_W, axis=-1), 1)
    return _combine_call(astart.reshape(-1), nrounds.reshape(-1), x1, aff, slot_tm, y, n_groups, cap, na)


def _run(x_prompt, x_sample, params):
    bp, sp, d = x_prompt.shape
    bs, ss, _ = x_sample.shape
    seqs, pos = [], 0
    for b, s in ((bp, sp), (bs, ss)):
        for _ in range(b):
            seqs.append((pos, s))
            pos += s
    assert bp * sp == bs * ss, "request groups are routed with one capacity"
    xa, xb = x_prompt.reshape(bp * sp, d), x_sample.reshape(bs * ss, d)
    depth = params[0].shape[0]
    for layer in range(depth):
        xa, xb = _layer(xa, xb, tuple(seqs), 2, *[p[layer] for p in params])
    return xa.reshape(bp, sp, d), xb.reshape(bs, ss, d)


def kernel(x_prompt, x_sample, mix_norm_g, w_in, q_norm_g, k_norm_g, conv_w, w_attn_branch, w_conv_branch, w_out,
           ffn_norm_g, w_router, w_gate, w_up, w_down):
    params = (mix_norm_g, w_in, q_norm_g, k_norm_g, conv_w, w_attn_branch, w_conv_branch, w_out,
              ffn_norm_g, w_router, w_gate, w_up, w_down)
    return _run(x_prompt, x_sample, params)
```
